```python
import jax, jax.numpy as jnp
from jax import lax
import numpy as np

D_MODEL = 1024
BATCH = 4
SEQ = 4096
DEPTH = 1
DEC_BATCH = 128
DEC_SEQ = 8
PAST_LEN = 2048
PAGE_SIZE = 128

N_HEADS = 8
N_KV_HEADS = 2
GROUP = N_HEADS // N_KV_HEADS
HEAD_DIM = D_MODEL // N_HEADS
N_IDX_HEADS = 8
IDX_DIM = 64
IDX_SCALE = (N_IDX_HEADS * IDX_DIM) ** -0.5
TOPK_MAX = 256
D_CONV = D_MODEL
CONV_WIDTH = 3
D_FF = -(-8 * D_MODEL // (3 * 256)) * 256
D_PLE = 256
ROPE_THETA = 10000.0
Q_BLOCK = 128
EPS = 1e-6
IN_SIZES = (N_HEADS * HEAD_DIM, N_KV_HEADS * HEAD_DIM, N_KV_HEADS * HEAD_DIM,
            N_IDX_HEADS * IDX_DIM, IDX_DIM, N_IDX_HEADS,
            D_CONV, D_CONV, D_CONV, D_MODEL, D_MODEL)
D_IN_TOTAL = sum(IN_SIZES)

kernel_name = "hybrid_dsa_shortconv_decoder_step"


def rmsnorm(x, g):
    xf = x.astype(jnp.float32)
    var = jnp.mean(xf * xf, axis=-1, keepdims=True)
    return (xf * lax.rsqrt(var + EPS)).astype(x.dtype) * g


def rope(x, pos):
    half = x.shape[-1] // 2
    freqs = ROPE_THETA ** (-jnp.arange(half, dtype=jnp.float32) / half)
    ang = pos.astype(jnp.float32)[:, None] * freqs[None, :]
    cos = jnp.cos(ang)[:, None, :]
    sin = jnp.sin(ang)[:, None, :]
    xf = x.astype(jnp.float32)
    x1, x2 = xf[..., :half], xf[..., half:]
    return jnp.concatenate([x1 * cos - x2 * sin, x2 * cos + x1 * sin], axis=-1).astype(x.dtype)


def mix_inputs(h, w_in, pos):
    z = h @ w_in
    offs = []
    acc = 0
    for s in IN_SIZES[:-1]:
        acc += s
        offs.append(acc)
    q, k, v, qi, ki, wi, bg, cg, xc, ga, gb = jnp.split(z, offs, axis=-1)
    B, T = h.shape[:2]
    q = rope(q.reshape(B, T, N_HEADS, HEAD_DIM), pos)
    k = rope(k.reshape(B, T, N_KV_HEADS, HEAD_DIM), pos)
    v = v.reshape(B, T, N_KV_HEADS, HEAD_DIM)
    qi = rope(qi.reshape(B, T, N_IDX_HEADS, IDX_DIM), pos)
    ki = rope(ki[:, :, None, :], pos)[:, :, 0, :]
    return q, k, v, qi, ki, wi, bg, cg, xc, ga, gb


def select_attend(q, qi, wi, q_pos, k_all, v_all, ki_all):
    B, Tq = q.shape[:2]
    L = k_all.shape[1]
    n_sel = min(TOPK_MAX, L // 4)
    causal = jnp.arange(L)[None, :] <= q_pos[:, None]
    dots = jnp.einsum('bthd,bld->bthl', qi.astype(jnp.float32), ki_all.astype(jnp.float32))
    score = jnp.einsum('bth,bthl->btl', wi.astype(jnp.float32), jax.nn.relu(dots)) * IDX_SCALE
    score = jnp.where(causal[None], score, -jnp.inf)
    _, idx = lax.top_k(score, n_sel)
    valid = idx <= q_pos[None, :, None]
    gather = jax.vmap(lambda kb, ib: kb[ib])
    k_sel = gather(k_all, idx)
    v_sel = gather(v_all, idx)
    qg = q.reshape(B, Tq, N_KV_HEADS, GROUP, HEAD_DIM)
    s = jnp.einsum('btkgd,btjkd->btkgj', qg, k_sel).astype(jnp.float32) * (HEAD_DIM ** -0.5)
    s = jnp.where(valid[:, :, None, None, :], s, -jnp.inf)
    p = jax.nn.softmax(s, axis=-1).astype(v_sel.dtype)
    o = jnp.einsum('btkgj,btjkd->btkgd', p, v_sel)
    return o.reshape(B, Tq, N_HEADS * HEAD_DIM)


def prompt_attention(q, qi, wi, k, v, ki):
    B, T = q.shape[:2]
    nb = T // Q_BLOCK

    def to_blocks(a):
        return jnp.moveaxis(a.reshape(B, nb, Q_BLOCK, *a.shape[2:]), 1, 0)

    def blk(args):
        qb, qib, wib, start = args
        pos = start + jnp.arange(Q_BLOCK)
        return select_attend(qb, qib, wib, pos, k, v, ki)

    starts = jnp.arange(nb) * Q_BLOCK
    o = lax.map(blk, (to_blocks(q), to_blocks(qi), to_blocks(wi), starts))
    return jnp.moveaxis(o, 0, 1).reshape(B, T, N_HEADS * HEAD_DIM)


def short_conv(u, prev, w):
    T = u.shape[1]
    up = jnp.concatenate([prev, u], axis=1)
    out = w[0] * up[:, 0:T]
    for j in range(1, CONV_WIDTH):
        out = out + w[j] * up[:, j:j + T]
    return out, up[:, -(CONV_WIDTH - 1):]


def merge_out(o_attn, conv_out, bg, ga, gb, w_out):
    merged = jax.nn.sigmoid(ga) * o_attn + jax.nn.sigmoid(gb) * (bg * conv_out)
    return merged @ w_out


def ffn_and_ple(x, p, norm_ffn, w_gate_up, w_down, norm_ple, w_ple, w_ple_gate):
    h = rmsnorm(x, norm_ffn)
    gu = h @ w_gate_up
    g, u = gu[..., :D_FF], gu[..., D_FF:]
    x = x + (jax.nn.silu(g) * u) @ w_down
    gate = jax.nn.sigmoid(rmsnorm(x, norm_ple) @ w_ple_gate)
    return x + (p @ w_ple) * gate


def setup_inputs(seed: int = 0) -> dict:
    key = jax.random.key(seed)
    ks = jax.random.split(key, 24)
    f32 = jnp.float32
    n_pages = PAST_LEN // PAGE_SIZE
    n_used = DEC_BATCH * n_pages
    n_phys = n_used + -(-n_used // 4)
    nrm = lambda k, shape, s=1.0: jax.random.normal(k, shape, f32) * s
    page_table = jax.random.permutation(ks[0], n_phys)[:n_used].reshape(DEC_BATCH, n_pages).astype(jnp.int32)
    return {
        "x_prompt": nrm(ks[1], (BATCH, SEQ, D_MODEL)),
        "x_sample": nrm(ks[2], (DEC_BATCH, DEC_SEQ, D_MODEL)),
        "cache_k": nrm(ks[3], (DEPTH, n_phys, PAGE_SIZE, N_KV_HEADS, HEAD_DIM)),
        "cache_v": nrm(ks[4], (DEPTH, n_phys, PAGE_SIZE, N_KV_HEADS, HEAD_DIM)),
        "cache_kidx": nrm(ks[5], (DEPTH, n_phys, PAGE_SIZE, IDX_DIM)),
        "state_conv": nrm(ks[6], (DEPTH, DEC_BATCH, CONV_WIDTH - 1, D_CONV)),
        "page_table": page_table,
        "p_prompt": nrm(ks[7], (DEPTH, BATCH, SEQ, D_PLE)),
        "p_sample": nrm(ks[8], (DEPTH, DEC_BATCH, DEC_SEQ, D_PLE)),
        "norm_mix": 1.0 + nrm(ks[9], (DEPTH, D_MODEL), 0.01),
        "w_in": nrm(ks[10], (DEPTH, D_MODEL, D_IN_TOTAL), D_MODEL ** -0.5),
        "conv_w": nrm(ks[11], (DEPTH, CONV_WIDTH, D_CONV), CONV_WIDTH ** -0.5),
        "w_out": nrm(ks[12], (DEPTH, D_MODEL, D_MODEL), D_MODEL ** -0.5),
        "norm_ffn": 1.0 + nrm(ks[13], (DEPTH, D_MODEL), 0.01),
        "w_gate_up": nrm(ks[14], (DEPTH, D_MODEL, 2 * D_FF), D_MODEL ** -0.5),
        "w_down": nrm(ks[15], (DEPTH, D_FF, D_MODEL), D_FF ** -0.5),
        "norm_ple": 1.0 + nrm(ks[16], (DEPTH, D_MODEL), 0.01),
        "w_ple": nrm(ks[17], (DEPTH, D_PLE, D_MODEL), D_PLE ** -0.5),
        "w_ple_gate": nrm(ks[18], (DEPTH, D_MODEL, D_MODEL), D_MODEL ** -0.5),
        "norm_final": 1.0 + nrm(ks[19], (D_MODEL,), 0.01),
    }


def reference(x_prompt, x_sample, cache_k, cache_v, cache_kidx, state_conv, page_table,
              p_prompt, p_sample, norm_mix, w_in, conv_w, w_out, norm_ffn, w_gate_up, w_down,
              norm_ple, w_ple, w_ple_gate, norm_final):
    Bp, Tp = x_prompt.shape[:2]
    Bs, Ts = x_sample.shape[:2]
    past_len = page_table.shape[1] * PAGE_SIZE
    pos_p = jnp.arange(Tp)
    pos_s = past_len + jnp.arange(Ts)
    xp, xs = x_prompt, x_sample
    kp_l, vp_l, kip_l, cp_l = [], [], [], []
    ks_l, vs_l, kis_l, cs_l = [], [], [], []
    for l in range(DEPTH):
        h = rmsnorm(xp, norm_mix[l])
        q, k, v, qi, ki, wi, bg, cg, xc, ga, gb = mix_inputs(h, w_in[l], pos_p)
        o_a = prompt_attention(q, qi, wi, k, v, ki)
        conv_prev = jnp.zeros((Bp, CONV_WIDTH - 1, D_CONV), xp.dtype)
        conv_o, conv_new = short_conv(cg * xc, conv_prev, conv_w[l])
        xp = xp + merge_out(o_a, conv_o, bg, ga, gb, w_out[l])
        xp = ffn_and_ple(xp, p_prompt[l], norm_ffn[l], w_gate_up[l], w_down[l],
                         norm_ple[l], w_ple[l], w_ple_gate[l])
        kp_l.append(k); vp_l.append(v); kip_l.append(ki); cp_l.append(conv_new)

        h = rmsnorm(xs, norm_mix[l])
        q, k, v, qi, ki, wi, bg, cg, xc, ga, gb = mix_inputs(h, w_in[l], pos_s)
        past_k = cache_k[l][page_table].reshape(Bs, past_len, N_KV_HEADS, HEAD_DIM)
        past_v = cache_v[l][page_table].reshape(Bs, past_len, N_KV_HEADS, HEAD_DIM)
        past_ki = cache_kidx[l][page_table].reshape(Bs, past_len, IDX_DIM)
        k_all = jnp.concatenate([past_k, k], axis=1)
        v_all = jnp.concatenate([past_v, v], axis=1)
        ki_all = jnp.concatenate([past_ki, ki], axis=1)
        o_a = select_attend(q, qi, wi, pos_s, k_all, v_all, ki_all)
        conv_o, conv_new = short_conv(cg * xc, state_conv[l], conv_w[l])
        xs = xs + merge_out(o_a, conv_o, bg, ga, gb, w_out[l])
        xs = ffn_and_ple(xs, p_sample[l], norm_ffn[l], w_gate_up[l], w_down[l],
                         norm_ple[l], w_ple[l], w_ple_gate[l])
        ks_l.append(k); vs_l.append(v); kis_l.append(ki); cs_l.append(conv_new)

    y_prompt = rmsnorm(xp, norm_final)
    y_sample = rmsnorm(xs, norm_final)
    new_k_prompt = jnp.stack(kp_l)
    new_v_prompt = jnp.stack(vp_l)
    new_kidx_prompt = jnp.stack(kip_l)
    new_conv_prompt = jnp.stack(cp_l)
    new_k_sample = jnp.stack(ks_l)
    new_v_sample = jnp.stack(vs_l)
    new_kidx_sample = jnp.stack(kis_l)
    new_conv_sample = jnp.stack(cs_l)
    return (y_prompt, y_sample, new_k_prompt, new_v_prompt, new_kidx_prompt, new_conv_prompt,
            new_k_sample, new_v_sample, new_kidx_sample, new_conv_sample)
```

```python
import functools

import jax
import jax.numpy as jnp
from jax import lax
from jax.experimental import pallas as pl
from jax.experimental.pallas import tpu as pltpu

D_MODEL = 1024
N_HEADS = 8
N_KV_HEADS = 2
GROUP = N_HEADS // N_KV_HEADS
HEAD_DIM = 128
N_IDX_HEADS = 8
IDX_DIM = 64
IDX_SCALE = (N_IDX_HEADS * IDX_DIM) ** -0.5
QK_SCALE = HEAD_DIM ** -0.5
TOPK_MAX = 256
D_CONV = D_MODEL
CONV_WIDTH = 3
D_FF = 2816
D_PLE = 256
PAGE_SIZE = 128
ROPE_THETA = 10000.0
EPS = 1e-6

LANES = 128
SUBLANES = 8
KEY_CHUNK = 128
Q_BLOCK = 128
VMEM_LIMIT = 56 * 1024 * 1024

W1_COLS = N_HEADS * HEAD_DIM + 2 * N_KV_HEADS * HEAD_DIM + N_IDX_HEADS * IDX_DIM + LANES
W2_COLS = 5 * D_MODEL

F32 = jnp.float32
BF16 = jnp.bfloat16
NEG_INF = float("-inf")
POS_INF = float("inf")


def _dot(a, b):
    return jnp.dot(a, b, preferred_element_type=F32)


def _dot_nt(a, b):
    return lax.dot_general(a, b, (((1,), (1,)), ((), ())), preferred_element_type=F32)


def _rmsnorm(x, g):
    var = jnp.mean(x * x, axis=-1, keepdims=True)
    return (x * lax.rsqrt(var + EPS)) * g


def _rope_table(pos):
    def tab(half, reps):
        freqs = ROPE_THETA ** (-jnp.arange(half, dtype=F32) / half)
        ang = pos.astype(F32)[:, None] * freqs[None, :]
        c, s = jnp.cos(ang), jnp.sin(ang)
        return jnp.tile(jnp.concatenate([c, c], 1), (1, reps)), jnp.tile(jnp.concatenate([-s, s], 1), (1, reps))
    c128, s128 = tab(HEAD_DIM // 2, 1)
    c64, s64 = tab(IDX_DIM // 2, LANES // IDX_DIM)
    return jnp.concatenate([c128, s128, c64, s64], axis=1)


def _rope128(x, cos, sin):
    return x * cos + pltpu.roll(x, HEAD_DIM // 2, axis=1) * sin


def _rope64(x, cos, sin, first_half):
    partner = jnp.where(first_half, pltpu.roll(x, LANES - IDX_DIM // 2, axis=1), pltpu.roll(x, IDX_DIM // 2, axis=1))
    return x * cos + partner * sin


def _inproj_kernel(*refs, mode, tm):
    if mode == "prompt":
        (x_ref, g_ref, w1_ref, w2_ref, cw_ref, tab_ref,
         q_ref, k_ref, v_ref, kbf_ref, vt_ref, qi_ref, ki_ref, kibf_ref, wit_ref, ga_ref, cm_ref, cnew_ref,
         carry_ref) = refs
    else:
        (x_ref, g_ref, w1_ref, w2_ref, cw_ref, tab_ref, prev_ref,
         q_ref, k_ref, v_ref, qi_ref, ki_ref, kibf_ref, wit_ref, ga_ref, cm_ref, u_ref) = refs

    x = x_ref[0]
    h = _rmsnorm(x, g_ref[...]).astype(BF16)
    z1 = _dot(h, w1_ref[...])
    tab = tab_ref[...]
    cos128, sin128 = tab[:, 0:LANES], tab[:, LANES:2 * LANES]
    cos64, sin64 = tab[:, 2 * LANES:3 * LANES], tab[:, 3 * LANES:4 * LANES]
    lane = lax.broadcasted_iota(jnp.int32, (tm, LANES), 1)
    first_half = (lane % IDX_DIM) < (IDX_DIM // 2)

    off = 0
    for hh in range(N_HEADS):
        sl = z1[:, off:off + HEAD_DIM]
        q_ref[0, :, hh * HEAD_DIM:(hh + 1) * HEAD_DIM] = (_rope128(sl, cos128, sin128) * QK_SCALE).astype(BF16)
        off += HEAD_DIM
    for hh in range(N_KV_HEADS):
        kr = _rope128(z1[:, off:off + HEAD_DIM], cos128, sin128)
        k_ref[0, :, hh * HEAD_DIM:(hh + 1) * HEAD_DIM] = kr
        if mode == "prompt":
            kbf_ref[0, :, hh * HEAD_DIM:(hh + 1) * HEAD_DIM] = kr.astype(BF16)
        off += HEAD_DIM
    v = z1[:, off:off + N_KV_HEADS * HEAD_DIM]
    v_ref[0] = v
    if mode == "prompt":
        for c in range(tm // KEY_CHUNK):
            vt_ref[0, c] = v[c * KEY_CHUNK:(c + 1) * KEY_CHUNK, :].T.astype(BF16)
    off += N_KV_HEADS * HEAD_DIM
    for hh in range(N_IDX_HEADS * IDX_DIM // LANES):
        sl = z1[:, off:off + LANES]
        qi_ref[0, :, hh * LANES:(hh + 1) * LANES] = _rope64(sl, cos64, sin64, first_half).astype(BF16)
        off += LANES
    kiw = z1[:, off:off + LANES]
    kir = _rope64(kiw, cos64, sin64, first_half)[:, 0:IDX_DIM]
    ki_ref[0] = kir
    kibf_ref[0] = kir.astype(BF16)
    wit_ref[0] = kiw.T[IDX_DIM:IDX_DIM + N_IDX_HEADS, :] * IDX_SCALE

    z2 = _dot(h, w2_ref[...])
    bg = z2[:, 0:D_MODEL]
    u = z2[:, D_MODEL:2 * D_MODEL] * z2[:, 2 * D_MODEL:3 * D_MODEL]
    ga = z2[:, 3 * D_MODEL:4 * D_MODEL]
    gb = z2[:, 4 * D_MODEL:5 * D_MODEL]

    row = lax.broadcasted_iota(jnp.int32, (tm, D_CONV), 0)
    r1 = pltpu.roll(u, 1, axis=0)
    r2 = pltpu.roll(u, 2, axis=0)
    if mode == "prompt":
        t = pl.program_id(1)

        @pl.when(t == 0)
        def _():
            carry_ref[...] = jnp.zeros_like(carry_ref)

        c0 = carry_ref[0:1, :]
        c1 = carry_ref[1:2, :]
        um1 = jnp.where(row == 0, c1, r1)
        um2 = jnp.where(row == 0, c0, jnp.where(row == 1, c1, r2))
        carry_ref[0:2, :] = u[tm - 2:tm, :]
        cnew_ref[0] = u[tm - 2:tm, :]
    else:
        prev = prev_ref[0]
        seq_row = row % SUBLANES
        um1 = jnp.where(seq_row == 0, pltpu.roll(prev, tm - 1, axis=0), r1)
        um2 = jnp.where(seq_row < 2, prev, r2)
        u_ref[0] = u
    cw = cw_ref[...]
    conv = cw[0:1, :] * um2 + cw[1:2, :] * um1 + cw[2:3, :] * u
    ga_ref[0] = jax.nn.sigmoid(ga)
    cm_ref[0] = jax.nn.sigmoid(gb) * (bg * conv)


def _const_spec(shape):
    nd = len(shape)
    return pl.BlockSpec(shape, lambda *_: (0,) * nd, pipeline_mode=pl.Buffered(1))


def _inproj(x, norm_g, w1, w2, conv_w, tab, prev, *, mode, tm):
    B, T, _ = x.shape
    nt = T // tm
    tok = lambda w: pl.BlockSpec((1, tm, w), lambda b, t: (b, t, 0))
    in_specs = [tok(D_MODEL), _const_spec((1, D_MODEL)), _const_spec((D_MODEL, W1_COLS)),
                _const_spec((D_MODEL, W2_COLS)), _const_spec((CONV_WIDTH, D_CONV)),
                pl.BlockSpec((tm, 4 * LANES), lambda b, t: (t, 0))]
    args = [x, norm_g.reshape(1, D_MODEL), w1, w2, conv_w, tab]
    kvw = N_KV_HEADS * HEAD_DIM
    qiw = N_IDX_HEADS * IDX_DIM
    wit_spec = pl.BlockSpec((1, N_IDX_HEADS, tm), lambda b, t: (b, 0, t))
    if mode == "prompt":
        out_shape = [
            jax.ShapeDtypeStruct((B, T, D_MODEL), BF16),
            jax.ShapeDtypeStruct((B, T, kvw), F32),
            jax.ShapeDtypeStruct((B, T, kvw), F32),
            jax.ShapeDtypeStruct((B, T, kvw), BF16),
            jax.ShapeDtypeStruct((B, T // KEY_CHUNK, kvw, KEY_CHUNK), BF16),
            jax.ShapeDtypeStruct((B, T, qiw), BF16),
            jax.ShapeDtypeStruct((B, T, IDX_DIM), F32),
            jax.ShapeDtypeStruct((B, T, IDX_DIM), BF16),
            jax.ShapeDtypeStruct((B, N_IDX_HEADS, T), F32),
            jax.ShapeDtypeStruct((B, T, D_MODEL), F32),
            jax.ShapeDtypeStruct((B, T, D_MODEL), F32),
            jax.ShapeDtypeStruct((B, CONV_WIDTH - 1, D_CONV), F32),
        ]
        out_specs = [tok(D_MODEL), tok(kvw), tok(kvw), tok(kvw),
                     pl.BlockSpec((1, tm // KEY_CHUNK, kvw, KEY_CHUNK), lambda b, t: (b, t, 0, 0)),
                     tok(qiw), tok(IDX_DIM), tok(IDX_DIM), wit_spec, tok(D_MODEL), tok(D_MODEL),
                     pl.BlockSpec((1, CONV_WIDTH - 1, D_CONV), lambda b, t: (b, 0, 0))]
        scratch = [pltpu.VMEM((SUBLANES, D_CONV), F32)]
    else:
        in_specs.append(tok(D_CONV))
        args.append(prev)
        out_shape = [
            jax.ShapeDtypeStruct((B, T, D_MODEL), BF16),
            jax.ShapeDtypeStruct((B, T, kvw), F32),
            jax.ShapeDtypeStruct((B, T, kvw), F32),
            jax.ShapeDtypeStruct((B, T, qiw), BF16),
            jax.ShapeDtypeStruct((B, T, IDX_DIM), F32),
            jax.ShapeDtypeStruct((B, T, IDX_DIM), BF16),
            jax.ShapeDtypeStruct((B, N_IDX_HEADS, T), F32),
            jax.ShapeDtypeStruct((B, T, D_MODEL), F32),
            jax.ShapeDtypeStruct((B, T, D_MODEL), F32),
            jax.ShapeDtypeStruct((B, T, D_CONV), F32),
        ]
        out_specs = [tok(D_MODEL), tok(kvw), tok(kvw), tok(qiw), tok(IDX_DIM), tok(IDX_DIM), wit_spec,
                     tok(D_MODEL), tok(D_MODEL), tok(D_CONV)]
        scratch = []
    return pl.pallas_call(
        functools.partial(_inproj_kernel, mode=mode, tm=tm),
        grid=(B, nt),
        in_specs=in_specs,
        out_specs=out_specs,
        out_shape=out_shape,
        scratch_shapes=scratch,
        compiler_params=pltpu.CompilerParams(dimension_semantics=("arbitrary", "arbitrary"),
                                             vmem_limit_bytes=VMEM_LIMIT),
        name="inproj_" + mode,
    )(*args)


def _bisect(count_ge, lo0, top, kk):
    n_top = count_ge(top)
    at_top = n_top >= kk
    lo = jnp.where(at_top, top, lo0)
    hi = jnp.where(at_top, POS_INF, top)
    n_hi = jnp.where(at_top, 0.0, n_top)
    done = jnp.where(at_top, 1.0, 0.0)

    def cond(c):
        return jnp.min(c[3]) < 0.5

    def body(c):
        lo, hi, n_hi, done = c
        mid = 0.5 * lo + 0.5 * hi
        stuck = (mid <= lo) | (mid >= hi)
        n_mid = count_ge(mid)
        upd = (done < 0.5) & jnp.logical_not(stuck)
        go_lo = n_mid >= kk
        lo = jnp.where(upd & go_lo, mid, lo)
        up_hi = upd & jnp.logical_not(go_lo)
        hi = jnp.where(up_hi, mid, hi)
        n_hi = jnp.where(up_hi, n_mid, n_hi)
        done = jnp.where(stuck | (upd & (n_mid == kk)), 1.0, done)
        return lo, hi, n_hi, done

    lo, hi, n_hi, _ = lax.while_loop(cond, body, (lo, hi, n_hi, done))
    return lo, hi, n_hi


def _prompt_attn_kernel(qi_ref, wit_ref, q_ref, ki_ref, k_ref, vt_ref, o_ref, s_scr, b_scr, p_scr, *, n_sel):
    j = pl.program_id(1)
    nk = j + 1
    qi = qi_ref[0]
    wit = wit_ref[0]
    lane_q = lax.broadcasted_iota(jnp.int32, (KEY_CHUNK, Q_BLOCK), 1)
    row_k = lax.broadcasted_iota(jnp.int32, (KEY_CHUNK, Q_BLOCK), 0)
    tri = (lax.broadcasted_iota(jnp.int32, (KEY_CHUNK, KEY_CHUNK), 1)
           <= lax.broadcasted_iota(jnp.int32, (KEY_CHUNK, KEY_CHUNK), 0)).astype(BF16)

    def chunk(c):
        return pl.ds(pl.multiple_of(c * KEY_CHUNK, KEY_CHUNK), KEY_CHUNK)

    def score_body(c, carry):
        mn, mx = carry
        kic = ki_ref[0, chunk(c), :]
        acc = jnp.zeros((KEY_CHUNK, Q_BLOCK), F32)
        for h in range(N_IDX_HEADS):
            d = _dot_nt(kic, qi[:, h * IDX_DIM:(h + 1) * IDX_DIM])
            acc = acc + wit[h:h + 1, :] * jnp.maximum(d, 0.0)
        valid = (row_k + c * KEY_CHUNK) <= (lane_q + j * Q_BLOCK)
        s_scr[chunk(c), :] = jnp.where(valid, acc, NEG_INF)
        return jnp.minimum(mn, jnp.where(valid, acc, POS_INF)), jnp.maximum(mx, jnp.where(valid, acc, NEG_INF))

    mn, mx = lax.fori_loop(0, nk, score_body, (jnp.full((KEY_CHUNK, Q_BLOCK), POS_INF, F32),
                                               jnp.full((KEY_CHUNK, Q_BLOCK), NEG_INF, F32)))
    lo0 = jnp.min(mn, axis=0, keepdims=True)
    top = jnp.max(mx, axis=0, keepdims=True)

    def count_ge(t):
        def body(c, acc):
            return acc + jnp.where(s_scr[chunk(c), :] >= t, 1.0, 0.0)
        return jnp.sum(lax.fori_loop(0, nk, body, jnp.zeros((KEY_CHUNK, Q_BLOCK), F32)), axis=0, keepdims=True)

    n_valid = (j * Q_BLOCK + 1 + lax.broadcasted_iota(jnp.int32, (1, Q_BLOCK), 1)).astype(F32)
    kk = jnp.minimum(n_valid, float(n_sel))
    lo, hi, n_hi = _bisect(count_ge, lo0, top, kk)
    need = kk - n_hi

    def mask_body(c, run):
        s = s_scr[chunk(c), :]
        above = s >= hi
        elig = (s >= lo) & jnp.logical_not(above)
        rank = run + _dot(tri, jnp.where(elig, 1.0, 0.0).astype(BF16))
        sel = above | (elig & (rank <= need))
        b_scr[chunk(c), :] = jnp.where(sel, 0.0, NEG_INF)
        return rank[KEY_CHUNK - 1:KEY_CHUNK, :]

    lax.fori_loop(0, nk, mask_body, jnp.zeros((1, Q_BLOCK), F32))

    for h in range(N_HEADS):
        kv = h // GROUP
        qh = q_ref[0, :, h * HEAD_DIM:(h + 1) * HEAD_DIM]

        def s_body(c, m, kv=kv, qh=qh):
            s = _dot_nt(k_ref[0, chunk(c), kv * HEAD_DIM:(kv + 1) * HEAD_DIM], qh) + b_scr[chunk(c), :]
            p_scr[chunk(c), :] = s
            return jnp.maximum(m, s)

        m = lax.fori_loop(0, nk, s_body, jnp.full((KEY_CHUNK, Q_BLOCK), NEG_INF, F32))
        m = jnp.max(m, axis=0, keepdims=True)

        def pv_body(c, carry, kv=kv, m=m):
            l, acc = carry
            p = jnp.exp(p_scr[chunk(c), :] - m)
            acc = acc + _dot(vt_ref[0, c, kv * HEAD_DIM:(kv + 1) * HEAD_DIM, :], p.astype(BF16))
            return l + p, acc

        l, acc = lax.fori_loop(0, nk, pv_body, (jnp.zeros((KEY_CHUNK, Q_BLOCK), F32),
                                                jnp.zeros((HEAD_DIM, Q_BLOCK), F32)))
        l = jnp.sum(l, axis=0, keepdims=True)
        o_ref[0, :, h * HEAD_DIM:(h + 1) * HEAD_DIM] = (acc / l).T


def _prompt_attention(qi, wit, q, ki_bf, k_bf, vt):
    B, T, _ = q.shape
    nq = T // Q_BLOCK
    n_sel = min(TOPK_MAX, T // 4)
    kvw = N_KV_HEADS * HEAD_DIM
    return pl.pallas_call(
        functools.partial(_prompt_attn_kernel, n_sel=n_sel),
        grid=(B, nq),
        in_specs=[pl.BlockSpec((1, Q_BLOCK, N_IDX_HEADS * IDX_DIM), lambda b, j: (b, j, 0)),
                  pl.BlockSpec((1, N_IDX_HEADS, Q_BLOCK), lambda b, j: (b, 0, j)),
                  pl.BlockSpec((1, Q_BLOCK, D_MODEL), lambda b, j: (b, j, 0)),
                  pl.BlockSpec((1, T, IDX_DIM), lambda b, j: (b, 0, 0)),
                  pl.BlockSpec((1, T, kvw), lambda b, j: (b, 0, 0)),
                  pl.BlockSpec((1, T // KEY_CHUNK, kvw, KEY_CHUNK), lambda b, j: (b, 0, 0, 0))],
        out_specs=pl.BlockSpec((1, Q_BLOCK, D_MODEL), lambda b, j: (b, j, 0)),
        out_shape=jax.ShapeDtypeStruct((B, T, D_MODEL), F32),
        scratch_shapes=[pltpu.VMEM((T, Q_BLOCK), F32), pltpu.VMEM((T, Q_BLOCK), F32), pltpu.VMEM((T, Q_BLOCK), F32)],
        compiler_params=pltpu.CompilerParams(dimension_semantics=("arbitrary", "arbitrary"),
                                             vmem_limit_bytes=VMEM_LIMIT),
        name="prompt_attention",
    )(qi, wit, q, ki_bf, k_bf, vt)


def _sample_score_kernel(pt_ref, qi_ref, wcol_ref, kin_ref, *rest, n_pages, ts):
    page_refs, s_ref = rest[:n_pages], rest[n_pages]
    qi = qi_ref[0].astype(F32)
    q_all = jnp.concatenate([qi[:, h * IDX_DIM:(h + 1) * IDX_DIM] for h in range(N_IDX_HEADS)], axis=0).astype(BF16)
    wcol = jnp.broadcast_to(wcol_ref[0], (N_IDX_HEADS * ts, LANES))

    def score(keys_bf):
        r = jnp.maximum(_dot_nt(q_all, keys_bf), 0.0) * wcol
        acc = r[0:ts]
        for h in range(1, N_IDX_HEADS):
            acc = acc + r[h * ts:(h + 1) * ts]
        return acc

    for p in range(n_pages):
        s_ref[0, :, p * PAGE_SIZE:(p + 1) * PAGE_SIZE] = score(page_refs[p][0].astype(BF16))
    new_keys = jnp.concatenate([kin_ref[0], jnp.zeros((PAGE_SIZE - ts, IDX_DIM), F32)], axis=0).astype(BF16)
    lane = lax.broadcasted_iota(jnp.int32, (ts, PAGE_SIZE), 1)
    row = lax.broadcasted_iota(jnp.int32, (ts, PAGE_SIZE), 0)
    s_ref[0, :, n_pages * PAGE_SIZE:(n_pages + 1) * PAGE_SIZE] = jnp.where(lane <= row, score(new_keys), NEG_INF)


def _sample_scores(page_table, qi, wcol, ki_bf, cache_kidx):
    nb, ts, _ = qi.shape
    n_pages = page_table.shape[1]

    def page_spec(p):
        return pl.BlockSpec((1, PAGE_SIZE, IDX_DIM), lambda b, pt: (pt[b, p], 0, 0))

    grid_spec = pltpu.PrefetchScalarGridSpec(
        num_scalar_prefetch=1,
        grid=(nb,),
        in_specs=[pl.BlockSpec((1, ts, N_IDX_HEADS * IDX_DIM), lambda b, pt: (b, 0, 0)),
                  pl.BlockSpec((1, N_IDX_HEADS * ts, 1), lambda b, pt: (b, 0, 0)),
                  pl.BlockSpec((1, ts, IDX_DIM), lambda b, pt: (b, 0, 0))] + [page_spec(p) for p in range(n_pages)],
        out_specs=pl.BlockSpec((1, ts, (n_pages + 1) * PAGE_SIZE), lambda b, pt: (b, 0, 0)),
    )
    return pl.pallas_call(
        functools.partial(_sample_score_kernel, n_pages=n_pages, ts=ts),
        grid_spec=grid_spec,
        out_shape=jax.ShapeDtypeStruct((nb, ts, (n_pages + 1) * PAGE_SIZE), F32),
        compiler_params=pltpu.CompilerParams(dimension_semantics=("arbitrary",), vmem_limit_bytes=VMEM_LIMIT),
        name="sample_scores",
    )(page_table, qi, wcol, ki_bf, *([cache_kidx] * n_pages))


def _sample_select_kernel(s_ref, b_ref, *, n_sel, n_chunks):
    rows = s_ref.shape[0]
    tri = (lax.broadcasted_iota(jnp.int32, (KEY_CHUNK, KEY_CHUNK), 0)
           <= lax.broadcasted_iota(jnp.int32, (KEY_CHUNK, KEY_CHUNK), 1)).astype(BF16)

    def chunk(c):
        return s_ref[:, c * KEY_CHUNK:(c + 1) * KEY_CHUNK]

    mx = chunk(0)
    mn = jnp.where(chunk(0) == NEG_INF, POS_INF, chunk(0))
    for c in range(1, n_chunks):
        mx = jnp.maximum(mx, chunk(c))
        mn = jnp.minimum(mn, jnp.where(chunk(c) == NEG_INF, POS_INF, chunk(c)))
    top = jnp.max(mx, axis=1, keepdims=True)
    lo0 = jnp.min(mn, axis=1, keepdims=True)

    def count_ge(t):
        acc = jnp.where(chunk(0) >= t, 1.0, 0.0)
        for c in range(1, n_chunks):
            acc = acc + jnp.where(chunk(c) >= t, 1.0, 0.0)
        return jnp.sum(acc, axis=1, keepdims=True)

    kk = jnp.full((rows, 1), float(n_sel), F32)
    lo, hi, n_hi = _bisect(count_ge, lo0, top, kk)
    need = kk - n_hi
    run = jnp.zeros((rows, 1), F32)
    for c in range(n_chunks):
        s = chunk(c)
        above = s >= hi
        elig = (s >= lo) & jnp.logical_not(above)
        rank = run + _dot(jnp.where(elig, 1.0, 0.0).astype(BF16), tri)
        sel = above | (elig & (rank <= need))
        b_ref[:, c * KEY_CHUNK:(c + 1) * KEY_CHUNK] = jnp.where(sel, 0.0, NEG_INF)
        run = rank[:, KEY_CHUNK - 1:KEY_CHUNK]


def _sample_select(scores, n_sel, rows_per_step):
    rows, width = scores.shape
    return pl.pallas_call(
        functools.partial(_sample_select_kernel, n_sel=n_sel, n_chunks=width // KEY_CHUNK),
        grid=(rows // rows_per_step,),
        in_specs=[pl.BlockSpec((rows_per_step, width), lambda i: (i, 0))],
        out_specs=pl.BlockSpec((rows_per_step, width), lambda i: (i, 0)),
        out_shape=jax.ShapeDtypeStruct((rows, width), F32),
        compiler_params=pltpu.CompilerParams(dimension_semantics=("arbitrary",), vmem_limit_bytes=VMEM_LIMIT),
        name="sample_select",
    )(scores)


def _sample_attn_kernel(pt_ref, q_ref, b_ref, kn_ref, vn_ref, *rest, n_pages, ts):
    k_refs, v_refs, o_ref = rest[:n_pages], rest[n_pages:2 * n_pages], rest[2 * n_pages]
    q = q_ref[0].astype(F32)
    bias = b_ref[0]
    pad = jnp.zeros((PAGE_SIZE - ts, HEAD_DIM), F32)
    for kv in range(N_KV_HEADS):
        qg = jnp.concatenate([q[:, (kv * GROUP + g) * HEAD_DIM:(kv * GROUP + g + 1) * HEAD_DIM]
                              for g in range(GROUP)], axis=0).astype(BF16)
        s_tiles, v_tiles = [], []
        for p in range(n_pages + 1):
            if p < n_pages:
                kp = k_refs[p][0, pl.ds(kv, PAGE_SIZE, stride=N_KV_HEADS), :].astype(BF16)
                vp = v_refs[p][0, pl.ds(kv, PAGE_SIZE, stride=N_KV_HEADS), :].astype(BF16)
            else:
                kp = jnp.concatenate([kn_ref[0, :, kv * HEAD_DIM:(kv + 1) * HEAD_DIM], pad], axis=0).astype(BF16)
                vp = jnp.concatenate([vn_ref[0, :, kv * HEAD_DIM:(kv + 1) * HEAD_DIM], pad], axis=0).astype(BF16)
            bp = bias[:, p * PAGE_SIZE:(p + 1) * PAGE_SIZE]
            s_tiles.append(_dot_nt(qg, kp) + jnp.concatenate([bp] * GROUP, axis=0))
            v_tiles.append(vp)
        m = s_tiles[0]
        for s in s_tiles[1:]:
            m = jnp.maximum(m, s)
        m = jnp.max(m, axis=1, keepdims=True)
        l = jnp.zeros((GROUP * ts, PAGE_SIZE), F32)
        acc = jnp.zeros((GROUP * ts, HEAD_DIM), F32)
        for s, vp in zip(s_tiles, v_tiles):
            p_ = jnp.exp(s - m)
            l = l + p_
            acc = acc + _dot(p_.astype(BF16), vp)
        o = acc / jnp.sum(l, axis=1, keepdims=True)
        for g in range(GROUP):
            hh = kv * GROUP + g
            o_ref[0, :, hh * HEAD_DIM:(hh + 1) * HEAD_DIM] = o[g * ts:(g + 1) * ts, :]


def _sample_attention(page_table, q, bias, k_new, v_new, cache_k, cache_v):
    nb, ts, _ = q.shape
    n_pages = page_table.shape[1]
    kvw = N_KV_HEADS * HEAD_DIM

    def page_spec(p):
        return pl.BlockSpec((1, PAGE_SIZE * N_KV_HEADS, HEAD_DIM), lambda b, pt: (pt[b, p], 0, 0))

    grid_spec = pltpu.PrefetchScalarGridSpec(
        num_scalar_prefetch=1,
        grid=(nb,),
        in_specs=[pl.BlockSpec((1, ts, D_MODEL), lambda b, pt: (b, 0, 0)),
                  pl.BlockSpec((1, ts, (n_pages + 1) * PAGE_SIZE), lambda b, pt: (b, 0, 0)),
                  pl.BlockSpec((1, ts, kvw), lambda b, pt: (b, 0, 0)),
                  pl.BlockSpec((1, ts, kvw), lambda b, pt: (b, 0, 0))]
                 + [page_spec(p) for p in range(n_pages)] * 2,
        out_specs=pl.BlockSpec((1, ts, D_MODEL), lambda b, pt: (b, 0, 0)),
    )
    return pl.pallas_call(
        functools.partial(_sample_attn_kernel, n_pages=n_pages, ts=ts),
        grid_spec=grid_spec,
        out_shape=jax.ShapeDtypeStruct((nb, ts, D_MODEL), F32),
        compiler_params=pltpu.CompilerParams(dimension_semantics=("arbitrary",), vmem_limit_bytes=VMEM_LIMIT),
        name="sample_attention",
    )(page_table, q, bias, k_new, v_new, *([cache_k] * n_pages), *([cache_v] * n_pages))


def _ffn_kernel(x_ref, oa_ref, ga_ref, cm_ref, p_ref, wout_ref, gffn_ref, wg_ref, wu_ref, wd_ref,
                gple_ref, wple_ref, wpg_ref, gfin_ref, y_ref, x1_scr, h_scr, acc_scr):
    f = pl.program_id(1)

    @pl.when(f == 0)
    def _():
        merged = ga_ref[...] * oa_ref[...] + cm_ref[...]
        x1 = x_ref[...] + _dot(merged.astype(BF16), wout_ref[...])
        x1_scr[...] = x1
        h_scr[...] = _rmsnorm(x1, gffn_ref[...]).astype(BF16)
        acc_scr[...] = jnp.zeros_like(acc_scr)

    h = h_scr[...]
    g = _dot(h, wg_ref[...])
    u = _dot(h, wu_ref[...])
    acc_scr[...] += _dot((g * jax.nn.sigmoid(g) * u).astype(BF16), wd_ref[...])

    @pl.when(f == pl.num_programs(1) - 1)
    def _():
        x2 = x1_scr[...] + acc_scr[...]
        gate = jax.nn.sigmoid(_dot(_rmsnorm(x2, gple_ref[...]).astype(BF16), wpg_ref[...]))
        x3 = x2 + _dot(p_ref[...].astype(BF16), wple_ref[...]) * gate
        y_ref[...] = _rmsnorm(x3, gfin_ref[...])


def _ffn(x, oa, ga, cm, p, wout, gffn, wg, wu, wd, gple, wple, wpg, gfin, *, tm, tf):
    n = x.shape[0]
    tm = min(tm, n)
    tok = lambda w: pl.BlockSpec((tm, w), lambda i, f: (i, 0))
    vec = lambda a: a.reshape(1, D_MODEL)
    return pl.pallas_call(
        _ffn_kernel,
        grid=(n // tm, D_FF // tf),
        in_specs=[tok(D_MODEL), tok(D_MODEL), tok(D_MODEL), tok(D_MODEL), tok(D_PLE),
                  _const_spec((D_MODEL, D_MODEL)), _const_spec((1, D_MODEL)),
                  pl.BlockSpec((D_MODEL, tf), lambda i, f: (0, f)),
                  pl.BlockSpec((D_MODEL, tf), lambda i, f: (0, f)),
                  pl.BlockSpec((tf, D_MODEL), lambda i, f: (f, 0)),
                  _const_spec((1, D_MODEL)), _const_spec((D_PLE, D_MODEL)), _const_spec((D_MODEL, D_MODEL)),
                  _const_spec((1, D_MODEL))],
        out_specs=tok(D_MODEL),
        out_shape=jax.ShapeDtypeStruct((n, D_MODEL), F32),
        scratch_shapes=[pltpu.VMEM((tm, D_MODEL), F32), pltpu.VMEM((tm, D_MODEL), BF16),
                        pltpu.VMEM((tm, D_MODEL), F32)],
        compiler_params=pltpu.CompilerParams(dimension_semantics=("arbitrary", "arbitrary"),
                                             vmem_limit_bytes=VMEM_LIMIT),
        name="ffn",
    )(x, oa, ga, cm, p, wout, vec(gffn), wg, wu, wd, vec(gple), wple, wpg, vec(gfin))


def _split_w_in(w_in):
    n_att = N_HEADS * HEAD_DIM + 2 * N_KV_HEADS * HEAD_DIM + N_IDX_HEADS * IDX_DIM + IDX_DIM + N_IDX_HEADS
    pad = jnp.zeros((D_MODEL, W1_COLS - n_att), w_in.dtype)
    w1 = jnp.concatenate([w_in[:, :n_att], pad], axis=1).astype(BF16)
    w2 = w_in[:, n_att:].astype(BF16)
    return w1, w2


def kernel(x_prompt, x_sample, cache_k, cache_v, cache_kidx, state_conv, page_table, p_prompt, p_sample, norm_mix, w_in, conv_w, w_out, norm_ffn, w_gate_up, w_down, norm_ple, w_ple, w_ple_gate, norm_final):
    Bp, Tp, _ = x_prompt.shape
    Bs, Ts, _ = x_sample.shape
    n_pages = page_table.shape[1]
    past_len = n_pages * PAGE_SIZE
    n_phys = cache_k.shape[1]
    kvw = N_KV_HEADS * HEAD_DIM
    l = 0

    w1, w2 = _split_w_in(w_in[l])
    wout = w_out[l].astype(BF16)
    wg = w_gate_up[l][:, :D_FF].astype(BF16)
    wu = w_gate_up[l][:, D_FF:].astype(BF16)
    wd = w_down[l].astype(BF16)
    wple = w_ple[l].astype(BF16)
    wpg = w_ple_gate[l].astype(BF16)

    tab_p = _rope_table(jnp.arange(Tp))
    (q_p, k_p, v_p, kbf_p, vt_p, qi_p, ki_p, kibf_p, wit_p, ga_p, cm_p, cnew_p) = _inproj(
        x_prompt, norm_mix[l], w1, w2, conv_w[l], tab_p, None, mode="prompt", tm=256)
    oa_p = _prompt_attention(qi_p, wit_p, q_p, kibf_p, kbf_p, vt_p)
    n_p = Bp * Tp
    y_p = _ffn(x_prompt.reshape(n_p, D_MODEL), oa_p.reshape(n_p, D_MODEL), ga_p.reshape(n_p, D_MODEL),
               cm_p.reshape(n_p, D_MODEL), p_prompt[l].reshape(n_p, D_PLE), wout, norm_ffn[l], wg, wu, wd,
               norm_ple[l], wple, wpg, norm_final, tm=512, tf=D_FF // 2)

    n_s = Bs * Ts
    tm_s = 256
    tab_s = _rope_table(past_len + (jnp.arange(tm_s) % Ts))
    prev = jnp.concatenate([state_conv[l], jnp.zeros((Bs, Ts - (CONV_WIDTH - 1), D_CONV), F32)], axis=1)
    (q_s, k_s, v_s, qi_s, ki_s, kibf_s, wit_s, ga_s, cm_s, u_s) = _inproj(
        x_sample.reshape(n_s // tm_s, tm_s, D_MODEL), norm_mix[l], w1, w2, conv_w[l], tab_s,
        prev.reshape(n_s // tm_s, tm_s, D_CONV), mode="sample", tm=tm_s)
    wcol = wit_s.transpose(1, 0, 2).reshape(N_IDX_HEADS, Bs, Ts).transpose(1, 0, 2).reshape(Bs, N_IDX_HEADS * Ts, 1)
    scores = _sample_scores(page_table, qi_s.reshape(Bs, Ts, -1), wcol, ki_s.reshape(Bs, Ts, IDX_DIM),
                            cache_kidx[l])
    n_sel = min(TOPK_MAX, (past_len + Ts) // 4)
    bias = _sample_select(scores.reshape(n_s, -1), n_sel, 256)
    oa_s = _sample_attention(page_table, q_s.reshape(Bs, Ts, D_MODEL), bias.reshape(Bs, Ts, -1),
                             k_s.reshape(Bs, Ts, kvw), v_s.reshape(Bs, Ts, kvw),
                             cache_k[l].reshape(n_phys, PAGE_SIZE * N_KV_HEADS, HEAD_DIM),
                             cache_v[l].reshape(n_phys, PAGE_SIZE * N_KV_HEADS, HEAD_DIM))
    y_s = _ffn(x_sample.reshape(n_s, D_MODEL), oa_s.reshape(n_s, D_MODEL), ga_s.reshape(n_s, D_MODEL),
               cm_s.reshape(n_s, D_MODEL), p_sample[l].reshape(n_s, D_PLE), wout, norm_ffn[l], wg, wu, wd,
               norm_ple[l], wple, wpg, norm_final, tm=512, tf=D_FF // 2)

    return (y_p.reshape(Bp, Tp, D_MODEL), y_s.reshape(Bs, Ts, D_MODEL),
            k_p.reshape(1, Bp, Tp, N_KV_HEADS, HEAD_DIM), v_p.reshape(1, Bp, Tp, N_KV_HEADS, HEAD_DIM),
            ki_p.reshape(1, Bp, Tp, IDX_DIM), cnew_p.reshape(1, Bp, CONV_WIDTH - 1, D_CONV),
            k_s.reshape(1, Bs, Ts, N_KV_HEADS, HEAD_DIM), v_s.reshape(1, Bs, Ts, N_KV_HEADS, HEAD_DIM),
            ki_s.reshape(1, Bs, Ts, IDX_DIM),
            u_s.reshape(Bs, Ts, D_CONV)[:, Ts - (CONV_WIDTH - 1):, :].reshape(1, Bs, CONV_WIDTH - 1, D_CONV))
```

```python
import functools

import jax
import jax.numpy as jnp
from jax import lax
from jax.experimental import pallas as pl
from jax.experimental.pallas import tpu as pltpu

D_MODEL = 1024
N_HEADS = 8
N_KV_HEADS = 2
GROUP = N_HEADS // N_KV_HEADS
HEAD_DIM = 128
N_IDX_HEADS = 8
IDX_DIM = 64
IDX_SCALE = (N_IDX_HEADS * IDX_DIM) ** -0.5
QK_SCALE = HEAD_DIM ** -0.5
TOPK_MAX = 256
D_CONV = D_MODEL
CONV_WIDTH = 3
D_FF = 2816
D_PLE = 256
PAGE_SIZE = 128
ROPE_THETA = 10000.0
EPS = 1e-6

LANES = 128
SUBLANES = 8
KEY_STEP = 256
Q_BLOCK = 128
VMEM_LIMIT = 56 * 1024 * 1024

W1_COLS = N_HEADS * HEAD_DIM + 2 * N_KV_HEADS * HEAD_DIM + N_IDX_HEADS * IDX_DIM + LANES
W2_COLS = 5 * D_MODEL

F32 = jnp.float32
BF16 = jnp.bfloat16
NEG_INF = float("-inf")
INT_MIN = -2 ** 31
INT_MAX = 2 ** 31 - 1


def _dot(a, b):
    return jnp.dot(a, b, preferred_element_type=F32)


def _dot_nt(a, b):
    return lax.dot_general(a, b, (((1,), (1,)), ((), ())), preferred_element_type=F32)


def _rmsnorm(x, g):
    var = jnp.mean(x * x, axis=-1, keepdims=True)
    return (x * lax.rsqrt(var + EPS)) * g


def _rope_table(pos):
    def tab(half, reps):
        freqs = ROPE_THETA ** (-jnp.arange(half, dtype=F32) / half)
        ang = pos.astype(F32)[:, None] * freqs[None, :]
        c, s = jnp.cos(ang), jnp.sin(ang)
        return jnp.tile(jnp.concatenate([c, c], 1), (1, reps)), jnp.tile(jnp.concatenate([-s, s], 1), (1, reps))
    c128, s128 = tab(HEAD_DIM // 2, 1)
    c64, s64 = tab(IDX_DIM // 2, LANES // IDX_DIM)
    return jnp.concatenate([c128, s128, c64, s64], axis=1)


def _rope128(x, cos, sin):
    return x * cos + pltpu.roll(x, HEAD_DIM // 2, axis=1) * sin


def _rope64(x, cos, sin, first_half):
    partner = jnp.where(first_half, pltpu.roll(x, LANES - IDX_DIM // 2, axis=1), pltpu.roll(x, IDX_DIM // 2, axis=1))
    return x * cos + partner * sin


def _inproj_kernel(*refs, mode, tm):
    if mode == "prompt":
        (x_ref, g_ref, w1_ref, w2_ref, cw_ref, tab_ref,
         q_ref, k_ref, v_ref, kbf_ref, vt_ref, qi_ref, ki_ref, kibf_ref, wit_ref, ga_ref, cm_ref, cnew_ref,
         carry_ref) = refs
    else:
        (x_ref, g_ref, w1_ref, w2_ref, cw_ref, tab_ref, prev_ref,
         q_ref, k_ref, v_ref, qi_ref, ki_ref, kibf_ref, wit_ref, ga_ref, cm_ref, u_ref) = refs

    x = x_ref[0]
    h = _rmsnorm(x, g_ref[...]).astype(BF16)
    z1 = _dot(h, w1_ref[...])
    tab = tab_ref[...]
    cos128, sin128 = tab[:, 0:LANES], tab[:, LANES:2 * LANES]
    cos64, sin64 = tab[:, 2 * LANES:3 * LANES], tab[:, 3 * LANES:4 * LANES]
    lane = lax.broadcasted_iota(jnp.int32, (tm, LANES), 1)
    first_half = (lane % IDX_DIM) < (IDX_DIM // 2)

    off = 0
    for hh in range(N_HEADS):
        sl = z1[:, off:off + HEAD_DIM]
        q_ref[0, :, hh * HEAD_DIM:(hh + 1) * HEAD_DIM] = (_rope128(sl, cos128, sin128) * QK_SCALE).astype(BF16)
        off += HEAD_DIM
    for hh in range(N_KV_HEADS):
        kr = _rope128(z1[:, off:off + HEAD_DIM], cos128, sin128)
        k_ref[0, :, hh * HEAD_DIM:(hh + 1) * HEAD_DIM] = kr
        if mode == "prompt":
            kbf_ref[0, :, hh * HEAD_DIM:(hh + 1) * HEAD_DIM] = kr.astype(BF16)
        off += HEAD_DIM
    v = z1[:, off:off + N_KV_HEADS * HEAD_DIM]
    v_ref[0] = v
    if mode == "prompt":
        for c in range(tm // KEY_STEP):
            vt_ref[0, c] = v[c * KEY_STEP:(c + 1) * KEY_STEP, :].T.astype(BF16)
    off += N_KV_HEADS * HEAD_DIM
    for hh in range(N_IDX_HEADS * IDX_DIM // LANES):
        sl = z1[:, off:off + LANES]
        qi_ref[0, :, hh * LANES:(hh + 1) * LANES] = _rope64(sl, cos64, sin64, first_half).astype(BF16)
        off += LANES
    kiw = z1[:, off:off + LANES]
    kir = _rope64(kiw, cos64, sin64, first_half)[:, 0:IDX_DIM]
    ki_ref[0] = kir
    kibf_ref[0] = kir.astype(BF16)
    wit_ref[0] = kiw.T[IDX_DIM:IDX_DIM + N_IDX_HEADS, :] * IDX_SCALE

    z2 = _dot(h, w2_ref[...])
    bg = z2[:, 0:D_MODEL]
    u = z2[:, D_MODEL:2 * D_MODEL] * z2[:, 2 * D_MODEL:3 * D_MODEL]
    ga = z2[:, 3 * D_MODEL:4 * D_MODEL]
    gb = z2[:, 4 * D_MODEL:5 * D_MODEL]

    row = lax.broadcasted_iota(jnp.int32, (tm, D_CONV), 0)
    r1 = pltpu.roll(u, 1, axis=0)
    r2 = pltpu.roll(u, 2, axis=0)
    if mode == "prompt":
        t = pl.program_id(1)

        @pl.when(t == 0)
        def _():
            carry_ref[...] = jnp.zeros_like(carry_ref)

        c0 = carry_ref[0:1, :]
        c1 = carry_ref[1:2, :]
        um1 = jnp.where(row == 0, c1, r1)
        um2 = jnp.where(row == 0, c0, jnp.where(row == 1, c1, r2))
        carry_ref[0:2, :] = u[tm - 2:tm, :]
        cnew_ref[0] = u[tm - 2:tm, :]
    else:
        prev = prev_ref[0]
        seq_row = row % SUBLANES
        um1 = jnp.where(seq_row == 0, pltpu.roll(prev, tm - 1, axis=0), r1)
        um2 = jnp.where(seq_row < 2, prev, r2)
        u_ref[0] = u
    cw = cw_ref[...]
    conv = cw[0:1, :] * um2 + cw[1:2, :] * um1 + cw[2:3, :] * u
    ga_ref[0] = jax.nn.sigmoid(ga)
    cm_ref[0] = jax.nn.sigmoid(gb) * (bg * conv)


def _const_spec(shape):
    nd = len(shape)
    return pl.BlockSpec(shape, lambda *_: (0,) * nd, pipeline_mode=pl.Buffered(1))


def _inproj(x, norm_g, w1, w2, conv_w, tab, prev, *, mode, tm):
    B, T, _ = x.shape
    nt = T // tm
    tok = lambda w: pl.BlockSpec((1, tm, w), lambda b, t: (b, t, 0))
    in_specs = [tok(D_MODEL), _const_spec((1, D_MODEL)), _const_spec((D_MODEL, W1_COLS)),
                _const_spec((D_MODEL, W2_COLS)), _const_spec((CONV_WIDTH, D_CONV)),
                pl.BlockSpec((tm, 4 * LANES), lambda b, t: (t, 0))]
    args = [x, norm_g.reshape(1, D_MODEL), w1, w2, conv_w, tab]
    kvw = N_KV_HEADS * HEAD_DIM
    qiw = N_IDX_HEADS * IDX_DIM
    wit_spec = pl.BlockSpec((1, N_IDX_HEADS, tm), lambda b, t: (b, 0, t))
    if mode == "prompt":
        out_shape = [
            jax.ShapeDtypeStruct((B, T, D_MODEL), BF16),
            jax.ShapeDtypeStruct((B, T, kvw), F32),
            jax.ShapeDtypeStruct((B, T, kvw), F32),
            jax.ShapeDtypeStruct((B, T, kvw), BF16),
            jax.ShapeDtypeStruct((B, T // KEY_STEP, kvw, KEY_STEP), BF16),
            jax.ShapeDtypeStruct((B, T, qiw), BF16),
            jax.ShapeDtypeStruct((B, T, IDX_DIM), F32),
            jax.ShapeDtypeStruct((B, T, IDX_DIM), BF16),
            jax.ShapeDtypeStruct((B, N_IDX_HEADS, T), F32),
            jax.ShapeDtypeStruct((B, T, D_MODEL), F32),
            jax.ShapeDtypeStruct((B, T, D_MODEL), F32),
            jax.ShapeDtypeStruct((B, CONV_WIDTH - 1, D_CONV), F32),
        ]
        out_specs = [tok(D_MODEL), tok(kvw), tok(kvw), tok(kvw),
                     pl.BlockSpec((1, tm // KEY_STEP, kvw, KEY_STEP), lambda b, t: (b, t, 0, 0)),
                     tok(qiw), tok(IDX_DIM), tok(IDX_DIM), wit_spec, tok(D_MODEL), tok(D_MODEL),
                     pl.BlockSpec((1, CONV_WIDTH - 1, D_CONV), lambda b, t: (b, 0, 0))]
        scratch = [pltpu.VMEM((SUBLANES, D_CONV), F32)]
    else:
        in_specs.append(tok(D_CONV))
        args.append(prev)
        out_shape = [
            jax.ShapeDtypeStruct((B, T, D_MODEL), BF16),
            jax.ShapeDtypeStruct((B, T, kvw), F32),
            jax.ShapeDtypeStruct((B, T, kvw), F32),
            jax.ShapeDtypeStruct((B, T, qiw), BF16),
            jax.ShapeDtypeStruct((B, T, IDX_DIM), F32),
            jax.ShapeDtypeStruct((B, T, IDX_DIM), BF16),
            jax.ShapeDtypeStruct((B, N_IDX_HEADS, T), F32),
            jax.ShapeDtypeStruct((B, T, D_MODEL), F32),
            jax.ShapeDtypeStruct((B, T, D_MODEL), F32),
            jax.ShapeDtypeStruct((B, T, D_CONV), F32),
        ]
        out_specs = [tok(D_MODEL), tok(kvw), tok(kvw), tok(qiw), tok(IDX_DIM), tok(IDX_DIM), wit_spec,
                     tok(D_MODEL), tok(D_MODEL), tok(D_CONV)]
        scratch = []
    return pl.pallas_call(
        functools.partial(_inproj_kernel, mode=mode, tm=tm),
        grid=(B, nt),
        in_specs=in_specs,
        out_specs=out_specs,
        out_shape=out_shape,
        scratch_shapes=scratch,
        compiler_params=pltpu.CompilerParams(dimension_semantics=("arbitrary", "arbitrary"),
                                             vmem_limit_bytes=VMEM_LIMIT),
        name="inproj_" + mode,
    )(*args)


def _sort_key(x):
    bits = lax.bitcast_convert_type(x, jnp.int32)
    return bits ^ ((bits >> 31) & 0x7FFFFFFF)


def _search_keys(count_ge, lo0, hi0, kk):
    def cond(c):
        return c[4] == 0

    def body(c):
        lo, hi, n_hi, done, _ = c
        all_done = (jnp.min(done) > 0.5).astype(jnp.int32)
        mid = (lo & hi) + ((lo ^ hi) >> 1)
        stuck = mid == lo
        n_mid = count_ge(mid)
        upd = (done < 0.5) & jnp.logical_not(stuck)
        go_lo = n_mid >= kk
        lo = jnp.where(upd & go_lo, mid, lo)
        up_hi = upd & jnp.logical_not(go_lo)
        hi = jnp.where(up_hi, mid, hi)
        n_hi = jnp.where(up_hi, n_mid, n_hi)
        done = jnp.where(stuck | (upd & (n_mid == kk)), 1.0, done)
        return lo, hi, n_hi, done, all_done

    zero = jnp.zeros(kk.shape, F32)
    lo, hi, n_hi, _, _ = lax.while_loop(cond, body, (lo0, hi0, zero, zero, jnp.int32(0)))
    return lo, hi, n_hi


def _fold_rows(x, op):
    return op(x.reshape(x.shape[0] // SUBLANES, SUBLANES, x.shape[1]), axis=0)


def _prompt_attn_kernel(qi_ref, wit_ref, q_ref, ki_ref, k_ref, vt_ref, o_ref, key_scr, b_scr, acc_scr, *, n_sel):
    j = pl.program_id(1)
    n_steps = (j + KEY_STEP // Q_BLOCK) // (KEY_STEP // Q_BLOCK)
    qi = qi_ref[0]
    q_stack = jnp.concatenate([qi[:, h * IDX_DIM:(h + 1) * IDX_DIM] for h in range(N_IDX_HEADS)], axis=0)
    wit = wit_ref[0]
    q = q_ref[0]
    q_grp = [jnp.concatenate([q[:, (kv * GROUP + g) * HEAD_DIM:(kv * GROUP + g + 1) * HEAD_DIM]
                              for g in range(GROUP)], axis=0) for kv in range(N_KV_HEADS)]
    key_pos = lax.broadcasted_iota(jnp.int32, (KEY_STEP, Q_BLOCK), 0)
    q_pos = lax.broadcasted_iota(jnp.int32, (KEY_STEP, Q_BLOCK), 1) + j * Q_BLOCK
    tri = jnp.where(lax.broadcasted_iota(jnp.int32, (KEY_STEP, KEY_STEP), 1)
                    <= lax.broadcasted_iota(jnp.int32, (KEY_STEP, KEY_STEP), 0), 1.0, 0.0).astype(BF16)

    def step(i):
        return pl.ds(pl.multiple_of(i * KEY_STEP, KEY_STEP), KEY_STEP)

    def score_body(i, carry):
        mn, mx = carry
        d = _dot_nt(ki_ref[0, step(i), :], q_stack)
        acc = wit[0:1, :] * jnp.maximum(d[:, 0:Q_BLOCK], 0.0)
        for h in range(1, N_IDX_HEADS):
            acc = acc + wit[h:h + 1, :] * jnp.maximum(d[:, h * Q_BLOCK:(h + 1) * Q_BLOCK], 0.0)
        key = _sort_key(acc)
        masked = jnp.where(key_pos + i * KEY_STEP <= q_pos, key, INT_MIN)
        key_scr[step(i), :] = masked
        return jnp.minimum(mn, _fold_rows(key, jnp.min)), jnp.maximum(mx, _fold_rows(masked, jnp.max))

    mn, mx = lax.fori_loop(0, n_steps, score_body, (jnp.full((SUBLANES, Q_BLOCK), INT_MAX, jnp.int32),
                                                    jnp.full((SUBLANES, Q_BLOCK), INT_MIN, jnp.int32)))
    lo0 = jnp.min(mn, axis=0, keepdims=True)
    hi0 = jnp.max(mx, axis=0, keepdims=True) + 1

    def count_ge(t):
        def body(i, acc):
            return acc + _fold_rows(jnp.where(key_scr[step(i), :] >= t, 1.0, 0.0), jnp.sum)
        return jnp.sum(lax.fori_loop(0, n_steps, body, jnp.zeros((SUBLANES, Q_BLOCK), F32)), axis=0, keepdims=True)

    n_valid = (j * Q_BLOCK + 1 + lax.broadcasted_iota(jnp.int32, (1, Q_BLOCK), 1)).astype(F32)
    kk = jnp.minimum(n_valid, float(n_sel))
    lo, hi, n_hi = _search_keys(count_ge, lo0, hi0, kk)
    need = kk - n_hi

    def mask_body(i, run):
        key = key_scr[step(i), :]
        above = key >= hi
        elig = (key >= lo) & jnp.logical_not(above)
        rank = run + _dot(tri, jnp.where(elig, 1.0, 0.0).astype(BF16))
        sel = above | (elig & (rank <= need))
        b_scr[step(i), :] = jnp.where(sel, 0.0, NEG_INF)
        return rank[KEY_STEP - 1:KEY_STEP, :]

    lax.fori_loop(0, n_steps, mask_body, jnp.zeros((1, Q_BLOCK), F32))

    def logits(i, kv):
        return _dot_nt(k_ref[0, step(i), kv * HEAD_DIM:(kv + 1) * HEAD_DIM], q_grp[kv])

    def max_body(i, m):
        b = b_scr[step(i), :]
        out = []
        for kv in range(N_KV_HEADS):
            s = logits(i, kv)
            for g in range(GROUP):
                sg = s[:, g * Q_BLOCK:(g + 1) * Q_BLOCK] + b
                out.append(jnp.maximum(m[kv * GROUP + g], _fold_rows(sg, jnp.max)))
        return tuple(out)

    m = lax.fori_loop(0, n_steps, max_body, tuple(jnp.full((SUBLANES, Q_BLOCK), NEG_INF, F32) for _ in range(N_HEADS)))
    m = [jnp.max(x, axis=0, keepdims=True) for x in m]

    acc_scr[...] = jnp.zeros_like(acc_scr)

    def pv_body(i, l):
        b = b_scr[step(i), :]
        out = []
        for kv in range(N_KV_HEADS):
            s = logits(i, kv)
            ps = []
            for g in range(GROUP):
                h = kv * GROUP + g
                p = jnp.exp(s[:, g * Q_BLOCK:(g + 1) * Q_BLOCK] + b - m[h])
                out.append(l[h] + _fold_rows(p, jnp.sum))
                ps.append(p.astype(BF16))
            acc_scr[kv] += _dot(vt_ref[0, i, kv * HEAD_DIM:(kv + 1) * HEAD_DIM, :], jnp.concatenate(ps, axis=1))
        return tuple(out)

    l = lax.fori_loop(0, n_steps, pv_body, tuple(jnp.zeros((SUBLANES, Q_BLOCK), F32) for _ in range(N_HEADS)))
    for h in range(N_HEADS):
        kv, g = divmod(h, GROUP)
        inv_l = 1.0 / jnp.sum(l[h], axis=0, keepdims=True)
        o_ref[0, :, h * HEAD_DIM:(h + 1) * HEAD_DIM] = (acc_scr[kv, :, g * Q_BLOCK:(g + 1) * Q_BLOCK] * inv_l).T


def _prompt_attention(qi, wit, q, ki_bf, k_bf, vt):
    B, T, _ = q.shape
    nq = T // Q_BLOCK
    n_sel = min(TOPK_MAX, T // 4)
    kvw = N_KV_HEADS * HEAD_DIM
    return pl.pallas_call(
        functools.partial(_prompt_attn_kernel, n_sel=n_sel),
        grid=(B, nq),
        in_specs=[pl.BlockSpec((1, Q_BLOCK, N_IDX_HEADS * IDX_DIM), lambda b, j: (b, j, 0)),
                  pl.BlockSpec((1, N_IDX_HEADS, Q_BLOCK), lambda b, j: (b, 0, j)),
                  pl.BlockSpec((1, Q_BLOCK, D_MODEL), lambda b, j: (b, j, 0)),
                  pl.BlockSpec((1, T, IDX_DIM), lambda b, j: (b, 0, 0)),
                  pl.BlockSpec((1, T, kvw), lambda b, j: (b, 0, 0)),
                  pl.BlockSpec((1, T // KEY_STEP, kvw, KEY_STEP), lambda b, j: (b, 0, 0, 0))],
        out_specs=pl.BlockSpec((1, Q_BLOCK, D_MODEL), lambda b, j: (b, j, 0)),
        out_shape=jax.ShapeDtypeStruct((B, T, D_MODEL), F32),
        scratch_shapes=[pltpu.VMEM((T, Q_BLOCK), jnp.int32), pltpu.VMEM((T, Q_BLOCK), F32),
                        pltpu.VMEM((N_KV_HEADS, HEAD_DIM, GROUP * Q_BLOCK), F32)],
        compiler_params=pltpu.CompilerParams(dimension_semantics=("arbitrary", "arbitrary"),
                                             vmem_limit_bytes=VMEM_LIMIT),
        name="prompt_attention",
    )(qi, wit, q, ki_bf, k_bf, vt)


def _sample_score_kernel(pt_ref, qi_ref, wcol_ref, kin_ref, *rest, n_pages, ts):
    page_refs, s_ref = rest[:n_pages], rest[n_pages]
    qi = qi_ref[0].astype(F32)
    q_all = jnp.concatenate([qi[:, h * IDX_DIM:(h + 1) * IDX_DIM] for h in range(N_IDX_HEADS)], axis=0).astype(BF16)
    wcol = jnp.broadcast_to(wcol_ref[0], (N_IDX_HEADS * ts, LANES))

    def score(keys_bf):
        r = jnp.maximum(_dot_nt(q_all, keys_bf), 0.0) * wcol
        acc = r[0:ts]
        for h in range(1, N_IDX_HEADS):
            acc = acc + r[h * ts:(h + 1) * ts]
        return acc

    for p in range(n_pages):
        s_ref[0, :, p * PAGE_SIZE:(p + 1) * PAGE_SIZE] = score(page_refs[p][0].astype(BF16))
    new_keys = jnp.concatenate([kin_ref[0], jnp.zeros((PAGE_SIZE - ts, IDX_DIM), F32)], axis=0).astype(BF16)
    lane = lax.broadcasted_iota(jnp.int32, (ts, PAGE_SIZE), 1)
    row = lax.broadcasted_iota(jnp.int32, (ts, PAGE_SIZE), 0)
    s_ref[0, :, n_pages * PAGE_SIZE:(n_pages + 1) * PAGE_SIZE] = jnp.where(lane <= row, score(new_keys), NEG_INF)


def _sample_scores(page_table, qi, wcol, ki_bf, cache_kidx):
    nb, ts, _ = qi.shape
    n_pages = page_table.shape[1]

    def page_spec(p):
        return pl.BlockSpec((1, PAGE_SIZE, IDX_DIM), lambda b, pt: (pt[b, p], 0, 0))

    grid_spec = pltpu.PrefetchScalarGridSpec(
        num_scalar_prefetch=1,
        grid=(nb,),
        in_specs=[pl.BlockSpec((1, ts, N_IDX_HEADS * IDX_DIM), lambda b, pt: (b, 0, 0)),
                  pl.BlockSpec((1, N_IDX_HEADS * ts, 1), lambda b, pt: (b, 0, 0)),
                  pl.BlockSpec((1, ts, IDX_DIM), lambda b, pt: (b, 0, 0))] + [page_spec(p) for p in range(n_pages)],
        out_specs=pl.BlockSpec((1, ts, (n_pages + 1) * PAGE_SIZE), lambda b, pt: (b, 0, 0)),
    )
    return pl.pallas_call(
        functools.partial(_sample_score_kernel, n_pages=n_pages, ts=ts),
        grid_spec=grid_spec,
        out_shape=jax.ShapeDtypeStruct((nb, ts, (n_pages + 1) * PAGE_SIZE), F32),
        compiler_params=pltpu.CompilerParams(dimension_semantics=("arbitrary",), vmem_limit_bytes=VMEM_LIMIT),
        name="sample_scores",
    )(page_table, qi, wcol, ki_bf, *([cache_kidx] * n_pages))


def _sample_select_kernel(s_ref, b_ref, key_scr, *, n_sel, n_chunks):
    rows = s_ref.shape[0]
    tri = jnp.where(lax.broadcasted_iota(jnp.int32, (LANES, LANES), 0)
                    <= lax.broadcasted_iota(jnp.int32, (LANES, LANES), 1), 1.0, 0.0).astype(BF16)

    def cols(c):
        return slice(c * LANES, (c + 1) * LANES)

    mn = jnp.full((rows, LANES), INT_MAX, jnp.int32)
    mx = jnp.full((rows, LANES), INT_MIN, jnp.int32)
    for c in range(n_chunks):
        s = s_ref[:, cols(c)]
        valid = s > NEG_INF
        key = _sort_key(s)
        key_scr[:, cols(c)] = jnp.where(valid, key, INT_MIN)
        mn = jnp.minimum(mn, jnp.where(valid, key, INT_MAX))
        mx = jnp.maximum(mx, jnp.where(valid, key, INT_MIN))
    lo0 = jnp.min(mn, axis=1, keepdims=True)
    hi0 = jnp.max(mx, axis=1, keepdims=True) + 1

    def count_ge(t):
        acc = jnp.where(key_scr[:, cols(0)] >= t, 1.0, 0.0)
        for c in range(1, n_chunks):
            acc = acc + jnp.where(key_scr[:, cols(c)] >= t, 1.0, 0.0)
        return jnp.sum(acc, axis=1, keepdims=True)

    kk = jnp.full((rows, 1), float(n_sel), F32)
    lo, hi, n_hi = _search_keys(count_ge, lo0, hi0, kk)
    need = kk - n_hi
    run = jnp.zeros((rows, 1), F32)
    for c in range(n_chunks):
        key = key_scr[:, cols(c)]
        above = key >= hi
        elig = (key >= lo) & jnp.logical_not(above)
        rank = run + _dot(jnp.where(elig, 1.0, 0.0).astype(BF16), tri)
        sel = above | (elig & (rank <= need))
        b_ref[:, cols(c)] = jnp.where(sel, 0.0, NEG_INF)
        run = rank[:, LANES - 1:LANES]


def _sample_select(scores, n_sel, rows_per_step):
    rows, width = scores.shape
    return pl.pallas_call(
        functools.partial(_sample_select_kernel, n_sel=n_sel, n_chunks=width // LANES),
        grid=(rows // rows_per_step,),
        in_specs=[pl.BlockSpec((rows_per_step, width), lambda i: (i, 0))],
        out_specs=pl.BlockSpec((rows_per_step, width), lambda i: (i, 0)),
        out_shape=jax.ShapeDtypeStruct((rows, width), F32),
        scratch_shapes=[pltpu.VMEM((rows_per_step, width), jnp.int32)],
        compiler_params=pltpu.CompilerParams(dimension_semantics=("arbitrary",), vmem_limit_bytes=VMEM_LIMIT),
        name="sample_select",
    )(scores)


def _sample_attn_kernel(pt_ref, q_ref, b_ref, kn_ref, vn_ref, *rest, n_pages, ts):
    k_refs, v_refs, o_ref = rest[:n_pages], rest[n_pages:2 * n_pages], rest[2 * n_pages]
    q = q_ref[0].astype(F32)
    bias = b_ref[0]
    pad = jnp.zeros((PAGE_SIZE - ts, HEAD_DIM), F32)
    for kv in range(N_KV_HEADS):
        qg = jnp.concatenate([q[:, (kv * GROUP + g) * HEAD_DIM:(kv * GROUP + g + 1) * HEAD_DIM]
                              for g in range(GROUP)], axis=0).astype(BF16)
        s_tiles, v_tiles = [], []
        for p in range(n_pages + 1):
            if p < n_pages:
                kp = k_refs[p][0, pl.ds(kv, PAGE_SIZE, stride=N_KV_HEADS), :].astype(BF16)
                vp = v_refs[p][0, pl.ds(kv, PAGE_SIZE, stride=N_KV_HEADS), :].astype(BF16)
            else:
                kp = jnp.concatenate([kn_ref[0, :, kv * HEAD_DIM:(kv + 1) * HEAD_DIM], pad], axis=0).astype(BF16)
                vp = jnp.concatenate([vn_ref[0, :, kv * HEAD_DIM:(kv + 1) * HEAD_DIM], pad], axis=0).astype(BF16)
            bp = bias[:, p * PAGE_SIZE:(p + 1) * PAGE_SIZE]
            s_tiles.append(_dot_nt(qg, kp) + jnp.concatenate([bp] * GROUP, axis=0))
            v_tiles.append(vp)
        m = s_tiles[0]
        for s in s_tiles[1:]:
            m = jnp.maximum(m, s)
        m = jnp.max(m, axis=1, keepdims=True)
        l = jnp.zeros((GROUP * ts, PAGE_SIZE), F32)
        acc = jnp.zeros((GROUP * ts, HEAD_DIM), F32)
        for s, vp in zip(s_tiles, v_tiles):
            p_ = jnp.exp(s - m)
            l = l + p_
            acc = acc + _dot(p_.astype(BF16), vp)
        o = acc / jnp.sum(l, axis=1, keepdims=True)
        for g in range(GROUP):
            hh = kv * GROUP + g
            o_ref[0, :, hh * HEAD_DIM:(hh + 1) * HEAD_DIM] = o[g * ts:(g + 1) * ts, :]


def _sample_attention(page_table, q, bias, k_new, v_new, cache_k, cache_v):
    nb, ts, _ = q.shape
    n_pages = page_table.shape[1]
    kvw = N_KV_HEADS * HEAD_DIM

    def page_spec(p):
        return pl.BlockSpec((1, PAGE_SIZE * N_KV_HEADS, HEAD_DIM), lambda b, pt: (pt[b, p], 0, 0))

    grid_spec = pltpu.PrefetchScalarGridSpec(
        num_scalar_prefetch=1,
        grid=(nb,),
        in_specs=[pl.BlockSpec((1, ts, D_MODEL), lambda b, pt: (b, 0, 0)),
                  pl.BlockSpec((1, ts, (n_pages + 1) * PAGE_SIZE), lambda b, pt: (b, 0, 0)),
                  pl.BlockSpec((1, ts, kvw), lambda b, pt: (b, 0, 0)),
                  pl.BlockSpec((1, ts, kvw), lambda b, pt: (b, 0, 0))]
                 + [page_spec(p) for p in range(n_pages)] * 2,
        out_specs=pl.BlockSpec((1, ts, D_MODEL), lambda b, pt: (b, 0, 0)),
    )
    return pl.pallas_call(
        functools.partial(_sample_attn_kernel, n_pages=n_pages, ts=ts),
        grid_spec=grid_spec,
        out_shape=jax.ShapeDtypeStruct((nb, ts, D_MODEL), F32),
        compiler_params=pltpu.CompilerParams(dimension_semantics=("arbitrary",), vmem_limit_bytes=VMEM_LIMIT),
        name="sample_attention",
    )(page_table, q, bias, k_new, v_new, *([cache_k] * n_pages), *([cache_v] * n_pages))


def _ffn_kernel(x_ref, oa_ref, ga_ref, cm_ref, p_ref, wout_ref, gffn_ref, wg_ref, wu_ref, wd_ref,
                gple_ref, wple_ref, wpg_ref, gfin_ref, y_ref, x1_scr, h_scr, acc_scr):
    f = pl.program_id(1)

    @pl.when(f == 0)
    def _():
        merged = ga_ref[...] * oa_ref[...] + cm_ref[...]
        x1 = x_ref[...] + _dot(merged.astype(BF16), wout_ref[...])
        x1_scr[...] = x1
        h_scr[...] = _rmsnorm(x1, gffn_ref[...]).astype(BF16)
        acc_scr[...] = jnp.zeros_like(acc_scr)

    h = h_scr[...]
    g = _dot(h, wg_ref[...])
    u = _dot(h, wu_ref[...])
    acc_scr[...] += _dot((g * jax.nn.sigmoid(g) * u).astype(BF16), wd_ref[...])

    @pl.when(f == pl.num_programs(1) - 1)
    def _():
        x2 = x1_scr[...] + acc_scr[...]
        gate = jax.nn.sigmoid(_dot(_rmsnorm(x2, gple_ref[...]).astype(BF16), wpg_ref[...]))
        x3 = x2 + _dot(p_ref[...].astype(BF16), wple_ref[...]) * gate
        y_ref[...] = _rmsnorm(x3, gfin_ref[...])


def _ffn(x, oa, ga, cm, p, wout, gffn, wg, wu, wd, gple, wple, wpg, gfin, *, tm, tf):
    n = x.shape[0]
    tm = min(tm, n)
    tok = lambda w: pl.BlockSpec((tm, w), lambda i, f: (i, 0))
    vec = lambda a: a.reshape(1, D_MODEL)
    return pl.pallas_call(
        _ffn_kernel,
        grid=(n // tm, D_FF // tf),
        in_specs=[tok(D_MODEL), tok(D_MODEL), tok(D_MODEL), tok(D_MODEL), tok(D_PLE),
                  _const_spec((D_MODEL, D_MODEL)), _const_spec((1, D_MODEL)),
                  pl.BlockSpec((D_MODEL, tf), lambda i, f: (0, f)),
                  pl.BlockSpec((D_MODEL, tf), lambda i, f: (0, f)),
                  pl.BlockSpec((tf, D_MODEL), lambda i, f: (f, 0)),
                  _const_spec((1, D_MODEL)), _const_spec((D_PLE, D_MODEL)), _const_spec((D_MODEL, D_MODEL)),
                  _const_spec((1, D_MODEL))],
        out_specs=tok(D_MODEL),
        out_shape=jax.ShapeDtypeStruct((n, D_MODEL), F32),
        scratch_shapes=[pltpu.VMEM((tm, D_MODEL), F32), pltpu.VMEM((tm, D_MODEL), BF16),
                        pltpu.VMEM((tm, D_MODEL), F32)],
        compiler_params=pltpu.CompilerParams(dimension_semantics=("arbitrary", "arbitrary"),
                                             vmem_limit_bytes=VMEM_LIMIT),
        name="ffn",
    )(x, oa, ga, cm, p, wout, vec(gffn), wg, wu, wd, vec(gple), wple, wpg, vec(gfin))


def _split_w_in(w_in):
    n_att = N_HEADS * HEAD_DIM + 2 * N_KV_HEADS * HEAD_DIM + N_IDX_HEADS * IDX_DIM + IDX_DIM + N_IDX_HEADS
    pad = jnp.zeros((D_MODEL, W1_COLS - n_att), w_in.dtype)
    w1 = jnp.concatenate([w_in[:, :n_att], pad], axis=1).astype(BF16)
    w2 = w_in[:, n_att:].astype(BF16)
    return w1, w2


def kernel(x_prompt, x_sample, cache_k, cache_v, cache_kidx, state_conv, page_table, p_prompt, p_sample, norm_mix, w_in, conv_w, w_out, norm_ffn, w_gate_up, w_down, norm_ple, w_ple, w_ple_gate, norm_final):
    Bp, Tp, _ = x_prompt.shape
    Bs, Ts, _ = x_sample.shape
    n_pages = page_table.shape[1]
    past_len = n_pages * PAGE_SIZE
    n_phys = cache_k.shape[1]
    kvw = N_KV_HEADS * HEAD_DIM
    l = 0

    w1, w2 = _split_w_in(w_in[l])
    wout = w_out[l].astype(BF16)
    wg = w_gate_up[l][:, :D_FF].astype(BF16)
    wu = w_gate_up[l][:, D_FF:].astype(BF16)
    wd = w_down[l].astype(BF16)
    wple = w_ple[l].astype(BF16)
    wpg = w_ple_gate[l].astype(BF16)

    tab_p = _rope_table(jnp.arange(Tp))
    (q_p, k_p, v_p, kbf_p, vt_p, qi_p, ki_p, kibf_p, wit_p, ga_p, cm_p, cnew_p) = _inproj(
        x_prompt, norm_mix[l], w1, w2, conv_w[l], tab_p, None, mode="prompt", tm=256)
    oa_p = _prompt_attention(qi_p, wit_p, q_p, kibf_p, kbf_p, vt_p)
    n_p = Bp * Tp
    y_p = _ffn(x_prompt.reshape(n_p, D_MODEL), oa_p.reshape(n_p, D_MODEL), ga_p.reshape(n_p, D_MODEL),
               cm_p.reshape(n_p, D_MODEL), p_prompt[l].reshape(n_p, D_PLE), wout, norm_ffn[l], wg, wu, wd,
               norm_ple[l], wple, wpg, norm_final, tm=512, tf=D_FF // 2)

    n_s = Bs * Ts
    tm_s = 256
    tab_s = _rope_table(past_len + (jnp.arange(tm_s) % Ts))
    prev = jnp.concatenate([state_conv[l], jnp.zeros((Bs, Ts - (CONV_WIDTH - 1), D_CONV), F32)], axis=1)
    (q_s, k_s, v_s, qi_s, ki_s, kibf_s, wit_s, ga_s, cm_s, u_s) = _inproj(
        x_sample.reshape(n_s // tm_s, tm_s, D_MODEL), norm_mix[l], w1, w2, conv_w[l], tab_s,
        prev.reshape(n_s // tm_s, tm_s, D_CONV), mode="sample", tm=tm_s)
    wcol = wit_s.transpose(1, 0, 2).reshape(N_IDX_HEADS, Bs, Ts).transpose(1, 0, 2).reshape(Bs, N_IDX_HEADS * Ts, 1)
    scores = _sample_scores(page_table, qi_s.reshape(Bs, Ts, -1), wcol, ki_s.reshape(Bs, Ts, IDX_DIM),
                            cache_kidx[l])
    n_sel = min(TOPK_MAX, (past_len + Ts) // 4)
    bias = _sample_select(scores.reshape(n_s, -1), n_sel, 256)
    oa_s = _sample_attention(page_table, q_s.reshape(Bs, Ts, D_MODEL), bias.reshape(Bs, Ts, -1),
                             k_s.reshape(Bs, Ts, kvw), v_s.reshape(Bs, Ts, kvw),
                             cache_k[l].reshape(n_phys, PAGE_SIZE * N_KV_HEADS, HEAD_DIM),
                             cache_v[l].reshape(n_phys, PAGE_SIZE * N_KV_HEADS, HEAD_DIM))
    y_s = _ffn(x_sample.reshape(n_s, D_MODEL), oa_s.reshape(n_s, D_MODEL), ga_s.reshape(n_s, D_MODEL),
               cm_s.reshape(n_s, D_MODEL), p_sample[l].reshape(n_s, D_PLE), wout, norm_ffn[l], wg, wu, wd,
               norm_ple[l], wple, wpg, norm_final, tm=512, tf=D_FF // 2)

    return (y_p.reshape(Bp, Tp, D_MODEL), y_s.reshape(Bs, Ts, D_MODEL),
            k_p.reshape(1, Bp, Tp, N_KV_HEADS, HEAD_DIM), v_p.reshape(1, Bp, Tp, N_KV_HEADS, HEAD_DIM),
            ki_p.reshape(1, Bp, Tp, IDX_DIM), cnew_p.reshape(1, Bp, CONV_WIDTH - 1, D_CONV),
            k_s.reshape(1, Bs, Ts, N_KV_HEADS, HEAD_DIM), v_s.reshape(1, Bs, Ts, N_KV_HEADS, HEAD_DIM),
            ki_s.reshape(1, Bs, Ts, IDX_DIM),
            u_s.reshape(Bs, Ts, D_CONV)[:, Ts - (CONV_WIDTH - 1):, :].reshape(1, Bs, CONV_WIDTH - 1, D_CONV))
```

```python
import functools

import jax
import jax.numpy as jnp
from jax import lax
from jax.experimental import pallas as pl
from jax.experimental.pallas import tpu as pltpu

D_MODEL = 1024
N_HEADS = 8
N_KV_HEADS = 2
GROUP = N_HEADS // N_KV_HEADS
HEAD_DIM = 128
N_IDX_HEADS = 8
IDX_DIM = 64
IDX_SCALE = (N_IDX_HEADS * IDX_DIM) ** -0.5
QK_SCALE = HEAD_DIM ** -0.5
TOPK_MAX = 256
D_CONV = D_MODEL
CONV_WIDTH = 3
D_FF = 2816
D_PLE = 256
PAGE_SIZE = 128
ROPE_THETA = 10000.0
EPS = 1e-6

LANES = 128
SUBLANES = 8
KEY_STEP = 256
Q_BLOCK = 128
COUNT_STEP = 1024
COUNT_LANES = 8
VMEM_LIMIT = 56 * 1024 * 1024

W1_COLS = N_HEADS * HEAD_DIM + 2 * N_KV_HEADS * HEAD_DIM + N_IDX_HEADS * IDX_DIM + LANES
W2_COLS = 5 * D_MODEL

F32 = jnp.float32
BF16 = jnp.bfloat16
NEG_INF = float("-inf")
INT_MIN = -2 ** 31
INT_MAX = 2 ** 31 - 1
KEY_BITS = 32
M_FLOOR = -1e30
LOG2E = 1.4426950408889634


def _dot(a, b):
    return jnp.dot(a, b, preferred_element_type=F32)


def _dot_nt(a, b):
    return lax.dot_general(a, b, (((1,), (1,)), ((), ())), preferred_element_type=F32)


def _rmsnorm(x, g):
    var = jnp.mean(x * x, axis=-1, keepdims=True)
    return (x * lax.rsqrt(var + EPS)) * g


def _rope_table(pos):
    def tab(half, reps):
        freqs = ROPE_THETA ** (-jnp.arange(half, dtype=F32) / half)
        ang = pos.astype(F32)[:, None] * freqs[None, :]
        c, s = jnp.cos(ang), jnp.sin(ang)
        return jnp.tile(jnp.concatenate([c, c], 1), (1, reps)), jnp.tile(jnp.concatenate([-s, s], 1), (1, reps))
    c128, s128 = tab(HEAD_DIM // 2, 1)
    c64, s64 = tab(IDX_DIM // 2, LANES // IDX_DIM)
    return jnp.concatenate([c128, s128, c64, s64], axis=1)


def _rope128(x, cos, sin):
    return x * cos + pltpu.roll(x, HEAD_DIM // 2, axis=1) * sin


def _rope64(x, cos, sin, first_half):
    partner = jnp.where(first_half, pltpu.roll(x, LANES - IDX_DIM // 2, axis=1), pltpu.roll(x, IDX_DIM // 2, axis=1))
    return x * cos + partner * sin


def _inproj_kernel(*refs, mode, tm):
    if mode == "prompt":
        (x_ref, g_ref, w1_ref, w2_ref, cw_ref, tab_ref,
         q_ref, k_ref, v_ref, kbf_ref, vt_ref, qi_ref, ki_ref, kibf_ref, wit_ref, ga_ref, cm_ref, cnew_ref,
         carry_ref) = refs
    else:
        (x_ref, g_ref, w1_ref, w2_ref, cw_ref, tab_ref, prev_ref,
         q_ref, k_ref, v_ref, qi_ref, ki_ref, kibf_ref, wit_ref, ga_ref, cm_ref, u_ref) = refs

    x = x_ref[0]
    h = _rmsnorm(x, g_ref[...]).astype(BF16)
    z1 = _dot(h, w1_ref[...])
    tab = tab_ref[...]
    cos128, sin128 = tab[:, 0:LANES], tab[:, LANES:2 * LANES]
    cos64, sin64 = tab[:, 2 * LANES:3 * LANES], tab[:, 3 * LANES:4 * LANES]
    lane = lax.broadcasted_iota(jnp.int32, (tm, LANES), 1)
    first_half = (lane % IDX_DIM) < (IDX_DIM // 2)

    off = 0
    for hh in range(N_HEADS):
        sl = z1[:, off:off + HEAD_DIM]
        q_ref[0, :, hh * HEAD_DIM:(hh + 1) * HEAD_DIM] = (_rope128(sl, cos128, sin128) * (QK_SCALE * LOG2E)).astype(BF16)
        off += HEAD_DIM
    for hh in range(N_KV_HEADS):
        kr = _rope128(z1[:, off:off + HEAD_DIM], cos128, sin128)
        k_ref[0, :, hh * HEAD_DIM:(hh + 1) * HEAD_DIM] = kr
        if mode == "prompt":
            kbf_ref[0, :, hh * HEAD_DIM:(hh + 1) * HEAD_DIM] = kr.astype(BF16)
        off += HEAD_DIM
    v = z1[:, off:off + N_KV_HEADS * HEAD_DIM]
    v_ref[0] = v
    if mode == "prompt":
        for c in range(tm // KEY_STEP):
            vt_ref[0, c] = v[c * KEY_STEP:(c + 1) * KEY_STEP, :].T.astype(BF16)
    off += N_KV_HEADS * HEAD_DIM
    for hh in range(N_IDX_HEADS * IDX_DIM // LANES):
        sl = z1[:, off:off + LANES]
        qi_ref[0, :, hh * LANES:(hh + 1) * LANES] = _rope64(sl, cos64, sin64, first_half).astype(BF16)
        off += LANES
    kiw = z1[:, off:off + LANES]
    kir = _rope64(kiw, cos64, sin64, first_half)[:, 0:IDX_DIM]
    ki_ref[0] = kir
    kibf_ref[0] = kir.astype(BF16)
    wit_ref[0] = kiw.T[IDX_DIM:IDX_DIM + N_IDX_HEADS, :] * IDX_SCALE

    z2 = _dot(h, w2_ref[...])
    bg = z2[:, 0:D_MODEL]
    u = z2[:, D_MODEL:2 * D_MODEL] * z2[:, 2 * D_MODEL:3 * D_MODEL]
    ga = z2[:, 3 * D_MODEL:4 * D_MODEL]
    gb = z2[:, 4 * D_MODEL:5 * D_MODEL]

    row = lax.broadcasted_iota(jnp.int32, (tm, D_CONV), 0)
    r1 = pltpu.roll(u, 1, axis=0)
    r2 = pltpu.roll(u, 2, axis=0)
    if mode == "prompt":
        t = pl.program_id(1)

        @pl.when(t == 0)
        def _():
            carry_ref[...] = jnp.zeros_like(carry_ref)

        c0 = carry_ref[0:1, :]
        c1 = carry_ref[1:2, :]
        um1 = jnp.where(row == 0, c1, r1)
        um2 = jnp.where(row == 0, c0, jnp.where(row == 1, c1, r2))
        carry_ref[0:2, :] = u[tm - 2:tm, :]
        cnew_ref[0] = u[tm - 2:tm, :]
    else:
        prev = prev_ref[0]
        seq_row = row % SUBLANES
        um1 = jnp.where(seq_row == 0, pltpu.roll(prev, tm - 1, axis=0), r1)
        um2 = jnp.where(seq_row < 2, prev, r2)
        u_ref[0] = u
    cw = cw_ref[...]
    conv = cw[0:1, :] * um2 + cw[1:2, :] * um1 + cw[2:3, :] * u
    ga_ref[0] = jax.nn.sigmoid(ga)
    cm_ref[0] = jax.nn.sigmoid(gb) * (bg * conv)


def _const_spec(shape):
    nd = len(shape)
    return pl.BlockSpec(shape, lambda *_: (0,) * nd, pipeline_mode=pl.Buffered(1))


def _inproj(x, norm_g, w1, w2, conv_w, tab, prev, *, mode, tm):
    B, T, _ = x.shape
    nt = T // tm
    tok = lambda w: pl.BlockSpec((1, tm, w), lambda b, t: (b, t, 0))
    in_specs = [tok(D_MODEL), _const_spec((1, D_MODEL)), _const_spec((D_MODEL, W1_COLS)),
                _const_spec((D_MODEL, W2_COLS)), _const_spec((CONV_WIDTH, D_CONV)),
                pl.BlockSpec((tm, 4 * LANES), lambda b, t: (t, 0))]
    args = [x, norm_g.reshape(1, D_MODEL), w1, w2, conv_w, tab]
    kvw = N_KV_HEADS * HEAD_DIM
    qiw = N_IDX_HEADS * IDX_DIM
    wit_spec = pl.BlockSpec((1, N_IDX_HEADS, tm), lambda b, t: (b, 0, t))
    if mode == "prompt":
        out_shape = [
            jax.ShapeDtypeStruct((B, T, D_MODEL), BF16),
            jax.ShapeDtypeStruct((B, T, kvw), F32),
            jax.ShapeDtypeStruct((B, T, kvw), F32),
            jax.ShapeDtypeStruct((B, T, kvw), BF16),
            jax.ShapeDtypeStruct((B, T // KEY_STEP, kvw, KEY_STEP), BF16),
            jax.ShapeDtypeStruct((B, T, qiw), BF16),
            jax.ShapeDtypeStruct((B, T, IDX_DIM), F32),
            jax.ShapeDtypeStruct((B, T, IDX_DIM), BF16),
            jax.ShapeDtypeStruct((B, N_IDX_HEADS, T), F32),
            jax.ShapeDtypeStruct((B, T, D_MODEL), F32),
            jax.ShapeDtypeStruct((B, T, D_MODEL), F32),
            jax.ShapeDtypeStruct((B, CONV_WIDTH - 1, D_CONV), F32),
        ]
        out_specs = [tok(D_MODEL), tok(kvw), tok(kvw), tok(kvw),
                     pl.BlockSpec((1, tm // KEY_STEP, kvw, KEY_STEP), lambda b, t: (b, t, 0, 0)),
                     tok(qiw), tok(IDX_DIM), tok(IDX_DIM), wit_spec, tok(D_MODEL), tok(D_MODEL),
                     pl.BlockSpec((1, CONV_WIDTH - 1, D_CONV), lambda b, t: (b, 0, 0))]
        scratch = [pltpu.VMEM((SUBLANES, D_CONV), F32)]
    else:
        in_specs.append(tok(D_CONV))
        args.append(prev)
        out_shape = [
            jax.ShapeDtypeStruct((B, T, D_MODEL), BF16),
            jax.ShapeDtypeStruct((B, T, kvw), F32),
            jax.ShapeDtypeStruct((B, T, kvw), F32),
            jax.ShapeDtypeStruct((B, T, qiw), BF16),
            jax.ShapeDtypeStruct((B, T, IDX_DIM), F32),
            jax.ShapeDtypeStruct((B, T, IDX_DIM), BF16),
            jax.ShapeDtypeStruct((B, N_IDX_HEADS, T), F32),
            jax.ShapeDtypeStruct((B, T, D_MODEL), F32),
            jax.ShapeDtypeStruct((B, T, D_MODEL), F32),
            jax.ShapeDtypeStruct((B, T, D_CONV), F32),
        ]
        out_specs = [tok(D_MODEL), tok(kvw), tok(kvw), tok(qiw), tok(IDX_DIM), tok(IDX_DIM), wit_spec,
                     tok(D_MODEL), tok(D_MODEL), tok(D_CONV)]
        scratch = []
    return pl.pallas_call(
        functools.partial(_inproj_kernel, mode=mode, tm=tm),
        grid=(B, nt),
        in_specs=in_specs,
        out_specs=out_specs,
        out_shape=out_shape,
        scratch_shapes=scratch,
        compiler_params=pltpu.CompilerParams(dimension_semantics=("arbitrary", "arbitrary"),
                                             vmem_limit_bytes=VMEM_LIMIT),
        name="inproj_" + mode,
    )(*args)


def _sort_key(x):
    bits = lax.bitcast_convert_type(x, jnp.int32)
    return bits ^ ((bits >> 31) & 0x7FFFFFFF)


def _search_keys(count_ge, lo0, hi0, kk, n_lo0):
    def body(_, c):
        lo, hi, n_lo, n_hi, done = c
        mid = (lo & hi) + ((lo ^ hi) >> 1)
        stuck = mid == lo
        n_mid = count_ge(mid)
        upd = (done < 0.5) & jnp.logical_not(stuck)
        go_lo = n_mid >= kk
        up_lo = upd & go_lo
        lo = jnp.where(up_lo, mid, lo)
        n_lo = jnp.where(up_lo, n_mid, n_lo)
        up_hi = upd & jnp.logical_not(go_lo)
        hi = jnp.where(up_hi, mid, hi)
        n_hi = jnp.where(up_hi, n_mid, n_hi)
        done = jnp.where(stuck | (upd & (n_mid == kk)), 1.0, done)
        return lo, hi, n_lo, n_hi, done

    done0 = jnp.where(n_lo0 <= kk, 1.0, 0.0)
    lo, hi, n_lo, n_hi, _ = lax.fori_loop(0, KEY_BITS, body, (lo0, hi0, n_lo0, jnp.zeros_like(kk), done0))
    return lo, hi, n_lo, n_hi


def _fold_rows(x, op):
    return op(x.reshape(x.shape[0] // SUBLANES, SUBLANES, x.shape[1]), axis=0)


def _prompt_attn_kernel(qi_ref, wit_ref, q_ref, ki_ref, k_ref, vt_ref, o_ref,
                        key_scr, b_scr, d_a, d_b, s_a, s_b, p_a, p_b, acc_scr, *, n_sel, count_step):
    j = pl.program_id(1)
    per_step = KEY_STEP // Q_BLOCK
    n_steps = (j + per_step) // per_step
    n_pairs = (n_steps + 1) // 2
    last = n_steps - 1
    per_count = count_step // KEY_STEP
    n_count = (n_steps + per_count - 1) // per_count
    qi = qi_ref[0]
    q_stack = jnp.concatenate([qi[:, h * IDX_DIM:(h + 1) * IDX_DIM] for h in range(N_IDX_HEADS)], axis=0)
    wit = wit_ref[0]
    q = q_ref[0]
    q_grp = [jnp.concatenate([q[:, (kv * GROUP + g) * HEAD_DIM:(kv * GROUP + g + 1) * HEAD_DIM]
                              for g in range(GROUP)], axis=0) for kv in range(N_KV_HEADS)]
    key_pos = lax.broadcasted_iota(jnp.int32, (KEY_STEP, Q_BLOCK), 0)
    q_pos = lax.broadcasted_iota(jnp.int32, (KEY_STEP, Q_BLOCK), 1) + j * Q_BLOCK

    def step(i):
        return pl.ds(pl.multiple_of(i * KEY_STEP, KEY_STEP), KEY_STEP)

    def idx_dots(i):
        return _dot_nt(ki_ref[0, step(i), :], q_stack)

    def score_trip(i, d_cur, d_nxt, carry):
        mn, mx = carry
        ic = jnp.minimum(i, last)
        d_nxt[...] = idx_dots(jnp.minimum(i + 1, last))
        acc = wit[0:1, :] * jnp.maximum(d_cur[:, 0:Q_BLOCK], 0.0)
        for h in range(1, N_IDX_HEADS):
            acc = acc + wit[h:h + 1, :] * jnp.maximum(d_cur[:, h * Q_BLOCK:(h + 1) * Q_BLOCK], 0.0)
        key = _sort_key(acc)
        masked = jnp.where(key_pos + ic * KEY_STEP <= q_pos, key, INT_MIN)
        key_scr[step(ic), :] = masked
        return jnp.minimum(mn, _fold_rows(key, jnp.min)), jnp.maximum(mx, _fold_rows(masked, jnp.max))

    d_a[...] = idx_dots(0)

    def score_pair(pi, carry):
        carry = score_trip(2 * pi, d_a, d_b, carry)
        return score_trip(2 * pi + 1, d_b, d_a, carry)

    mn, mx = lax.fori_loop(0, n_pairs, score_pair, (jnp.full((SUBLANES, Q_BLOCK), INT_MAX, jnp.int32),
                                                    jnp.full((SUBLANES, Q_BLOCK), INT_MIN, jnp.int32)))
    lo0 = jnp.min(mn, axis=0, keepdims=True)
    hi0 = jnp.max(mx, axis=0, keepdims=True) + 1

    def fill(i, c):
        key_scr[step(i), :] = jnp.full((KEY_STEP, Q_BLOCK), INT_MIN, jnp.int32)
        return c

    lax.fori_loop(n_steps, n_count * per_count, fill, 0)

    def count_ge(t):
        tb = jnp.broadcast_to(t, (SUBLANES, Q_BLOCK))

        def body(i, accs):
            accs = list(accs)
            base = pl.multiple_of(i * count_step, count_step)
            for r in range(count_step // SUBLANES):
                keys = key_scr[pl.ds(base + r * SUBLANES, SUBLANES), :]
                accs[r % COUNT_LANES] = accs[r % COUNT_LANES] + jnp.where(keys >= tb, 1.0, 0.0)
            return tuple(accs)

        accs = lax.fori_loop(0, n_count, body, tuple(jnp.zeros((SUBLANES, Q_BLOCK), F32) for _ in range(COUNT_LANES)))
        return jnp.sum(functools.reduce(lambda a, b: a + b, accs), axis=0, keepdims=True)

    n_valid = (j * Q_BLOCK + 1 + lax.broadcasted_iota(jnp.int32, (1, Q_BLOCK), 1)).astype(F32)
    kk = jnp.minimum(n_valid, float(n_sel))
    lo, hi, n_lo, n_hi = lax.cond((j + 1) * Q_BLOCK <= n_sel,
                                  lambda: (lo0, hi0, n_valid, jnp.zeros_like(kk)),
                                  lambda: _search_keys(count_ge, lo0, hi0, kk, n_valid))
    need = kk - n_hi

    def mask_plain(i, c):
        b_scr[step(i), :] = jnp.where(key_scr[step(i), :] >= lo, 0.0, NEG_INF)
        return c

    def mask_ties(i, run):
        tri = jnp.where(lax.broadcasted_iota(jnp.int32, (KEY_STEP, KEY_STEP), 1)
                        <= lax.broadcasted_iota(jnp.int32, (KEY_STEP, KEY_STEP), 0), 1.0, 0.0).astype(BF16)
        key = key_scr[step(i), :]
        above = key >= hi
        elig = (key >= lo) & jnp.logical_not(above)
        rank = run + _dot(tri, jnp.where(elig, 1.0, 0.0).astype(BF16))
        b_scr[step(i), :] = jnp.where(above | (elig & (rank <= need)), 0.0, NEG_INF)
        return rank[KEY_STEP - 1:KEY_STEP, :]

    any_tie = jnp.max(jnp.where(n_lo > kk, 1.0, 0.0)) > 0.5

    @pl.when(any_tie)
    def _():
        lax.fori_loop(0, n_steps, mask_ties, jnp.zeros((1, Q_BLOCK), F32))

    @pl.when(jnp.logical_not(any_tie))
    def _():
        lax.fori_loop(0, n_steps, mask_plain, 0)

    def logits(i, kv):
        return _dot_nt(k_ref[0, step(i), kv * HEAD_DIM:(kv + 1) * HEAD_DIM], q_grp[kv])

    def attn_trip(i, s_cur, s_nxt, p_cur, p_prv, carry):
        m, l = carry
        ic = jnp.minimum(i, last)
        pv = [_dot(vt_ref[0, jnp.clip(i - 1, 0, last), kv * HEAD_DIM:(kv + 1) * HEAD_DIM, :], p_prv[kv])
              for kv in range(N_KV_HEADS)]
        for kv in range(N_KV_HEADS):
            s_nxt[kv] = logits(jnp.minimum(i + 1, last), kv)
        b = b_scr[step(ic), :] + jnp.where(i <= last, 0.0, NEG_INF)
        m_rows, l_new = [], []
        for kv in range(N_KV_HEADS):
            ps, alphas = [], []
            for g in range(GROUP):
                h = kv * GROUP + g
                sg = s_cur[kv, :, g * Q_BLOCK:(g + 1) * Q_BLOCK] + b
                m_old = m[h:h + 1, :]
                m_new = jnp.maximum(m_old, jnp.max(_fold_rows(sg, jnp.max), axis=0, keepdims=True))
                alpha = jnp.exp2(m_old - m_new)
                p = jnp.exp2(sg - m_new)
                l_new.append(alpha * l[h] + _fold_rows(p, jnp.sum))
                ps.append(p.astype(BF16))
                alphas.append(alpha)
                m_rows.append(m_new)
            p_cur[kv] = jnp.concatenate(ps, axis=1)
            acc_scr[kv] = (acc_scr[kv] + pv[kv]) * jnp.concatenate(alphas, axis=1)
        return jnp.concatenate(m_rows, axis=0), tuple(l_new)

    acc_scr[...] = jnp.zeros_like(acc_scr)
    p_b[...] = jnp.zeros_like(p_b)
    for kv in range(N_KV_HEADS):
        s_a[kv] = logits(0, kv)

    def attn_pair(pi, carry):
        carry = attn_trip(2 * pi, s_a, s_b, p_a, p_b, carry)
        return attn_trip(2 * pi + 1, s_b, s_a, p_b, p_a, carry)

    m, l = lax.fori_loop(0, n_pairs, attn_pair, (jnp.full((N_HEADS, Q_BLOCK), M_FLOOR, F32),
                                                 tuple(jnp.zeros((SUBLANES, Q_BLOCK), F32) for _ in range(N_HEADS))))
    i_end = jnp.minimum(2 * n_pairs - 1, last)
    for h in range(N_HEADS):
        kv, g = divmod(h, GROUP)
        if g == 0:
            acc_kv = acc_scr[kv] + _dot(vt_ref[0, i_end, kv * HEAD_DIM:(kv + 1) * HEAD_DIM, :], p_b[kv])
        inv_l = 1.0 / jnp.sum(l[h], axis=0, keepdims=True)
        o_ref[0, :, h * HEAD_DIM:(h + 1) * HEAD_DIM] = (acc_kv[:, g * Q_BLOCK:(g + 1) * Q_BLOCK] * inv_l).T


def _prompt_attention(qi, wit, q, ki_bf, k_bf, vt):
    B, T, _ = q.shape
    nq = T // Q_BLOCK
    n_sel = min(TOPK_MAX, T // 4)
    kvw = N_KV_HEADS * HEAD_DIM
    count_step = min(COUNT_STEP, T)
    gq = GROUP * Q_BLOCK
    return pl.pallas_call(
        functools.partial(_prompt_attn_kernel, n_sel=n_sel, count_step=count_step),
        grid=(B, nq),
        in_specs=[pl.BlockSpec((1, Q_BLOCK, N_IDX_HEADS * IDX_DIM), lambda b, j: (b, j, 0)),
                  pl.BlockSpec((1, N_IDX_HEADS, Q_BLOCK), lambda b, j: (b, 0, j)),
                  pl.BlockSpec((1, Q_BLOCK, D_MODEL), lambda b, j: (b, j, 0)),
                  pl.BlockSpec((1, T, IDX_DIM), lambda b, j: (b, 0, 0)),
                  pl.BlockSpec((1, T, kvw), lambda b, j: (b, 0, 0)),
                  pl.BlockSpec((1, T // KEY_STEP, kvw, KEY_STEP), lambda b, j: (b, 0, 0, 0))],
        out_specs=pl.BlockSpec((1, Q_BLOCK, D_MODEL), lambda b, j: (b, j, 0)),
        out_shape=jax.ShapeDtypeStruct((B, T, D_MODEL), F32),
        scratch_shapes=[pltpu.VMEM((T, Q_BLOCK), jnp.int32), pltpu.VMEM((T, Q_BLOCK), F32),
                        pltpu.VMEM((KEY_STEP, N_IDX_HEADS * Q_BLOCK), F32),
                        pltpu.VMEM((KEY_STEP, N_IDX_HEADS * Q_BLOCK), F32),
                        pltpu.VMEM((N_KV_HEADS, KEY_STEP, gq), F32), pltpu.VMEM((N_KV_HEADS, KEY_STEP, gq), F32),
                        pltpu.VMEM((N_KV_HEADS, KEY_STEP, gq), BF16), pltpu.VMEM((N_KV_HEADS, KEY_STEP, gq), BF16),
                        pltpu.VMEM((N_KV_HEADS, HEAD_DIM, gq), F32)],
        compiler_params=pltpu.CompilerParams(dimension_semantics=("arbitrary", "arbitrary"),
                                             vmem_limit_bytes=VMEM_LIMIT),
        name="prompt_attention",
    )(qi, wit, q, ki_bf, k_bf, vt)


def _sample_score_kernel(pt_ref, qi_ref, wcol_ref, kin_ref, *rest, n_pages, ts):
    page_refs, s_ref = rest[:n_pages], rest[n_pages]
    qi = qi_ref[0].astype(F32)
    q_all = jnp.concatenate([qi[:, h * IDX_DIM:(h + 1) * IDX_DIM] for h in range(N_IDX_HEADS)], axis=0).astype(BF16)
    wcol = jnp.broadcast_to(wcol_ref[0], (N_IDX_HEADS * ts, LANES))

    def score(keys_bf):
        r = jnp.maximum(_dot_nt(q_all, keys_bf), 0.0) * wcol
        acc = r[0:ts]
        for h in range(1, N_IDX_HEADS):
            acc = acc + r[h * ts:(h + 1) * ts]
        return acc

    for p in range(n_pages):
        s_ref[0, :, p * PAGE_SIZE:(p + 1) * PAGE_SIZE] = score(page_refs[p][0].astype(BF16))
    new_keys = jnp.concatenate([kin_ref[0], jnp.zeros((PAGE_SIZE - ts, IDX_DIM), F32)], axis=0).astype(BF16)
    lane = lax.broadcasted_iota(jnp.int32, (ts, PAGE_SIZE), 1)
    row = lax.broadcasted_iota(jnp.int32, (ts, PAGE_SIZE), 0)
    s_ref[0, :, n_pages * PAGE_SIZE:(n_pages + 1) * PAGE_SIZE] = jnp.where(lane <= row, score(new_keys), NEG_INF)


def _sample_scores(page_table, qi, wcol, ki_bf, cache_kidx):
    nb, ts, _ = qi.shape
    n_pages = page_table.shape[1]

    def page_spec(p):
        return pl.BlockSpec((1, PAGE_SIZE, IDX_DIM), lambda b, pt: (pt[b, p], 0, 0))

    grid_spec = pltpu.PrefetchScalarGridSpec(
        num_scalar_prefetch=1,
        grid=(nb,),
        in_specs=[pl.BlockSpec((1, ts, N_IDX_HEADS * IDX_DIM), lambda b, pt: (b, 0, 0)),
                  pl.BlockSpec((1, N_IDX_HEADS * ts, 1), lambda b, pt: (b, 0, 0)),
                  pl.BlockSpec((1, ts, IDX_DIM), lambda b, pt: (b, 0, 0))] + [page_spec(p) for p in range(n_pages)],
        out_specs=pl.BlockSpec((1, ts, (n_pages + 1) * PAGE_SIZE), lambda b, pt: (b, 0, 0)),
    )
    return pl.pallas_call(
        functools.partial(_sample_score_kernel, n_pages=n_pages, ts=ts),
        grid_spec=grid_spec,
        out_shape=jax.ShapeDtypeStruct((nb, ts, (n_pages + 1) * PAGE_SIZE), F32),
        compiler_params=pltpu.CompilerParams(dimension_semantics=("arbitrary",), vmem_limit_bytes=VMEM_LIMIT),
        name="sample_scores",
    )(page_table, qi, wcol, ki_bf, *([cache_kidx] * n_pages))


def _sample_select_kernel(s_ref, b_ref, key_scr, *, n_sel, n_chunks):
    rows = s_ref.shape[0]
    tri = jnp.where(lax.broadcasted_iota(jnp.int32, (LANES, LANES), 0)
                    <= lax.broadcasted_iota(jnp.int32, (LANES, LANES), 1), 1.0, 0.0).astype(BF16)

    def cols(c):
        return slice(c * LANES, (c + 1) * LANES)

    mn = jnp.full((rows, LANES), INT_MAX, jnp.int32)
    mx = jnp.full((rows, LANES), INT_MIN, jnp.int32)
    for c in range(n_chunks):
        s = s_ref[:, cols(c)]
        valid = s > NEG_INF
        key = _sort_key(s)
        key_scr[:, cols(c)] = jnp.where(valid, key, INT_MIN)
        mn = jnp.minimum(mn, jnp.where(valid, key, INT_MAX))
        mx = jnp.maximum(mx, jnp.where(valid, key, INT_MIN))
    lo0 = jnp.min(mn, axis=1, keepdims=True)
    hi0 = jnp.max(mx, axis=1, keepdims=True) + 1

    def count_ge(t):
        acc = jnp.where(key_scr[:, cols(0)] >= t, 1.0, 0.0)
        for c in range(1, n_chunks):
            acc = acc + jnp.where(key_scr[:, cols(c)] >= t, 1.0, 0.0)
        return jnp.sum(acc, axis=1, keepdims=True)

    kk = jnp.full((rows, 1), float(n_sel), F32)
    lo, hi, _, n_hi = _search_keys(count_ge, lo0, hi0, kk, count_ge(lo0))
    need = kk - n_hi
    run = jnp.zeros((rows, 1), F32)
    for c in range(n_chunks):
        key = key_scr[:, cols(c)]
        above = key >= hi
        elig = (key >= lo) & jnp.logical_not(above)
        rank = run + _dot(jnp.where(elig, 1.0, 0.0).astype(BF16), tri)
        sel = above | (elig & (rank <= need))
        b_ref[:, cols(c)] = jnp.where(sel, 0.0, NEG_INF)
        run = rank[:, LANES - 1:LANES]


def _sample_select(scores, n_sel, rows_per_step):
    rows, width = scores.shape
    return pl.pallas_call(
        functools.partial(_sample_select_kernel, n_sel=n_sel, n_chunks=width // LANES),
        grid=(rows // rows_per_step,),
        in_specs=[pl.BlockSpec((rows_per_step, width), lambda i: (i, 0))],
        out_specs=pl.BlockSpec((rows_per_step, width), lambda i: (i, 0)),
        out_shape=jax.ShapeDtypeStruct((rows, width), F32),
        scratch_shapes=[pltpu.VMEM((rows_per_step, width), jnp.int32)],
        compiler_params=pltpu.CompilerParams(dimension_semantics=("arbitrary",), vmem_limit_bytes=VMEM_LIMIT),
        name="sample_select",
    )(scores)


def _sample_attn_kernel(pt_ref, q_ref, b_ref, kn_ref, vn_ref, *rest, n_pages, ts):
    k_refs, v_refs, o_ref = rest[:n_pages], rest[n_pages:2 * n_pages], rest[2 * n_pages]
    q = q_ref[0].astype(F32)
    bias = b_ref[0]
    pad = jnp.zeros((PAGE_SIZE - ts, HEAD_DIM), F32)
    for kv in range(N_KV_HEADS):
        qg = jnp.concatenate([q[:, (kv * GROUP + g) * HEAD_DIM:(kv * GROUP + g + 1) * HEAD_DIM]
                              for g in range(GROUP)], axis=0).astype(BF16)
        s_tiles, v_tiles = [], []
        for p in range(n_pages + 1):
            if p < n_pages:
                kp = k_refs[p][0, pl.ds(kv, PAGE_SIZE, stride=N_KV_HEADS), :].astype(BF16)
                vp = v_refs[p][0, pl.ds(kv, PAGE_SIZE, stride=N_KV_HEADS), :].astype(BF16)
            else:
                kp = jnp.concatenate([kn_ref[0, :, kv * HEAD_DIM:(kv + 1) * HEAD_DIM], pad], axis=0).astype(BF16)
                vp = jnp.concatenate([vn_ref[0, :, kv * HEAD_DIM:(kv + 1) * HEAD_DIM], pad], axis=0).astype(BF16)
            bp = bias[:, p * PAGE_SIZE:(p + 1) * PAGE_SIZE]
            s_tiles.append(_dot_nt(qg, kp) + jnp.concatenate([bp] * GROUP, axis=0))
            v_tiles.append(vp)
        m = s_tiles[0]
        for s in s_tiles[1:]:
            m = jnp.maximum(m, s)
        m = jnp.max(m, axis=1, keepdims=True)
        l = jnp.zeros((GROUP * ts, PAGE_SIZE), F32)
        acc = jnp.zeros((GROUP * ts, HEAD_DIM), F32)
        for s, vp in zip(s_tiles, v_tiles):
            p_ = jnp.exp2(s - m)
            l = l + p_
            acc = acc + _dot(p_.astype(BF16), vp)
        o = acc / jnp.sum(l, axis=1, keepdims=True)
        for g in range(GROUP):
            hh = kv * GROUP + g
            o_ref[0, :, hh * HEAD_DIM:(hh + 1) * HEAD_DIM] = o[g * ts:(g + 1) * ts, :]


def _sample_attention(page_table, q, bias, k_new, v_new, cache_k, cache_v):
    nb, ts, _ = q.shape
    n_pages = page_table.shape[1]
    kvw = N_KV_HEADS * HEAD_DIM

    def page_spec(p):
        return pl.BlockSpec((1, PAGE_SIZE * N_KV_HEADS, HEAD_DIM), lambda b, pt: (pt[b, p], 0, 0))

    grid_spec = pltpu.PrefetchScalarGridSpec(
        num_scalar_prefetch=1,
        grid=(nb,),
        in_specs=[pl.BlockSpec((1, ts, D_MODEL), lambda b, pt: (b, 0, 0)),
                  pl.BlockSpec((1, ts, (n_pages + 1) * PAGE_SIZE), lambda b, pt: (b, 0, 0)),
                  pl.BlockSpec((1, ts, kvw), lambda b, pt: (b, 0, 0)),
                  pl.BlockSpec((1, ts, kvw), lambda b, pt: (b, 0, 0))]
                 + [page_spec(p) for p in range(n_pages)] * 2,
        out_specs=pl.BlockSpec((1, ts, D_MODEL), lambda b, pt: (b, 0, 0)),
    )
    return pl.pallas_call(
        functools.partial(_sample_attn_kernel, n_pages=n_pages, ts=ts),
        grid_spec=grid_spec,
        out_shape=jax.ShapeDtypeStruct((nb, ts, D_MODEL), F32),
        compiler_params=pltpu.CompilerParams(dimension_semantics=("arbitrary",), vmem_limit_bytes=VMEM_LIMIT),
        name="sample_attention",
    )(page_table, q, bias, k_new, v_new, *([cache_k] * n_pages), *([cache_v] * n_pages))


def _ffn_kernel(x_ref, oa_ref, ga_ref, cm_ref, p_ref, wout_ref, gffn_ref, wg_ref, wu_ref, wd_ref,
                gple_ref, wple_ref, wpg_ref, gfin_ref, y_ref, x1_scr, h_scr, acc_scr):
    f = pl.program_id(1)

    @pl.when(f == 0)
    def _():
        merged = ga_ref[...] * oa_ref[...] + cm_ref[...]
        x1 = x_ref[...] + _dot(merged.astype(BF16), wout_ref[...])
        x1_scr[...] = x1
        h_scr[...] = _rmsnorm(x1, gffn_ref[...]).astype(BF16)
        acc_scr[...] = jnp.zeros_like(acc_scr)

    h = h_scr[...]
    g = _dot(h, wg_ref[...])
    u = _dot(h, wu_ref[...])
    acc_scr[...] += _dot((g * jax.nn.sigmoid(g) * u).astype(BF16), wd_ref[...])

    @pl.when(f == pl.num_programs(1) - 1)
    def _():
        x2 = x1_scr[...] + acc_scr[...]
        gate = jax.nn.sigmoid(_dot(_rmsnorm(x2, gple_ref[...]).astype(BF16), wpg_ref[...]))
        x3 = x2 + _dot(p_ref[...].astype(BF16), wple_ref[...]) * gate
        y_ref[...] = _rmsnorm(x3, gfin_ref[...])


def _ffn(x, oa, ga, cm, p, wout, gffn, wg, wu, wd, gple, wple, wpg, gfin, *, tm, tf):
    n = x.shape[0]
    tm = min(tm, n)
    tok = lambda w: pl.BlockSpec((tm, w), lambda i, f: (i, 0))
    vec = lambda a: a.reshape(1, D_MODEL)
    return pl.pallas_call(
        _ffn_kernel,
        grid=(n // tm, D_FF // tf),
        in_specs=[tok(D_MODEL), tok(D_MODEL), tok(D_MODEL), tok(D_MODEL), tok(D_PLE),
                  _const_spec((D_MODEL, D_MODEL)), _const_spec((1, D_MODEL)),
                  pl.BlockSpec((D_MODEL, tf), lambda i, f: (0, f)),
                  pl.BlockSpec((D_MODEL, tf), lambda i, f: (0, f)),
                  pl.BlockSpec((tf, D_MODEL), lambda i, f: (f, 0)),
                  _const_spec((1, D_MODEL)), _const_spec((D_PLE, D_MODEL)), _const_spec((D_MODEL, D_MODEL)),
                  _const_spec((1, D_MODEL))],
        out_specs=tok(D_MODEL),
        out_shape=jax.ShapeDtypeStruct((n, D_MODEL), F32),
        scratch_shapes=[pltpu.VMEM((tm, D_MODEL), F32), pltpu.VMEM((tm, D_MODEL), BF16),
                        pltpu.VMEM((tm, D_MODEL), F32)],
        compiler_params=pltpu.CompilerParams(dimension_semantics=("arbitrary", "arbitrary"),
                                             vmem_limit_bytes=VMEM_LIMIT),
        name="ffn",
    )(x, oa, ga, cm, p, wout, vec(gffn), wg, wu, wd, vec(gple), wple, wpg, vec(gfin))


def _split_w_in(w_in):
    n_att = N_HEADS * HEAD_DIM + 2 * N_KV_HEADS * HEAD_DIM + N_IDX_HEADS * IDX_DIM + IDX_DIM + N_IDX_HEADS
    pad = jnp.zeros((D_MODEL, W1_COLS - n_att), w_in.dtype)
    w1 = jnp.concatenate([w_in[:, :n_att], pad], axis=1).astype(BF16)
    w2 = w_in[:, n_att:].astype(BF16)
    return w1, w2


def kernel(x_prompt, x_sample, cache_k, cache_v, cache_kidx, state_conv, page_table, p_prompt, p_sample, norm_mix, w_in, conv_w, w_out, norm_ffn, w_gate_up, w_down, norm_ple, w_ple, w_ple_gate, norm_final):
    Bp, Tp, _ = x_prompt.shape
    Bs, Ts, _ = x_sample.shape
    n_pages = page_table.shape[1]
    past_len = n_pages * PAGE_SIZE
    n_phys = cache_k.shape[1]
    kvw = N_KV_HEADS * HEAD_DIM
    l = 0

    w1, w2 = _split_w_in(w_in[l])
    wout = w_out[l].astype(BF16)
    wg = w_gate_up[l][:, :D_FF].astype(BF16)
    wu = w_gate_up[l][:, D_FF:].astype(BF16)
    wd = w_down[l].astype(BF16)
    wple = w_ple[l].astype(BF16)
    wpg = w_ple_gate[l].astype(BF16)

    tab_p = _rope_table(jnp.arange(Tp))
    (q_p, k_p, v_p, kbf_p, vt_p, qi_p, ki_p, kibf_p, wit_p, ga_p, cm_p, cnew_p) = _inproj(
        x_prompt, norm_mix[l], w1, w2, conv_w[l], tab_p, None, mode="prompt", tm=256)
    oa_p = _prompt_attention(qi_p, wit_p, q_p, kibf_p, kbf_p, vt_p)
    n_p = Bp * Tp
    y_p = _ffn(x_prompt.reshape(n_p, D_MODEL), oa_p.reshape(n_p, D_MODEL), ga_p.reshape(n_p, D_MODEL),
               cm_p.reshape(n_p, D_MODEL), p_prompt[l].reshape(n_p, D_PLE), wout, norm_ffn[l], wg, wu, wd,
               norm_ple[l], wple, wpg, norm_final, tm=512, tf=D_FF // 2)

    n_s = Bs * Ts
    tm_s = 256
    tab_s = _rope_table(past_len + (jnp.arange(tm_s) % Ts))
    prev = jnp.concatenate([state_conv[l], jnp.zeros((Bs, Ts - (CONV_WIDTH - 1), D_CONV), F32)], axis=1)
    (q_s, k_s, v_s, qi_s, ki_s, kibf_s, wit_s, ga_s, cm_s, u_s) = _inproj(
        x_sample.reshape(n_s // tm_s, tm_s, D_MODEL), norm_mix[l], w1, w2, conv_w[l], tab_s,
        prev.reshape(n_s // tm_s, tm_s, D_CONV), mode="sample", tm=tm_s)
    wcol = wit_s.transpose(1, 0, 2).reshape(N_IDX_HEADS, Bs, Ts).transpose(1, 0, 2).reshape(Bs, N_IDX_HEADS * Ts, 1)
    scores = _sample_scores(page_table, qi_s.reshape(Bs, Ts, -1), wcol, ki_s.reshape(Bs, Ts, IDX_DIM),
                            cache_kidx[l])
    n_sel = min(TOPK_MAX, (past_len + Ts) // 4)
    bias = _sample_select(scores.reshape(n_s, -1), n_sel, 256)
    oa_s = _sample_attention(page_table, q_s.reshape(Bs, Ts, D_MODEL), bias.reshape(Bs, Ts, -1),
                             k_s.reshape(Bs, Ts, kvw), v_s.reshape(Bs, Ts, kvw),
                             cache_k[l].reshape(n_phys, PAGE_SIZE * N_KV_HEADS, HEAD_DIM),
                             cache_v[l].reshape(n_phys, PAGE_SIZE * N_KV_HEADS, HEAD_DIM))
    y_s = _ffn(x_sample.reshape(n_s, D_MODEL), oa_s.reshape(n_s, D_MODEL), ga_s.reshape(n_s, D_MODEL),
               cm_s.reshape(n_s, D_MODEL), p_sample[l].reshape(n_s, D_PLE), wout, norm_ffn[l], wg, wu, wd,
               norm_ple[l], wple, wpg, norm_final, tm=512, tf=D_FF // 2)

    return (y_p.reshape(Bp, Tp, D_MODEL), y_s.reshape(Bs, Ts, D_MODEL),
            k_p.reshape(1, Bp, Tp, N_KV_HEADS, HEAD_DIM), v_p.reshape(1, Bp, Tp, N_KV_HEADS, HEAD_DIM),
            ki_p.reshape(1, Bp, Tp, IDX_DIM), cnew_p.reshape(1, Bp, CONV_WIDTH - 1, D_CONV),
            k_s.reshape(1, Bs, Ts, N_KV_HEADS, HEAD_DIM), v_s.reshape(1, Bs, Ts, N_KV_HEADS, HEAD_DIM),
            ki_s.reshape(1, Bs, Ts, IDX_DIM),
            u_s.reshape(Bs, Ts, D_CONV)[:, Ts - (CONV_WIDTH - 1):, :].reshape(1, Bs, CONV_WIDTH - 1, D_CONV))
```

```python
import functools

import jax
import jax.numpy as jnp
from jax import lax
from jax.experimental import pallas as pl
from jax.experimental.pallas import tpu as pltpu

D_MODEL = 1024
N_HEADS = 8
N_KV_HEADS = 2
GROUP = N_HEADS // N_KV_HEADS
HEAD_DIM = 128
N_IDX_HEADS = 8
IDX_DIM = 64
IDX_SCALE = (N_IDX_HEADS * IDX_DIM) ** -0.5
QK_SCALE = HEAD_DIM ** -0.5
TOPK_MAX = 256
D_CONV = D_MODEL
CONV_WIDTH = 3
D_FF = 2816
D_PLE = 256
PAGE_SIZE = 128
ROPE_THETA = 10000.0
EPS = 1e-6

LANES = 128
SUBLANES = 8
KEY_STEP = 256
Q_BLOCK = 128
COUNT_STEP = 1024
COUNT_LANES = 8
VMEM_LIMIT = 56 * 1024 * 1024

W1_COLS = N_HEADS * HEAD_DIM + 2 * N_KV_HEADS * HEAD_DIM + N_IDX_HEADS * IDX_DIM + LANES
W2_COLS = 5 * D_MODEL

F32 = jnp.float32
BF16 = jnp.bfloat16
NEG_INF = float("-inf")
INT_MIN = -2 ** 31
INT_MAX = 2 ** 31 - 1
KEY_BITS = 32
M_FLOOR = -1e30
LOG2E = 1.4426950408889634


def _dot(a, b):
    return jnp.dot(a, b, preferred_element_type=F32)


def _dot_nt(a, b):
    return lax.dot_general(a, b, (((1,), (1,)), ((), ())), preferred_element_type=F32)


def _rmsnorm(x, g):
    var = jnp.mean(x * x, axis=-1, keepdims=True)
    return (x * lax.rsqrt(var + EPS)) * g


def _rope_table(pos):
    def tab(half, reps):
        freqs = ROPE_THETA ** (-jnp.arange(half, dtype=F32) / half)
        ang = pos.astype(F32)[:, None] * freqs[None, :]
        c, s = jnp.cos(ang), jnp.sin(ang)
        return jnp.tile(jnp.concatenate([c, c], 1), (1, reps)), jnp.tile(jnp.concatenate([-s, s], 1), (1, reps))
    c128, s128 = tab(HEAD_DIM // 2, 1)
    c64, s64 = tab(IDX_DIM // 2, LANES // IDX_DIM)
    return jnp.concatenate([c128, s128, c64, s64], axis=1)


def _rope128(x, cos, sin):
    return x * cos + pltpu.roll(x, HEAD_DIM // 2, axis=1) * sin


def _rope64(x, cos, sin, first_half):
    partner = jnp.where(first_half, pltpu.roll(x, LANES - IDX_DIM // 2, axis=1), pltpu.roll(x, IDX_DIM // 2, axis=1))
    return x * cos + partner * sin


def _inproj_kernel(*refs, mode, tm):
    if mode == "prompt":
        (x_ref, g_ref, w1_ref, w2_ref, cw_ref, tab_ref,
         q_ref, k_ref, v_ref, kbf_ref, vt_ref, qi_ref, ki_ref, kibf_ref, wit_ref, ga_ref, cm_ref, cnew_ref,
         carry_ref) = refs
    else:
        (x_ref, g_ref, w1_ref, w2_ref, cw_ref, tab_ref, prev_ref,
         q_ref, k_ref, v_ref, qi_ref, ki_ref, kibf_ref, wit_ref, ga_ref, cm_ref, u_ref) = refs

    x = x_ref[0]
    h = _rmsnorm(x, g_ref[...]).astype(BF16)
    z1 = _dot(h, w1_ref[...])
    tab = tab_ref[...]
    cos128, sin128 = tab[:, 0:LANES], tab[:, LANES:2 * LANES]
    cos64, sin64 = tab[:, 2 * LANES:3 * LANES], tab[:, 3 * LANES:4 * LANES]
    lane = lax.broadcasted_iota(jnp.int32, (tm, LANES), 1)
    first_half = (lane % IDX_DIM) < (IDX_DIM // 2)

    off = 0
    for hh in range(N_HEADS):
        sl = z1[:, off:off + HEAD_DIM]
        q_ref[0, :, hh * HEAD_DIM:(hh + 1) * HEAD_DIM] = (_rope128(sl, cos128, sin128) * (QK_SCALE * LOG2E)).astype(BF16)
        off += HEAD_DIM
    for hh in range(N_KV_HEADS):
        kr = _rope128(z1[:, off:off + HEAD_DIM], cos128, sin128)
        k_ref[0, pl.ds(hh, tm, stride=N_KV_HEADS), :] = kr
        if mode == "prompt":
            kbf_ref[0, :, hh * HEAD_DIM:(hh + 1) * HEAD_DIM] = kr.astype(BF16)
        off += HEAD_DIM
    v = z1[:, off:off + N_KV_HEADS * HEAD_DIM]
    for hh in range(N_KV_HEADS):
        v_ref[0, pl.ds(hh, tm, stride=N_KV_HEADS), :] = v[:, hh * HEAD_DIM:(hh + 1) * HEAD_DIM]
    if mode == "prompt":
        for c in range(tm // KEY_STEP):
            vt_ref[0, c] = v[c * KEY_STEP:(c + 1) * KEY_STEP, :].T.astype(BF16)
    off += N_KV_HEADS * HEAD_DIM
    for hh in range(N_IDX_HEADS * IDX_DIM // LANES):
        sl = z1[:, off:off + LANES]
        qi_ref[0, :, hh * LANES:(hh + 1) * LANES] = _rope64(sl, cos64, sin64, first_half).astype(BF16)
        off += LANES
    kiw = z1[:, off:off + LANES]
    kir = _rope64(kiw, cos64, sin64, first_half)[:, 0:IDX_DIM]
    ki_ref[0] = kir
    kibf_ref[0] = kir.astype(BF16)
    wit_ref[0] = kiw.T[IDX_DIM:IDX_DIM + N_IDX_HEADS, :] * IDX_SCALE

    z2 = _dot(h, w2_ref[...])
    bg = z2[:, 0:D_MODEL]
    u = z2[:, D_MODEL:2 * D_MODEL] * z2[:, 2 * D_MODEL:3 * D_MODEL]
    ga = z2[:, 3 * D_MODEL:4 * D_MODEL]
    gb = z2[:, 4 * D_MODEL:5 * D_MODEL]

    row = lax.broadcasted_iota(jnp.int32, (tm, D_CONV), 0)
    r1 = pltpu.roll(u, 1, axis=0)
    r2 = pltpu.roll(u, 2, axis=0)
    if mode == "prompt":
        t = pl.program_id(1)

        @pl.when(t == 0)
        def _():
            carry_ref[...] = jnp.zeros_like(carry_ref)

        c0 = carry_ref[0:1, :]
        c1 = carry_ref[1:2, :]
        um1 = jnp.where(row == 0, c1, r1)
        um2 = jnp.where(row == 0, c0, jnp.where(row == 1, c1, r2))
        carry_ref[0:2, :] = u[tm - 2:tm, :]
        cnew_ref[0] = u[tm - 2:tm, :]
    else:
        prev = prev_ref[0]
        seq_row = row % SUBLANES
        um1 = jnp.where(seq_row == 0, pltpu.roll(prev, tm - 1, axis=0), r1)
        um2 = jnp.where(seq_row < 2, prev, r2)
        u_ref[0] = u
    cw = cw_ref[...]
    conv = cw[0:1, :] * um2 + cw[1:2, :] * um1 + cw[2:3, :] * u
    ga_ref[0] = jax.nn.sigmoid(ga)
    cm_ref[0] = jax.nn.sigmoid(gb) * (bg * conv)


def _const_spec(shape):
    nd = len(shape)
    return pl.BlockSpec(shape, lambda *_: (0,) * nd, pipeline_mode=pl.Buffered(1))


def _inproj(x, norm_g, w1, w2, conv_w, tab, prev, *, mode, tm):
    B, T, _ = x.shape
    nt = T // tm
    tok = lambda w: pl.BlockSpec((1, tm, w), lambda b, t: (b, t, 0))
    in_specs = [tok(D_MODEL), _const_spec((1, D_MODEL)), _const_spec((D_MODEL, W1_COLS)),
                _const_spec((D_MODEL, W2_COLS)), _const_spec((CONV_WIDTH, D_CONV)),
                pl.BlockSpec((tm, 4 * LANES), lambda b, t: (t, 0))]
    args = [x, norm_g.reshape(1, D_MODEL), w1, w2, conv_w, tab]
    kvw = N_KV_HEADS * HEAD_DIM
    qiw = N_IDX_HEADS * IDX_DIM
    wit_spec = pl.BlockSpec((1, N_IDX_HEADS, tm), lambda b, t: (b, 0, t))
    kv_spec = pl.BlockSpec((1, tm * N_KV_HEADS, HEAD_DIM), lambda b, t: (b, t, 0))
    if mode == "prompt":
        out_shape = [
            jax.ShapeDtypeStruct((B, T, D_MODEL), BF16),
            jax.ShapeDtypeStruct((B, T * N_KV_HEADS, HEAD_DIM), F32),
            jax.ShapeDtypeStruct((B, T * N_KV_HEADS, HEAD_DIM), F32),
            jax.ShapeDtypeStruct((B, T, kvw), BF16),
            jax.ShapeDtypeStruct((B, T // KEY_STEP, kvw, KEY_STEP), BF16),
            jax.ShapeDtypeStruct((B, T, qiw), BF16),
            jax.ShapeDtypeStruct((B, T, IDX_DIM), F32),
            jax.ShapeDtypeStruct((B, T, IDX_DIM), BF16),
            jax.ShapeDtypeStruct((B, N_IDX_HEADS, T), F32),
            jax.ShapeDtypeStruct((B, T, D_MODEL), F32),
            jax.ShapeDtypeStruct((B, T, D_MODEL), F32),
            jax.ShapeDtypeStruct((B, CONV_WIDTH - 1, D_CONV), F32),
        ]
        out_specs = [tok(D_MODEL), kv_spec, kv_spec, tok(kvw),
                     pl.BlockSpec((1, tm // KEY_STEP, kvw, KEY_STEP), lambda b, t: (b, t, 0, 0)),
                     tok(qiw), tok(IDX_DIM), tok(IDX_DIM), wit_spec, tok(D_MODEL), tok(D_MODEL),
                     pl.BlockSpec((1, CONV_WIDTH - 1, D_CONV), lambda b, t: (b, 0, 0))]
        scratch = [pltpu.VMEM((SUBLANES, D_CONV), F32)]
    else:
        in_specs.append(tok(D_CONV))
        args.append(prev)
        out_shape = [
            jax.ShapeDtypeStruct((B, T, D_MODEL), BF16),
            jax.ShapeDtypeStruct((B, T * N_KV_HEADS, HEAD_DIM), F32),
            jax.ShapeDtypeStruct((B, T * N_KV_HEADS, HEAD_DIM), F32),
            jax.ShapeDtypeStruct((B, T, qiw), BF16),
            jax.ShapeDtypeStruct((B, T, IDX_DIM), F32),
            jax.ShapeDtypeStruct((B, T, IDX_DIM), BF16),
            jax.ShapeDtypeStruct((B, N_IDX_HEADS, T), F32),
            jax.ShapeDtypeStruct((B, T, D_MODEL), F32),
            jax.ShapeDtypeStruct((B, T, D_MODEL), F32),
            jax.ShapeDtypeStruct((B, T, D_CONV), F32),
        ]
        out_specs = [tok(D_MODEL), kv_spec, kv_spec, tok(qiw), tok(IDX_DIM), tok(IDX_DIM), wit_spec,
                     tok(D_MODEL), tok(D_MODEL), tok(D_CONV)]
        scratch = []
    return pl.pallas_call(
        functools.partial(_inproj_kernel, mode=mode, tm=tm),
        grid=(B, nt),
        in_specs=in_specs,
        out_specs=out_specs,
        out_shape=out_shape,
        scratch_shapes=scratch,
        compiler_params=pltpu.CompilerParams(dimension_semantics=("arbitrary", "arbitrary"),
                                             vmem_limit_bytes=VMEM_LIMIT),
        name="inproj_" + mode,
    )(*args)


def _sort_key(x):
    bits = lax.bitcast_convert_type(x, jnp.int32)
    return bits ^ ((bits >> 31) & 0x7FFFFFFF)


def _search_keys(count_ge, lo0, hi0, kk, n_lo0):
    def body(_, c):
        lo, hi, n_lo, n_hi, done = c
        mid = (lo & hi) + ((lo ^ hi) >> 1)
        stuck = mid == lo
        n_mid = count_ge(mid)
        upd = (done < 0.5) & jnp.logical_not(stuck)
        go_lo = n_mid >= kk
        up_lo = upd & go_lo
        lo = jnp.where(up_lo, mid, lo)
        n_lo = jnp.where(up_lo, n_mid, n_lo)
        up_hi = upd & jnp.logical_not(go_lo)
        hi = jnp.where(up_hi, mid, hi)
        n_hi = jnp.where(up_hi, n_mid, n_hi)
        done = jnp.where(stuck | (upd & (n_mid == kk)), 1.0, done)
        return lo, hi, n_lo, n_hi, done

    done0 = jnp.where(n_lo0 <= kk, 1.0, 0.0)
    lo, hi, n_lo, n_hi, _ = lax.fori_loop(0, KEY_BITS, body, (lo0, hi0, n_lo0, jnp.zeros_like(kk), done0))
    return lo, hi, n_lo, n_hi


def _fold_rows(x, op):
    return op(x.reshape(x.shape[0] // SUBLANES, SUBLANES, x.shape[1]), axis=0)


def _prompt_attn_kernel(qi_ref, wit_ref, q_ref, ki_ref, k_ref, vt_ref, o_ref,
                        key_scr, b_scr, d_a, d_b, s_a, s_b, p_a, p_b, acc_scr, *, n_sel, count_step):
    j = pl.program_id(1)
    per_step = KEY_STEP // Q_BLOCK
    n_steps = (j + per_step) // per_step
    n_pairs = (n_steps + 1) // 2
    last = n_steps - 1
    per_count = count_step // KEY_STEP
    n_count = (n_steps + per_count - 1) // per_count
    qi = qi_ref[0]
    q_stack = jnp.concatenate([qi[:, h * IDX_DIM:(h + 1) * IDX_DIM] for h in range(N_IDX_HEADS)], axis=0)
    wit = wit_ref[0]
    q = q_ref[0]
    q_grp = [jnp.concatenate([q[:, (kv * GROUP + g) * HEAD_DIM:(kv * GROUP + g + 1) * HEAD_DIM]
                              for g in range(GROUP)], axis=0) for kv in range(N_KV_HEADS)]
    key_pos = lax.broadcasted_iota(jnp.int32, (KEY_STEP, Q_BLOCK), 0)
    q_pos = lax.broadcasted_iota(jnp.int32, (KEY_STEP, Q_BLOCK), 1) + j * Q_BLOCK

    def step(i):
        return pl.ds(pl.multiple_of(i * KEY_STEP, KEY_STEP), KEY_STEP)

    def idx_dots(i):
        return _dot_nt(ki_ref[0, step(i), :], q_stack)

    def score_trip(i, d_cur, d_nxt, carry):
        mn, mx = carry
        ic = jnp.minimum(i, last)
        d_nxt[...] = idx_dots(jnp.minimum(i + 1, last))
        acc = wit[0:1, :] * jnp.maximum(d_cur[:, 0:Q_BLOCK], 0.0)
        for h in range(1, N_IDX_HEADS):
            acc = acc + wit[h:h + 1, :] * jnp.maximum(d_cur[:, h * Q_BLOCK:(h + 1) * Q_BLOCK], 0.0)
        key = _sort_key(acc)
        masked = jnp.where(key_pos + ic * KEY_STEP <= q_pos, key, INT_MIN)
        key_scr[step(ic), :] = masked
        return jnp.minimum(mn, _fold_rows(key, jnp.min)), jnp.maximum(mx, _fold_rows(masked, jnp.max))

    d_a[...] = idx_dots(0)

    def score_pair(pi, carry):
        carry = score_trip(2 * pi, d_a, d_b, carry)
        return score_trip(2 * pi + 1, d_b, d_a, carry)

    mn, mx = lax.fori_loop(0, n_pairs, score_pair, (jnp.full((SUBLANES, Q_BLOCK), INT_MAX, jnp.int32),
                                                    jnp.full((SUBLANES, Q_BLOCK), INT_MIN, jnp.int32)))
    lo0 = jnp.min(mn, axis=0, keepdims=True)
    hi0 = jnp.max(mx, axis=0, keepdims=True) + 1

    def fill(i, c):
        key_scr[step(i), :] = jnp.full((KEY_STEP, Q_BLOCK), INT_MIN, jnp.int32)
        return c

    lax.fori_loop(n_steps, n_count * per_count, fill, 0)

    def count_ge(t):
        tb = jnp.broadcast_to(t, (SUBLANES, Q_BLOCK))

        def body(i, accs):
            accs = list(accs)
            base = pl.multiple_of(i * count_step, count_step)
            for r in range(count_step // SUBLANES):
                keys = key_scr[pl.ds(base + r * SUBLANES, SUBLANES), :]
                accs[r % COUNT_LANES] = accs[r % COUNT_LANES] + jnp.where(keys >= tb, 1.0, 0.0)
            return tuple(accs)

        accs = lax.fori_loop(0, n_count, body, tuple(jnp.zeros((SUBLANES, Q_BLOCK), F32) for _ in range(COUNT_LANES)))
        return jnp.sum(functools.reduce(lambda a, b: a + b, accs), axis=0, keepdims=True)

    n_valid = (j * Q_BLOCK + 1 + lax.broadcasted_iota(jnp.int32, (1, Q_BLOCK), 1)).astype(F32)
    kk = jnp.minimum(n_valid, float(n_sel))
    lo, hi, n_lo, n_hi = lax.cond((j + 1) * Q_BLOCK <= n_sel,
                                  lambda: (lo0, hi0, n_valid, jnp.zeros_like(kk)),
                                  lambda: _search_keys(count_ge, lo0, hi0, kk, n_valid))
    need = kk - n_hi

    def mask_plain(i, c):
        b_scr[step(i), :] = jnp.where(key_scr[step(i), :] >= lo, 0.0, NEG_INF)
        return c

    def mask_ties(i, run):
        tri = jnp.where(lax.broadcasted_iota(jnp.int32, (KEY_STEP, KEY_STEP), 1)
                        <= lax.broadcasted_iota(jnp.int32, (KEY_STEP, KEY_STEP), 0), 1.0, 0.0).astype(BF16)
        key = key_scr[step(i), :]
        above = key >= hi
        elig = (key >= lo) & jnp.logical_not(above)
        rank = run + _dot(tri, jnp.where(elig, 1.0, 0.0).astype(BF16))
        b_scr[step(i), :] = jnp.where(above | (elig & (rank <= need)), 0.0, NEG_INF)
        return rank[KEY_STEP - 1:KEY_STEP, :]

    any_tie = jnp.max(jnp.where(n_lo > kk, 1.0, 0.0)) > 0.5

    @pl.when(any_tie)
    def _():
        lax.fori_loop(0, n_steps, mask_ties, jnp.zeros((1, Q_BLOCK), F32))

    @pl.when(jnp.logical_not(any_tie))
    def _():
        lax.fori_loop(0, n_steps, mask_plain, 0)

    def logits(i, kv):
        return _dot_nt(k_ref[0, step(i), kv * HEAD_DIM:(kv + 1) * HEAD_DIM], q_grp[kv])

    def attn_trip(i, s_cur, s_nxt, p_cur, p_prv, carry):
        m, l = carry
        ic = jnp.minimum(i, last)
        pv = [_dot(vt_ref[0, jnp.clip(i - 1, 0, last), kv * HEAD_DIM:(kv + 1) * HEAD_DIM, :], p_prv[kv])
              for kv in range(N_KV_HEADS)]
        for kv in range(N_KV_HEADS):
            s_nxt[kv] = logits(jnp.minimum(i + 1, last), kv)
        b = b_scr[step(ic), :] + jnp.where(i <= last, 0.0, NEG_INF)
        m_rows, l_new = [], []
        for kv in range(N_KV_HEADS):
            ps, alphas = [], []
            for g in range(GROUP):
                h = kv * GROUP + g
                sg = s_cur[kv, :, g * Q_BLOCK:(g + 1) * Q_BLOCK] + b
                m_old = m[h:h + 1, :]
                m_new = jnp.maximum(m_old, jnp.max(_fold_rows(sg, jnp.max), axis=0, keepdims=True))
                alpha = jnp.exp2(m_old - m_new)
                p = jnp.exp2(sg - m_new)
                l_new.append(alpha * l[h] + _fold_rows(p, jnp.sum))
                ps.append(p.astype(BF16))
                alphas.append(alpha)
                m_rows.append(m_new)
            p_cur[kv] = jnp.concatenate(ps, axis=1)
            acc_scr[kv] = (acc_scr[kv] + pv[kv]) * jnp.concatenate(alphas, axis=1)
        return jnp.concatenate(m_rows, axis=0), tuple(l_new)

    acc_scr[...] = jnp.zeros_like(acc_scr)
    p_b[...] = jnp.zeros_like(p_b)
    for kv in range(N_KV_HEADS):
        s_a[kv] = logits(0, kv)

    def attn_pair(pi, carry):
        carry = attn_trip(2 * pi, s_a, s_b, p_a, p_b, carry)
        return attn_trip(2 * pi + 1, s_b, s_a, p_b, p_a, carry)

    m, l = lax.fori_loop(0, n_pairs, attn_pair, (jnp.full((N_HEADS, Q_BLOCK), M_FLOOR, F32),
                                                 tuple(jnp.zeros((SUBLANES, Q_BLOCK), F32) for _ in range(N_HEADS))))
    i_end = jnp.minimum(2 * n_pairs - 1, last)
    for h in range(N_HEADS):
        kv, g = divmod(h, GROUP)
        if g == 0:
            acc_kv = acc_scr[kv] + _dot(vt_ref[0, i_end, kv * HEAD_DIM:(kv + 1) * HEAD_DIM, :], p_b[kv])
        inv_l = 1.0 / jnp.sum(l[h], axis=0, keepdims=True)
        o_ref[0, :, h * HEAD_DIM:(h + 1) * HEAD_DIM] = (acc_kv[:, g * Q_BLOCK:(g + 1) * Q_BLOCK] * inv_l).T


def _prompt_attention(qi, wit, q, ki_bf, k_bf, vt):
    B, T, _ = q.shape
    nq = T // Q_BLOCK
    n_sel = min(TOPK_MAX, T // 4)
    kvw = N_KV_HEADS * HEAD_DIM
    count_step = min(COUNT_STEP, T)
    gq = GROUP * Q_BLOCK
    return pl.pallas_call(
        functools.partial(_prompt_attn_kernel, n_sel=n_sel, count_step=count_step),
        grid=(B, nq),
        in_specs=[pl.BlockSpec((1, Q_BLOCK, N_IDX_HEADS * IDX_DIM), lambda b, j: (b, j, 0)),
                  pl.BlockSpec((1, N_IDX_HEADS, Q_BLOCK), lambda b, j: (b, 0, j)),
                  pl.BlockSpec((1, Q_BLOCK, D_MODEL), lambda b, j: (b, j, 0)),
                  pl.BlockSpec((1, T, IDX_DIM), lambda b, j: (b, 0, 0)),
                  pl.BlockSpec((1, T, kvw), lambda b, j: (b, 0, 0)),
                  pl.BlockSpec((1, T // KEY_STEP, kvw, KEY_STEP), lambda b, j: (b, 0, 0, 0))],
        out_specs=pl.BlockSpec((1, Q_BLOCK, D_MODEL), lambda b, j: (b, j, 0)),
        out_shape=jax.ShapeDtypeStruct((B, T, D_MODEL), F32),
        scratch_shapes=[pltpu.VMEM((T, Q_BLOCK), jnp.int32), pltpu.VMEM((T, Q_BLOCK), F32),
                        pltpu.VMEM((KEY_STEP, N_IDX_HEADS * Q_BLOCK), F32),
                        pltpu.VMEM((KEY_STEP, N_IDX_HEADS * Q_BLOCK), F32),
                        pltpu.VMEM((N_KV_HEADS, KEY_STEP, gq), F32), pltpu.VMEM((N_KV_HEADS, KEY_STEP, gq), F32),
                        pltpu.VMEM((N_KV_HEADS, KEY_STEP, gq), BF16), pltpu.VMEM((N_KV_HEADS, KEY_STEP, gq), BF16),
                        pltpu.VMEM((N_KV_HEADS, HEAD_DIM, gq), F32)],
        compiler_params=pltpu.CompilerParams(dimension_semantics=("arbitrary", "arbitrary"),
                                             vmem_limit_bytes=VMEM_LIMIT),
        name="prompt_attention",
    )(qi, wit, q, ki_bf, k_bf, vt)


def _sample_score_kernel(pt_ref, qi_ref, wcol_ref, kin_ref, *rest, n_pages, ts):
    page_refs, s_ref = rest[:n_pages], rest[n_pages]
    qi = qi_ref[0].astype(F32)
    q_all = jnp.concatenate([qi[:, h * IDX_DIM:(h + 1) * IDX_DIM] for h in range(N_IDX_HEADS)], axis=0).astype(BF16)
    wcol = wcol_ref[0]

    def score(dots):
        r = jnp.maximum(dots, 0.0) * wcol
        acc = r[0:ts]
        for h in range(1, N_IDX_HEADS):
            acc = acc + r[h * ts:(h + 1) * ts]
        return acc

    past = jnp.concatenate([page_refs[p][0] for p in range(n_pages)], axis=1).astype(BF16)
    s_ref[0, :, 0:n_pages * PAGE_SIZE] = score(_dot(q_all, past))
    new_keys = jnp.concatenate([kin_ref[0], jnp.zeros((PAGE_SIZE - ts, IDX_DIM), F32)], axis=0).astype(BF16)
    lane = lax.broadcasted_iota(jnp.int32, (ts, PAGE_SIZE), 1)
    row = lax.broadcasted_iota(jnp.int32, (ts, PAGE_SIZE), 0)
    s_ref[0, :, n_pages * PAGE_SIZE:(n_pages + 1) * PAGE_SIZE] = jnp.where(
        lane <= row, score(_dot_nt(q_all, new_keys)), NEG_INF)


def _sample_scores(page_table, qi, wcol, ki_bf, cache_kidx):
    nb, ts, _ = qi.shape
    n_pages = page_table.shape[1]

    def page_spec(p):
        return pl.BlockSpec((1, IDX_DIM, PAGE_SIZE), lambda b, pt: (pt[b, p], 0, 0))

    grid_spec = pltpu.PrefetchScalarGridSpec(
        num_scalar_prefetch=1,
        grid=(nb,),
        in_specs=[pl.BlockSpec((1, ts, N_IDX_HEADS * IDX_DIM), lambda b, pt: (b, 0, 0)),
                  pl.BlockSpec((1, N_IDX_HEADS * ts, 1), lambda b, pt: (b, 0, 0)),
                  pl.BlockSpec((1, ts, IDX_DIM), lambda b, pt: (b, 0, 0))] + [page_spec(p) for p in range(n_pages)],
        out_specs=pl.BlockSpec((1, ts, (n_pages + 1) * PAGE_SIZE), lambda b, pt: (b, 0, 0)),
    )
    return pl.pallas_call(
        functools.partial(_sample_score_kernel, n_pages=n_pages, ts=ts),
        grid_spec=grid_spec,
        out_shape=jax.ShapeDtypeStruct((nb, ts, (n_pages + 1) * PAGE_SIZE), F32),
        compiler_params=pltpu.CompilerParams(dimension_semantics=("arbitrary",), vmem_limit_bytes=VMEM_LIMIT),
        name="sample_scores",
    )(page_table, qi, wcol, ki_bf, *([cache_kidx] * n_pages))


def _sample_select_kernel(s_ref, b_ref, key_scr, *, n_sel, n_chunks):
    rows = s_ref.shape[0]
    tri = jnp.where(lax.broadcasted_iota(jnp.int32, (LANES, LANES), 0)
                    <= lax.broadcasted_iota(jnp.int32, (LANES, LANES), 1), 1.0, 0.0).astype(BF16)

    def cols(c):
        return slice(c * LANES, (c + 1) * LANES)

    mn = jnp.full((rows, LANES), INT_MAX, jnp.int32)
    mx = jnp.full((rows, LANES), INT_MIN, jnp.int32)
    for c in range(n_chunks):
        s = s_ref[:, cols(c)]
        valid = s > NEG_INF
        key = _sort_key(s)
        key_scr[:, cols(c)] = jnp.where(valid, key, INT_MIN)
        mn = jnp.minimum(mn, jnp.where(valid, key, INT_MAX))
        mx = jnp.maximum(mx, jnp.where(valid, key, INT_MIN))
    lo0 = jnp.min(mn, axis=1, keepdims=True)
    hi0 = jnp.max(mx, axis=1, keepdims=True) + 1

    def count_ge(t):
        acc = jnp.where(key_scr[:, cols(0)] >= t, 1.0, 0.0)
        for c in range(1, n_chunks):
            acc = acc + jnp.where(key_scr[:, cols(c)] >= t, 1.0, 0.0)
        return jnp.sum(acc, axis=1, keepdims=True)

    kk = jnp.full((rows, 1), float(n_sel), F32)
    lo, hi, _, n_hi = _search_keys(count_ge, lo0, hi0, kk, count_ge(lo0))
    need = kk - n_hi
    run = jnp.zeros((rows, 1), F32)
    for c in range(n_chunks):
        key = key_scr[:, cols(c)]
        above = key >= hi
        elig = (key >= lo) & jnp.logical_not(above)
        rank = run + _dot(jnp.where(elig, 1.0, 0.0).astype(BF16), tri)
        sel = above | (elig & (rank <= need))
        b_ref[:, cols(c)] = jnp.where(sel, 0.0, NEG_INF)
        run = rank[:, LANES - 1:LANES]


def _sample_select(scores, n_sel, rows_per_step):
    rows, width = scores.shape
    return pl.pallas_call(
        functools.partial(_sample_select_kernel, n_sel=n_sel, n_chunks=width // LANES),
        grid=(rows // rows_per_step,),
        in_specs=[pl.BlockSpec((rows_per_step, width), lambda i: (i, 0))],
        out_specs=pl.BlockSpec((rows_per_step, width), lambda i: (i, 0)),
        out_shape=jax.ShapeDtypeStruct((rows, width), F32),
        scratch_shapes=[pltpu.VMEM((rows_per_step, width), jnp.int32)],
        compiler_params=pltpu.CompilerParams(dimension_semantics=("arbitrary",), vmem_limit_bytes=VMEM_LIMIT),
        name="sample_select",
    )(scores)


def _sample_attn_kernel(pt_ref, q_ref, b_ref, kn_ref, vn_ref, *rest, n_pages, ts):
    k_refs, v_refs, o_ref = rest[:n_pages], rest[n_pages:2 * n_pages], rest[2 * n_pages]
    q = q_ref[0].astype(F32)
    bias = jnp.concatenate([b_ref[0]] * GROUP, axis=0)
    pad = jnp.zeros((PAGE_SIZE - ts, HEAD_DIM), F32)
    for kv in range(N_KV_HEADS):
        qg = jnp.concatenate([q[:, (kv * GROUP + g) * HEAD_DIM:(kv * GROUP + g + 1) * HEAD_DIM]
                              for g in range(GROUP)], axis=0).astype(BF16)

        def head_rows(refs, new_ref, kv=kv):
            tiles = [r[0, pl.ds(kv, PAGE_SIZE, stride=N_KV_HEADS), :] for r in refs]
            tiles += [new_ref[0, pl.ds(kv, ts, stride=N_KV_HEADS), :], pad]
            return jnp.concatenate(tiles, axis=0).astype(BF16)

        s = _dot_nt(qg, head_rows(k_refs, kn_ref)) + bias
        p_ = jnp.exp2(s - jnp.max(s, axis=1, keepdims=True))
        o = _dot(p_.astype(BF16), head_rows(v_refs, vn_ref)) / jnp.sum(p_, axis=1, keepdims=True)
        for g in range(GROUP):
            hh = kv * GROUP + g
            o_ref[0, :, hh * HEAD_DIM:(hh + 1) * HEAD_DIM] = o[g * ts:(g + 1) * ts, :]


def _sample_attention(page_table, q, bias, k_new, v_new, cache_k, cache_v):
    nb, ts, _ = q.shape
    n_pages = page_table.shape[1]
    kvw = N_KV_HEADS * HEAD_DIM

    def page_spec(p):
        return pl.BlockSpec((1, PAGE_SIZE * N_KV_HEADS, HEAD_DIM), lambda b, pt: (pt[b, p], 0, 0))

    grid_spec = pltpu.PrefetchScalarGridSpec(
        num_scalar_prefetch=1,
        grid=(nb,),
        in_specs=[pl.BlockSpec((1, ts, D_MODEL), lambda b, pt: (b, 0, 0)),
                  pl.BlockSpec((1, ts, (n_pages + 1) * PAGE_SIZE), lambda b, pt: (b, 0, 0)),
                  pl.BlockSpec((1, ts * N_KV_HEADS, HEAD_DIM), lambda b, pt: (b, 0, 0)),
                  pl.BlockSpec((1, ts * N_KV_HEADS, HEAD_DIM), lambda b, pt: (b, 0, 0))]
                 + [page_spec(p) for p in range(n_pages)] * 2,
        out_specs=pl.BlockSpec((1, ts, D_MODEL), lambda b, pt: (b, 0, 0)),
    )
    return pl.pallas_call(
        functools.partial(_sample_attn_kernel, n_pages=n_pages, ts=ts),
        grid_spec=grid_spec,
        out_shape=jax.ShapeDtypeStruct((nb, ts, D_MODEL), F32),
        compiler_params=pltpu.CompilerParams(dimension_semantics=("arbitrary",), vmem_limit_bytes=VMEM_LIMIT),
        name="sample_attention",
    )(page_table, q, bias, k_new, v_new, *([cache_k] * n_pages), *([cache_v] * n_pages))


def _ffn_kernel(x_ref, oa_ref, ga_ref, cm_ref, p_ref, wout_ref, gffn_ref, wg_ref, wu_ref, wd_ref,
                gple_ref, wple_ref, wpg_ref, gfin_ref, y_ref, x1_scr, h_scr, acc_scr):
    f = pl.program_id(1)

    @pl.when(f == 0)
    def _():
        merged = ga_ref[...] * oa_ref[...] + cm_ref[...]
        x1 = x_ref[...] + _dot(merged.astype(BF16), wout_ref[...])
        x1_scr[...] = x1
        h_scr[...] = _rmsnorm(x1, gffn_ref[...]).astype(BF16)
        acc_scr[...] = jnp.zeros_like(acc_scr)

    h = h_scr[...]
    g = _dot(h, wg_ref[...])
    u = _dot(h, wu_ref[...])
    acc_scr[...] += _dot((g * jax.nn.sigmoid(g) * u).astype(BF16), wd_ref[...])

    @pl.when(f == pl.num_programs(1) - 1)
    def _():
        x2 = x1_scr[...] + acc_scr[...]
        gate = jax.nn.sigmoid(_dot(_rmsnorm(x2, gple_ref[...]).astype(BF16), wpg_ref[...]))
        x3 = x2 + _dot(p_ref[...].astype(BF16), wple_ref[...]) * gate
        y_ref[...] = _rmsnorm(x3, gfin_ref[...])


def _ffn(x, oa, ga, cm, p, wout, gffn, wg, wu, wd, gple, wple, wpg, gfin, *, tm, tf):
    n = x.shape[0]
    tm = min(tm, n)
    tok = lambda w: pl.BlockSpec((tm, w), lambda i, f: (i, 0))
    vec = lambda a: a.reshape(1, D_MODEL)
    return pl.pallas_call(
        _ffn_kernel,
        grid=(n // tm, D_FF // tf),
        in_specs=[tok(D_MODEL), tok(D_MODEL), tok(D_MODEL), tok(D_MODEL), tok(D_PLE),
                  _const_spec((D_MODEL, D_MODEL)), _const_spec((1, D_MODEL)),
                  pl.BlockSpec((D_MODEL, tf), lambda i, f: (0, f)),
                  pl.BlockSpec((D_MODEL, tf), lambda i, f: (0, f)),
                  pl.BlockSpec((tf, D_MODEL), lambda i, f: (f, 0)),
                  _const_spec((1, D_MODEL)), _const_spec((D_PLE, D_MODEL)), _const_spec((D_MODEL, D_MODEL)),
                  _const_spec((1, D_MODEL))],
        out_specs=tok(D_MODEL),
        out_shape=jax.ShapeDtypeStruct((n, D_MODEL), F32),
        scratch_shapes=[pltpu.VMEM((tm, D_MODEL), F32), pltpu.VMEM((tm, D_MODEL), BF16),
                        pltpu.VMEM((tm, D_MODEL), F32)],
        compiler_params=pltpu.CompilerParams(dimension_semantics=("arbitrary", "arbitrary"),
                                             vmem_limit_bytes=VMEM_LIMIT),
        name="ffn",
    )(x, oa, ga, cm, p, wout, vec(gffn), wg, wu, wd, vec(gple), wple, wpg, vec(gfin))


def _split_w_in(w_in):
    n_att = N_HEADS * HEAD_DIM + 2 * N_KV_HEADS * HEAD_DIM + N_IDX_HEADS * IDX_DIM + IDX_DIM + N_IDX_HEADS
    pad = jnp.zeros((D_MODEL, W1_COLS - n_att), w_in.dtype)
    w1 = jnp.concatenate([w_in[:, :n_att], pad], axis=1).astype(BF16)
    w2 = w_in[:, n_att:].astype(BF16)
    return w1, w2


def kernel(x_prompt, x_sample, cache_k, cache_v, cache_kidx, state_conv, page_table, p_prompt, p_sample, norm_mix, w_in, conv_w, w_out, norm_ffn, w_gate_up, w_down, norm_ple, w_ple, w_ple_gate, norm_final):
    Bp, Tp, _ = x_prompt.shape
    Bs, Ts, _ = x_sample.shape
    n_pages = page_table.shape[1]
    past_len = n_pages * PAGE_SIZE
    n_phys = cache_k.shape[1]
    kvw = N_KV_HEADS * HEAD_DIM
    l = 0

    w1, w2 = _split_w_in(w_in[l])
    wout = w_out[l].astype(BF16)
    wg = w_gate_up[l][:, :D_FF].astype(BF16)
    wu = w_gate_up[l][:, D_FF:].astype(BF16)
    wd = w_down[l].astype(BF16)
    wple = w_ple[l].astype(BF16)
    wpg = w_ple_gate[l].astype(BF16)

    tab_p = _rope_table(jnp.arange(Tp))
    (q_p, k_p, v_p, kbf_p, vt_p, qi_p, ki_p, kibf_p, wit_p, ga_p, cm_p, cnew_p) = _inproj(
        x_prompt, norm_mix[l], w1, w2, conv_w[l], tab_p, None, mode="prompt", tm=256)
    oa_p = _prompt_attention(qi_p, wit_p, q_p, kibf_p, kbf_p, vt_p)
    n_p = Bp * Tp
    y_p = _ffn(x_prompt.reshape(n_p, D_MODEL), oa_p.reshape(n_p, D_MODEL), ga_p.reshape(n_p, D_MODEL),
               cm_p.reshape(n_p, D_MODEL), p_prompt[l].reshape(n_p, D_PLE), wout, norm_ffn[l], wg, wu, wd,
               norm_ple[l], wple, wpg, norm_final, tm=512, tf=D_FF // 2)

    n_s = Bs * Ts
    tm_s = 256
    tab_s = _rope_table(past_len + (jnp.arange(tm_s) % Ts))
    prev = jnp.concatenate([state_conv[l], jnp.zeros((Bs, Ts - (CONV_WIDTH - 1), D_CONV), F32)], axis=1)
    (q_s, k_s, v_s, qi_s, ki_s, kibf_s, wit_s, ga_s, cm_s, u_s) = _inproj(
        x_sample.reshape(n_s // tm_s, tm_s, D_MODEL), norm_mix[l], w1, w2, conv_w[l], tab_s,
        prev.reshape(n_s // tm_s, tm_s, D_CONV), mode="sample", tm=tm_s)
    wcol = wit_s.transpose(1, 0, 2).reshape(N_IDX_HEADS, Bs, Ts).transpose(1, 0, 2).reshape(Bs, N_IDX_HEADS * Ts, 1)
    scores = _sample_scores(page_table, qi_s.reshape(Bs, Ts, -1), wcol, ki_s.reshape(Bs, Ts, IDX_DIM),
                            jnp.swapaxes(cache_kidx[l], 1, 2))
    n_sel = min(TOPK_MAX, (past_len + Ts) // 4)
    bias = _sample_select(scores.reshape(n_s, -1), n_sel, 256)
    oa_s = _sample_attention(page_table, q_s.reshape(Bs, Ts, D_MODEL), bias.reshape(Bs, Ts, -1),
                             k_s.reshape(Bs, Ts * N_KV_HEADS, HEAD_DIM), v_s.reshape(Bs, Ts * N_KV_HEADS, HEAD_DIM),
                             cache_k[l].reshape(n_phys, PAGE_SIZE * N_KV_HEADS, HEAD_DIM),
                             cache_v[l].reshape(n_phys, PAGE_SIZE * N_KV_HEADS, HEAD_DIM))
    y_s = _ffn(x_sample.reshape(n_s, D_MODEL), oa_s.reshape(n_s, D_MODEL), ga_s.reshape(n_s, D_MODEL),
               cm_s.reshape(n_s, D_MODEL), p_sample[l].reshape(n_s, D_PLE), wout, norm_ffn[l], wg, wu, wd,
               norm_ple[l], wple, wpg, norm_final, tm=512, tf=D_FF // 2)

    return (y_p.reshape(Bp, Tp, D_MODEL), y_s.reshape(Bs, Ts, D_MODEL),
            k_p.reshape(1, Bp, Tp, N_KV_HEADS, HEAD_DIM), v_p.reshape(1, Bp, Tp, N_KV_HEADS, HEAD_DIM),
            ki_p.reshape(1, Bp, Tp, IDX_DIM), cnew_p.reshape(1, Bp, CONV_WIDTH - 1, D_CONV),
            k_s.reshape(1, Bs, Ts, N_KV_HEADS, HEAD_DIM), v_s.reshape(1, Bs, Ts, N_KV_HEADS, HEAD_DIM),
            ki_s.reshape(1, Bs, Ts, IDX_DIM),
            u_s.reshape(Bs, Ts, D_CONV)[:, Ts - (CONV_WIDTH - 1):, :].reshape(1, Bs, CONV_WIDTH - 1, D_CONV))
```

```python
import functools

import numpy as np
import jax
import jax.numpy as jnp
from jax import lax
from jax.experimental import pallas as pl
from jax.experimental.pallas import tpu as pltpu

D_MODEL = 1024
N_HEADS = 8
N_KV_HEADS = 2
GROUP = N_HEADS // N_KV_HEADS
HEAD_DIM = 128
N_IDX_HEADS = 8
IDX_DIM = 64
IDX_SCALE = (N_IDX_HEADS * IDX_DIM) ** -0.5
QK_SCALE = HEAD_DIM ** -0.5
TOPK_MAX = 256
D_CONV = D_MODEL
CONV_WIDTH = 3
D_FF = 2816
D_PLE = 256
PAGE_SIZE = 128
ROPE_THETA = 10000.0
EPS = 1e-6

LANES = 128
SUBLANES = 8
KEY_STEP = 256
Q_BLOCK = 128
COUNT_STEP = 1024
COUNT_LANES = 8
VT_ROWS = HEAD_DIM + 16
CONV_SLAB = 256
INPROJ_ROWS = 512
SCORE_SEQS = 4
VMEM_LIMIT = 56 * 1024 * 1024

W1_COLS = N_HEADS * HEAD_DIM + 2 * N_KV_HEADS * HEAD_DIM + N_IDX_HEADS * IDX_DIM + LANES
W2_COLS = 5 * D_MODEL

F32 = jnp.float32
BF16 = jnp.bfloat16
NEG_INF = float("-inf")
INT_MIN = -2 ** 31
INT_MAX = 2 ** 31 - 1
KEY_BITS = 32
M_FLOOR = -1e30
LOG2E = 1.4426950408889634


def _dot(a, b):
    return jnp.dot(a, b, preferred_element_type=F32)


def _dot_nt(a, b):
    return lax.dot_general(a, b, (((1,), (1,)), ((), ())), preferred_element_type=F32)


def _rmsnorm(x, g):
    var = jnp.mean(x * x, axis=-1, keepdims=True)
    return (x * lax.rsqrt(var + EPS)) * g


def _rope_table(pos):
    def tab(half, reps):
        freqs = np.float32(ROPE_THETA) ** (-np.arange(half, dtype=np.float32) / np.float32(half))
        ang = pos.astype(np.float32)[:, None] * freqs[None, :].astype(np.float32)
        c, s = np.cos(ang).astype(np.float32), np.sin(ang).astype(np.float32)
        return np.tile(np.concatenate([c, c], 1), (1, reps)), np.tile(np.concatenate([-s, s], 1), (1, reps))
    c128, s128 = tab(HEAD_DIM // 2, 1)
    c64, s64 = tab(IDX_DIM // 2, LANES // IDX_DIM)
    return jnp.asarray(np.concatenate([c128, s128, c64, s64], axis=1))


def _rope128(x, cos, sin):
    return x * cos + pltpu.roll(x, HEAD_DIM // 2, axis=1) * sin


def _rope64(x, cos, sin, first_half):
    partner = jnp.where(first_half, pltpu.roll(x, LANES - IDX_DIM // 2, axis=1), pltpu.roll(x, IDX_DIM // 2, axis=1))
    return x * cos + partner * sin


def _inproj_kernel(*refs, mode, tm):
    if mode == "prompt":
        (x_ref, g_ref, w1_ref, w2_ref, cw_ref, tab_ref,
         q_ref, k_ref, v_ref, kbf_ref, vt_ref, qi_ref, ki_ref, kibf_ref, wit_ref, ga_ref, cm_ref, cnew_ref,
         carry_ref) = refs
    else:
        (x_ref, g_ref, w1_ref, w2_ref, cw_ref, tab_ref, prev_ref,
         q_ref, k_ref, v_ref, qi_ref, ki_ref, kibf_ref, wit_ref, ga_ref, cm_ref, u_ref) = refs

    x = x_ref[0]
    h = _rmsnorm(x, g_ref[...]).astype(BF16)
    z1 = _dot(h, w1_ref[...])
    tab = tab_ref[...]
    cos128, sin128 = tab[:, 0:LANES], tab[:, LANES:2 * LANES]
    cos64, sin64 = tab[:, 2 * LANES:3 * LANES], tab[:, 3 * LANES:4 * LANES]
    lane = lax.broadcasted_iota(jnp.int32, (tm, LANES), 1)
    first_half = (lane % IDX_DIM) < (IDX_DIM // 2)

    off = 0
    for hh in range(N_HEADS):
        sl = z1[:, off:off + HEAD_DIM]
        q_ref[0, :, hh * HEAD_DIM:(hh + 1) * HEAD_DIM] = (_rope128(sl, cos128, sin128) * (QK_SCALE * LOG2E)).astype(BF16)
        off += HEAD_DIM
    for hh in range(N_KV_HEADS):
        kr = _rope128(z1[:, off:off + HEAD_DIM], cos128, sin128)
        k_ref[0, pl.ds(hh, tm, stride=N_KV_HEADS), :] = kr
        if mode == "prompt":
            kbf_ref[0, :, hh * HEAD_DIM:(hh + 1) * HEAD_DIM] = kr.astype(BF16)
        off += HEAD_DIM
    v = z1[:, off:off + N_KV_HEADS * HEAD_DIM]
    for hh in range(N_KV_HEADS):
        v_ref[0, pl.ds(hh, tm, stride=N_KV_HEADS), :] = v[:, hh * HEAD_DIM:(hh + 1) * HEAD_DIM]
    if mode == "prompt":
        for c in range(tm // KEY_STEP):
            vt = v[c * KEY_STEP:(c + 1) * KEY_STEP, :].T
            ones_row = jnp.where(lax.broadcasted_iota(jnp.int32, (VT_ROWS - HEAD_DIM, KEY_STEP), 0) == 0, 1.0, 0.0)
            for hh in range(N_KV_HEADS):
                vt_ref[0, c, hh * VT_ROWS:(hh + 1) * VT_ROWS, :] = jnp.concatenate(
                    [vt[hh * HEAD_DIM:(hh + 1) * HEAD_DIM, :], ones_row], axis=0).astype(BF16)
    off += N_KV_HEADS * HEAD_DIM
    for hh in range(N_IDX_HEADS * IDX_DIM // LANES):
        sl = z1[:, off:off + LANES]
        qi_ref[0, :, hh * LANES:(hh + 1) * LANES] = _rope64(sl, cos64, sin64, first_half).astype(BF16)
        off += LANES
    kiw = z1[:, off:off + LANES]
    kir = _rope64(kiw, cos64, sin64, first_half)[:, 0:IDX_DIM]
    ki_ref[0] = kir
    kibf_ref[0] = kir.astype(BF16)
    wit_ref[0] = kiw.T[IDX_DIM:IDX_DIM + N_IDX_HEADS, :] * IDX_SCALE

    if mode == "prompt":
        @pl.when(pl.program_id(1) == 0)
        def _():
            carry_ref[...] = jnp.zeros_like(carry_ref)

    row = lax.broadcasted_iota(jnp.int32, (tm, CONV_SLAB), 0)
    for c in range(D_CONV // CONV_SLAB):
        cols = slice(c * CONV_SLAB, (c + 1) * CONV_SLAB)
        bg, cg, xc, ga, gb = (_dot(h, w2_ref[:, k * D_MODEL + c * CONV_SLAB:k * D_MODEL + (c + 1) * CONV_SLAB])
                              for k in range(5))
        u = cg * xc
        r1 = pltpu.roll(u, 1, axis=0)
        r2 = pltpu.roll(u, 2, axis=0)
        if mode == "prompt":
            c0 = carry_ref[0:1, cols]
            c1 = carry_ref[1:2, cols]
            um1 = jnp.where(row == 0, c1, r1)
            um2 = jnp.where(row == 0, c0, jnp.where(row == 1, c1, r2))
            carry_ref[0:2, cols] = u[tm - 2:tm, :]
            cnew_ref[0, :, cols] = u[tm - 2:tm, :]
        else:
            prev = prev_ref[0, :, cols]
            seq_row = row % SUBLANES
            um1 = jnp.where(seq_row == 0, pltpu.roll(prev, tm - 1, axis=0), r1)
            um2 = jnp.where(seq_row < 2, prev, r2)
            u_ref[0, :, cols] = u
        cw = cw_ref[:, cols]
        conv = cw[0:1, :] * um2 + cw[1:2, :] * um1 + cw[2:3, :] * u
        ga_ref[0, :, cols] = jax.nn.sigmoid(ga)
        cm_ref[0, :, cols] = jax.nn.sigmoid(gb) * (bg * conv)


def _const_spec(shape):
    nd = len(shape)
    return pl.BlockSpec(shape, lambda *_: (0,) * nd, pipeline_mode=pl.Buffered(1))


def _inproj(x, norm_g, w1, w2, conv_w, tab, prev, *, mode, tm):
    B, T, _ = x.shape
    nt = T // tm
    tok = lambda w: pl.BlockSpec((1, tm, w), lambda b, t: (b, t, 0))
    in_specs = [tok(D_MODEL), _const_spec((1, D_MODEL)), _const_spec((D_MODEL, W1_COLS)),
                _const_spec((D_MODEL, W2_COLS)), _const_spec((CONV_WIDTH, D_CONV)),
                pl.BlockSpec((tm, 4 * LANES), lambda b, t: (t, 0))]
    args = [x, norm_g.reshape(1, D_MODEL), w1, w2, conv_w, tab]
    kvw = N_KV_HEADS * HEAD_DIM
    qiw = N_IDX_HEADS * IDX_DIM
    wit_spec = pl.BlockSpec((1, N_IDX_HEADS, tm), lambda b, t: (b, 0, t))
    kv_spec = pl.BlockSpec((1, tm * N_KV_HEADS, HEAD_DIM), lambda b, t: (b, t, 0))
    if mode == "prompt":
        out_shape = [
            jax.ShapeDtypeStruct((B, T, D_MODEL), BF16),
            jax.ShapeDtypeStruct((B, T * N_KV_HEADS, HEAD_DIM), F32),
            jax.ShapeDtypeStruct((B, T * N_KV_HEADS, HEAD_DIM), F32),
            jax.ShapeDtypeStruct((B, T, kvw), BF16),
            jax.ShapeDtypeStruct((B, T // KEY_STEP, N_KV_HEADS * VT_ROWS, KEY_STEP), BF16),
            jax.ShapeDtypeStruct((B, T, qiw), BF16),
            jax.ShapeDtypeStruct((B, T, IDX_DIM), F32),
            jax.ShapeDtypeStruct((B, T, IDX_DIM), BF16),
            jax.ShapeDtypeStruct((B, N_IDX_HEADS, T), F32),
            jax.ShapeDtypeStruct((B, T, D_MODEL), F32),
            jax.ShapeDtypeStruct((B, T, D_MODEL), F32),
            jax.ShapeDtypeStruct((B, CONV_WIDTH - 1, D_CONV), F32),
        ]
        out_specs = [tok(D_MODEL), kv_spec, kv_spec, tok(kvw),
                     pl.BlockSpec((1, tm // KEY_STEP, N_KV_HEADS * VT_ROWS, KEY_STEP), lambda b, t: (b, t, 0, 0)),
                     tok(qiw), tok(IDX_DIM), tok(IDX_DIM), wit_spec, tok(D_MODEL), tok(D_MODEL),
                     pl.BlockSpec((1, CONV_WIDTH - 1, D_CONV), lambda b, t: (b, 0, 0))]
        scratch = [pltpu.VMEM((SUBLANES, D_CONV), F32)]
    else:
        in_specs.append(tok(D_CONV))
        args.append(prev)
        out_shape = [
            jax.ShapeDtypeStruct((B, T, D_MODEL), BF16),
            jax.ShapeDtypeStruct((B, T * N_KV_HEADS, HEAD_DIM), F32),
            jax.ShapeDtypeStruct((B, T * N_KV_HEADS, HEAD_DIM), F32),
            jax.ShapeDtypeStruct((B, T, qiw), BF16),
            jax.ShapeDtypeStruct((B, T, IDX_DIM), F32),
            jax.ShapeDtypeStruct((B, T, IDX_DIM), BF16),
            jax.ShapeDtypeStruct((B, N_IDX_HEADS, T), F32),
            jax.ShapeDtypeStruct((B, T, D_MODEL), F32),
            jax.ShapeDtypeStruct((B, T, D_MODEL), F32),
            jax.ShapeDtypeStruct((B, T, D_CONV), F32),
        ]
        out_specs = [tok(D_MODEL), kv_spec, kv_spec, tok(qiw), tok(IDX_DIM), tok(IDX_DIM), wit_spec,
                     tok(D_MODEL), tok(D_MODEL), tok(D_CONV)]
        scratch = []
    return pl.pallas_call(
        functools.partial(_inproj_kernel, mode=mode, tm=tm),
        grid=(B, nt),
        in_specs=in_specs,
        out_specs=out_specs,
        out_shape=out_shape,
        scratch_shapes=scratch,
        compiler_params=pltpu.CompilerParams(dimension_semantics=("arbitrary", "arbitrary"),
                                             vmem_limit_bytes=VMEM_LIMIT),
        name="inproj_" + mode,
    )(*args)


def _sort_key(x):
    bits = lax.bitcast_convert_type(x, jnp.int32)
    return bits ^ ((bits >> 31) & 0x7FFFFFFF)


def _search_keys(count_ge, lo0, hi0, kk, n_lo0):
    def body(_, c):
        lo, hi, n_lo, n_hi, done = c
        mid = (lo & hi) + ((lo ^ hi) >> 1)
        stuck = mid == lo
        n_mid = count_ge(mid)
        upd = (done < 0.5) & jnp.logical_not(stuck)
        go_lo = n_mid >= kk
        up_lo = upd & go_lo
        lo = jnp.where(up_lo, mid, lo)
        n_lo = jnp.where(up_lo, n_mid, n_lo)
        up_hi = upd & jnp.logical_not(go_lo)
        hi = jnp.where(up_hi, mid, hi)
        n_hi = jnp.where(up_hi, n_mid, n_hi)
        done = jnp.where(stuck | (upd & (n_mid == kk)), 1.0, done)
        return lo, hi, n_lo, n_hi, done

    done0 = jnp.where(n_lo0 <= kk, 1.0, 0.0)
    lo, hi, n_lo, n_hi, _ = lax.fori_loop(0, KEY_BITS, body, (lo0, hi0, n_lo0, jnp.zeros_like(kk), done0))
    return lo, hi, n_lo, n_hi


def _fold_rows(x, op):
    return op(x.reshape(x.shape[0] // SUBLANES, SUBLANES, x.shape[1]), axis=0)


def _prompt_attn_kernel(qi_ref, wit_ref, q_ref, ki_ref, k_ref, vt_ref, o_ref,
                        key_scr, b_scr, d_a, d_b, s_a, s_b, p_a, p_b, acc_scr, *, n_sel, count_step):
    j = pl.program_id(1)
    per_step = KEY_STEP // Q_BLOCK
    n_steps = (j + per_step) // per_step
    n_pairs = (n_steps + 1) // 2
    last = n_steps - 1
    per_count = count_step // KEY_STEP
    n_count = (n_steps + per_count - 1) // per_count
    qi = qi_ref[0]
    q_stack = jnp.concatenate([qi[:, h * IDX_DIM:(h + 1) * IDX_DIM] for h in range(N_IDX_HEADS)], axis=0)
    wit = wit_ref[0]
    q = q_ref[0]
    q_grp = [jnp.concatenate([q[:, (kv * GROUP + g) * HEAD_DIM:(kv * GROUP + g + 1) * HEAD_DIM]
                              for g in range(GROUP)], axis=0) for kv in range(N_KV_HEADS)]
    key_pos = lax.broadcasted_iota(jnp.int32, (KEY_STEP, Q_BLOCK), 0)
    q_pos = lax.broadcasted_iota(jnp.int32, (KEY_STEP, Q_BLOCK), 1) + j * Q_BLOCK

    def step(i):
        return pl.ds(pl.multiple_of(i * KEY_STEP, KEY_STEP), KEY_STEP)

    def idx_dots(i):
        return _dot_nt(ki_ref[0, step(i), :], q_stack)

    def score_trip(i, d_cur, d_nxt, carry):
        mn, mx = carry
        ic = jnp.minimum(i, last)
        d_nxt[...] = idx_dots(jnp.minimum(i + 1, last))
        acc = wit[0:1, :] * jnp.maximum(d_cur[:, 0:Q_BLOCK], 0.0)
        for h in range(1, N_IDX_HEADS):
            acc = acc + wit[h:h + 1, :] * jnp.maximum(d_cur[:, h * Q_BLOCK:(h + 1) * Q_BLOCK], 0.0)
        key = _sort_key(acc)
        masked = jnp.where(key_pos + ic * KEY_STEP <= q_pos, key, INT_MIN)
        key_scr[step(ic), :] = masked
        return jnp.minimum(mn, _fold_rows(key, jnp.min)), jnp.maximum(mx, _fold_rows(masked, jnp.max))

    def logits(i, kv):
        return _dot_nt(k_ref[0, step(i), kv * HEAD_DIM:(kv + 1) * HEAD_DIM], q_grp[kv])

    d_a[...] = idx_dots(0)
    for kv in range(N_KV_HEADS):
        s_a[kv] = logits(0, kv)

    def score_pair(pi, carry):
        carry = score_trip(2 * pi, d_a, d_b, carry)
        return score_trip(2 * pi + 1, d_b, d_a, carry)

    mn, mx = lax.fori_loop(0, n_pairs, score_pair, (jnp.full((SUBLANES, Q_BLOCK), INT_MAX, jnp.int32),
                                                    jnp.full((SUBLANES, Q_BLOCK), INT_MIN, jnp.int32)))
    lo0 = jnp.min(mn, axis=0, keepdims=True)
    hi0 = jnp.max(mx, axis=0, keepdims=True) + 1

    def fill(i, c):
        key_scr[step(i), :] = jnp.full((KEY_STEP, Q_BLOCK), INT_MIN, jnp.int32)
        return c

    lax.fori_loop(n_steps, n_count * per_count, fill, 0)

    def count_ge(t):
        tb = jnp.broadcast_to(t, (SUBLANES, Q_BLOCK))

        def body(i, accs):
            accs = list(accs)
            base = pl.multiple_of(i * count_step, count_step)
            for r in range(count_step // SUBLANES):
                keys = key_scr[pl.ds(base + r * SUBLANES, SUBLANES), :]
                accs[r % COUNT_LANES] = accs[r % COUNT_LANES] + jnp.where(keys >= tb, 1.0, 0.0)
            return tuple(accs)

        accs = lax.fori_loop(0, n_count, body, tuple(jnp.zeros((SUBLANES, Q_BLOCK), F32) for _ in range(COUNT_LANES)))
        return jnp.sum(functools.reduce(lambda a, b: a + b, accs), axis=0, keepdims=True)

    n_valid = (j * Q_BLOCK + 1 + lax.broadcasted_iota(jnp.int32, (1, Q_BLOCK), 1)).astype(F32)
    kk = jnp.minimum(n_valid, float(n_sel))
    lo, hi, n_lo, n_hi = lax.cond((j + 1) * Q_BLOCK <= n_sel,
                                  lambda: (lo0, hi0, n_valid, jnp.zeros_like(kk)),
                                  lambda: _search_keys(count_ge, lo0, hi0, kk, n_valid))
    need = kk - n_hi

    def mask_plain(i, c):
        b_scr[step(i), :] = jnp.where(key_scr[step(i), :] >= lo, 0.0, NEG_INF)
        return c

    def mask_ties(i, run):
        tri = jnp.where(lax.broadcasted_iota(jnp.int32, (KEY_STEP, KEY_STEP), 1)
                        <= lax.broadcasted_iota(jnp.int32, (KEY_STEP, KEY_STEP), 0), 1.0, 0.0).astype(BF16)
        key = key_scr[step(i), :]
        above = key >= hi
        elig = (key >= lo) & jnp.logical_not(above)
        rank = run + _dot(tri, jnp.where(elig, 1.0, 0.0).astype(BF16))
        b_scr[step(i), :] = jnp.where(above | (elig & (rank <= need)), 0.0, NEG_INF)
        return rank[KEY_STEP - 1:KEY_STEP, :]

    any_tie = jnp.max(jnp.where(n_lo > kk, 1.0, 0.0)) > 0.5

    @pl.when(any_tie)
    def _():
        lax.fori_loop(0, n_steps, mask_ties, jnp.zeros((1, Q_BLOCK), F32))

    @pl.when(jnp.logical_not(any_tie))
    def _():
        lax.fori_loop(0, n_steps, mask_plain, 0)

    def pv_dot(i, kv, p_ref):
        return _dot(vt_ref[0, i, kv * VT_ROWS:(kv + 1) * VT_ROWS, :], p_ref[kv])

    def attn_trip(i, s_cur, s_nxt, p_cur, p_prv, m):
        ic = jnp.minimum(i, last)
        pv = [pv_dot(jnp.clip(i - 1, 0, last), kv, p_prv) for kv in range(N_KV_HEADS)]
        for kv in range(N_KV_HEADS):
            s_nxt[kv] = logits(jnp.minimum(i + 1, last), kv)
        b = b_scr[step(ic), :] + jnp.where(i <= last, 0.0, NEG_INF)
        m_rows = []
        for kv in range(N_KV_HEADS):
            ps, alphas = [], []
            for g in range(GROUP):
                h = kv * GROUP + g
                sg = s_cur[kv, :, g * Q_BLOCK:(g + 1) * Q_BLOCK] + b
                m_old = m[h:h + 1, :]
                m_new = jnp.maximum(m_old, jnp.max(_fold_rows(sg, jnp.max), axis=0, keepdims=True))
                ps.append(jnp.exp2(sg - m_new).astype(BF16))
                alphas.append(jnp.exp2(m_old - m_new))
                m_rows.append(m_new)
            p_cur[kv] = jnp.concatenate(ps, axis=1)
            acc_scr[kv] = (acc_scr[kv] + pv[kv]) * jnp.concatenate(alphas, axis=1)
        return jnp.concatenate(m_rows, axis=0)

    acc_scr[...] = jnp.zeros_like(acc_scr)
    p_b[...] = jnp.zeros_like(p_b)

    def attn_pair(pi, m):
        m = attn_trip(2 * pi, s_a, s_b, p_a, p_b, m)
        return attn_trip(2 * pi + 1, s_b, s_a, p_b, p_a, m)

    lax.fori_loop(0, n_pairs, attn_pair, jnp.full((N_HEADS, Q_BLOCK), M_FLOOR, F32))
    i_end = jnp.minimum(2 * n_pairs - 1, last)
    for h in range(N_HEADS):
        kv, g = divmod(h, GROUP)
        if g == 0:
            acc_kv = acc_scr[kv] + pv_dot(i_end, kv, p_b)
        cols = slice(g * Q_BLOCK, (g + 1) * Q_BLOCK)
        inv_l = 1.0 / acc_kv[HEAD_DIM:HEAD_DIM + 1, cols]
        o_ref[0, :, h * HEAD_DIM:(h + 1) * HEAD_DIM] = (acc_kv[0:HEAD_DIM, cols] * inv_l).T


def _prompt_attention(qi, wit, q, ki_bf, k_bf, vt):
    B, T, _ = q.shape
    nq = T // Q_BLOCK
    n_sel = min(TOPK_MAX, T // 4)
    kvw = N_KV_HEADS * HEAD_DIM
    count_step = min(COUNT_STEP, T)
    gq = GROUP * Q_BLOCK
    return pl.pallas_call(
        functools.partial(_prompt_attn_kernel, n_sel=n_sel, count_step=count_step),
        grid=(B, nq),
        in_specs=[pl.BlockSpec((1, Q_BLOCK, N_IDX_HEADS * IDX_DIM), lambda b, j: (b, j, 0)),
                  pl.BlockSpec((1, N_IDX_HEADS, Q_BLOCK), lambda b, j: (b, 0, j)),
                  pl.BlockSpec((1, Q_BLOCK, D_MODEL), lambda b, j: (b, j, 0)),
                  pl.BlockSpec((1, T, IDX_DIM), lambda b, j: (b, 0, 0)),
                  pl.BlockSpec((1, T, kvw), lambda b, j: (b, 0, 0)),
                  pl.BlockSpec((1, T // KEY_STEP, N_KV_HEADS * VT_ROWS, KEY_STEP), lambda b, j: (b, 0, 0, 0))],
        out_specs=pl.BlockSpec((1, Q_BLOCK, D_MODEL), lambda b, j: (b, j, 0)),
        out_shape=jax.ShapeDtypeStruct((B, T, D_MODEL), F32),
        scratch_shapes=[pltpu.VMEM((T, Q_BLOCK), jnp.int32), pltpu.VMEM((T, Q_BLOCK), F32),
                        pltpu.VMEM((KEY_STEP, N_IDX_HEADS * Q_BLOCK), F32),
                        pltpu.VMEM((KEY_STEP, N_IDX_HEADS * Q_BLOCK), F32),
                        pltpu.VMEM((N_KV_HEADS, KEY_STEP, gq), F32), pltpu.VMEM((N_KV_HEADS, KEY_STEP, gq), F32),
                        pltpu.VMEM((N_KV_HEADS, KEY_STEP, gq), BF16), pltpu.VMEM((N_KV_HEADS, KEY_STEP, gq), BF16),
                        pltpu.VMEM((N_KV_HEADS, VT_ROWS, gq), F32)],
        compiler_params=pltpu.CompilerParams(dimension_semantics=("arbitrary", "arbitrary"),
                                             vmem_limit_bytes=VMEM_LIMIT),
        name="prompt_attention",
    )(qi, wit, q, ki_bf, k_bf, vt)


def _sample_score_kernel(pt_ref, qi_ref, wcol_ref, kin_ref, *rest, n_pages, ts, n_seq):
    page_refs, s_ref = rest[:n_seq * n_pages], rest[n_seq * n_pages]
    lane = lax.broadcasted_iota(jnp.int32, (ts, PAGE_SIZE), 1)
    row = lax.broadcasted_iota(jnp.int32, (ts, PAGE_SIZE), 0)
    for q in range(n_seq):
        qi = qi_ref[q].astype(F32)
        q_all = jnp.concatenate([qi[:, h * IDX_DIM:(h + 1) * IDX_DIM] for h in range(N_IDX_HEADS)],
                                axis=0).astype(BF16)
        wcol = wcol_ref[q]

        def score(dots, wcol=wcol):
            r = jnp.maximum(dots, 0.0) * wcol
            acc = r[0:ts]
            for h in range(1, N_IDX_HEADS):
                acc = acc + r[h * ts:(h + 1) * ts]
            return acc

        past = jnp.concatenate([page_refs[q * n_pages + p][0] for p in range(n_pages)], axis=1).astype(BF16)
        s_ref[q, :, 0:n_pages * PAGE_SIZE] = score(_dot(q_all, past))
        new_keys = jnp.concatenate([kin_ref[q], jnp.zeros((PAGE_SIZE - ts, IDX_DIM), F32)], axis=0).astype(BF16)
        s_ref[q, :, n_pages * PAGE_SIZE:(n_pages + 1) * PAGE_SIZE] = jnp.where(
            lane <= row, score(_dot_nt(q_all, new_keys)), NEG_INF)


def _sample_scores(page_table, qi, wcol, ki_bf, cache_kidx):
    nb, ts, _ = qi.shape
    n_pages = page_table.shape[1]
    n_seq = SCORE_SEQS if nb % SCORE_SEQS == 0 else 1

    def page_spec(q, p):
        return pl.BlockSpec((1, IDX_DIM, PAGE_SIZE), lambda b, pt: (pt[b * n_seq + q, p], 0, 0))

    grid_spec = pltpu.PrefetchScalarGridSpec(
        num_scalar_prefetch=1,
        grid=(nb // n_seq,),
        in_specs=[pl.BlockSpec((n_seq, ts, N_IDX_HEADS * IDX_DIM), lambda b, pt: (b, 0, 0)),
                  pl.BlockSpec((n_seq, N_IDX_HEADS * ts, 1), lambda b, pt: (b, 0, 0)),
                  pl.BlockSpec((n_seq, ts, IDX_DIM), lambda b, pt: (b, 0, 0))]
                 + [page_spec(q, p) for q in range(n_seq) for p in range(n_pages)],
        out_specs=pl.BlockSpec((n_seq, ts, (n_pages + 1) * PAGE_SIZE), lambda b, pt: (b, 0, 0)),
    )
    return pl.pallas_call(
        functools.partial(_sample_score_kernel, n_pages=n_pages, ts=ts, n_seq=n_seq),
        grid_spec=grid_spec,
        out_shape=jax.ShapeDtypeStruct((nb, ts, (n_pages + 1) * PAGE_SIZE), F32),
        compiler_params=pltpu.CompilerParams(dimension_semantics=("arbitrary",), vmem_limit_bytes=VMEM_LIMIT),
        name="sample_scores",
    )(page_table, qi, wcol, ki_bf, *([cache_kidx] * (n_seq * n_pages)))


def _sample_select_kernel(s_ref, b_ref, key_scr, *, n_sel, n_chunks):
    rows = s_ref.shape[0]
    tri = jnp.where(lax.broadcasted_iota(jnp.int32, (LANES, LANES), 0)
                    <= lax.broadcasted_iota(jnp.int32, (LANES, LANES), 1), 1.0, 0.0).astype(BF16)

    def cols(c):
        return slice(c * LANES, (c + 1) * LANES)

    mn = jnp.full((rows, LANES), INT_MAX, jnp.int32)
    mx = jnp.full((rows, LANES), INT_MIN, jnp.int32)
    for c in range(n_chunks):
        s = s_ref[:, cols(c)]
        valid = s > NEG_INF
        key = _sort_key(s)
        key_scr[:, cols(c)] = jnp.where(valid, key, INT_MIN)
        mn = jnp.minimum(mn, jnp.where(valid, key, INT_MAX))
        mx = jnp.maximum(mx, jnp.where(valid, key, INT_MIN))
    lo0 = jnp.min(mn, axis=1, keepdims=True)
    hi0 = jnp.max(mx, axis=1, keepdims=True) + 1

    def count_ge(t):
        acc = jnp.where(key_scr[:, cols(0)] >= t, 1.0, 0.0)
        for c in range(1, n_chunks):
            acc = acc + jnp.where(key_scr[:, cols(c)] >= t, 1.0, 0.0)
        return jnp.sum(acc, axis=1, keepdims=True)

    kk = jnp.full((rows, 1), float(n_sel), F32)
    lo, hi, _, n_hi = _search_keys(count_ge, lo0, hi0, kk, count_ge(lo0))
    need = kk - n_hi
    run = jnp.zeros((rows, 1), F32)
    for c in range(n_chunks):
        key = key_scr[:, cols(c)]
        above = key >= hi
        elig = (key >= lo) & jnp.logical_not(above)
        rank = run + _dot(jnp.where(elig, 1.0, 0.0).astype(BF16), tri)
        sel = above | (elig & (rank <= need))
        b_ref[:, cols(c)] = jnp.where(sel, 0.0, NEG_INF)
        run = rank[:, LANES - 1:LANES]


def _sample_select(scores, n_sel, rows_per_step):
    rows, width = scores.shape
    return pl.pallas_call(
        functools.partial(_sample_select_kernel, n_sel=n_sel, n_chunks=width // LANES),
        grid=(rows // rows_per_step,),
        in_specs=[pl.BlockSpec((rows_per_step, width), lambda i: (i, 0))],
        out_specs=pl.BlockSpec((rows_per_step, width), lambda i: (i, 0)),
        out_shape=jax.ShapeDtypeStruct((rows, width), F32),
        scratch_shapes=[pltpu.VMEM((rows_per_step, width), jnp.int32)],
        compiler_params=pltpu.CompilerParams(dimension_semantics=("arbitrary",), vmem_limit_bytes=VMEM_LIMIT),
        name="sample_select",
    )(scores)


def _sample_attn_kernel(pt_ref, q_ref, b_ref, kn_ref, vn_ref, *rest, n_pages, ts):
    k_refs, v_refs, o_ref = rest[:n_pages], rest[n_pages:2 * n_pages], rest[2 * n_pages]
    q = q_ref[0].astype(F32)
    bias = jnp.concatenate([b_ref[0]] * GROUP, axis=0)
    pad = jnp.zeros((PAGE_SIZE - ts, HEAD_DIM), F32)
    for kv in range(N_KV_HEADS):
        qg = jnp.concatenate([q[:, (kv * GROUP + g) * HEAD_DIM:(kv * GROUP + g + 1) * HEAD_DIM]
                              for g in range(GROUP)], axis=0).astype(BF16)

        def head_rows(refs, new_ref, kv=kv):
            tiles = [r[0, pl.ds(kv, PAGE_SIZE, stride=N_KV_HEADS), :] for r in refs]
            tiles += [new_ref[0, pl.ds(kv, ts, stride=N_KV_HEADS), :], pad]
            return jnp.concatenate(tiles, axis=0).astype(BF16)

        s = _dot_nt(qg, head_rows(k_refs, kn_ref)) + bias
        p_ = jnp.exp2(s - jnp.max(s, axis=1, keepdims=True))
        o = _dot(p_.astype(BF16), head_rows(v_refs, vn_ref)) / jnp.sum(p_, axis=1, keepdims=True)
        for g in range(GROUP):
            hh = kv * GROUP + g
            o_ref[0, :, hh * HEAD_DIM:(hh + 1) * HEAD_DIM] = o[g * ts:(g + 1) * ts, :]


def _sample_attention(page_table, q, bias, k_new, v_new, cache_k, cache_v):
    nb, ts, _ = q.shape
    n_pages = page_table.shape[1]
    kvw = N_KV_HEADS * HEAD_DIM

    def page_spec(p):
        return pl.BlockSpec((1, PAGE_SIZE * N_KV_HEADS, HEAD_DIM), lambda b, pt: (pt[b, p], 0, 0))

    grid_spec = pltpu.PrefetchScalarGridSpec(
        num_scalar_prefetch=1,
        grid=(nb,),
        in_specs=[pl.BlockSpec((1, ts, D_MODEL), lambda b, pt: (b, 0, 0)),
                  pl.BlockSpec((1, ts, (n_pages + 1) * PAGE_SIZE), lambda b, pt: (b, 0, 0)),
                  pl.BlockSpec((1, ts * N_KV_HEADS, HEAD_DIM), lambda b, pt: (b, 0, 0)),
                  pl.BlockSpec((1, ts * N_KV_HEADS, HEAD_DIM), lambda b, pt: (b, 0, 0))]
                 + [page_spec(p) for p in range(n_pages)] * 2,
        out_specs=pl.BlockSpec((1, ts, D_MODEL), lambda b, pt: (b, 0, 0)),
    )
    return pl.pallas_call(
        functools.partial(_sample_attn_kernel, n_pages=n_pages, ts=ts),
        grid_spec=grid_spec,
        out_shape=jax.ShapeDtypeStruct((nb, ts, D_MODEL), F32),
        compiler_params=pltpu.CompilerParams(dimension_semantics=("arbitrary",), vmem_limit_bytes=VMEM_LIMIT),
        name="sample_attention",
    )(page_table, q, bias, k_new, v_new, *([cache_k] * n_pages), *([cache_v] * n_pages))


def _ffn_kernel(x_ref, oa_ref, ga_ref, cm_ref, p_ref, wout_ref, gffn_ref, wg_ref, wu_ref, wd_ref,
                gple_ref, wple_ref, wpg_ref, gfin_ref, y_ref, x1_scr, h_scr, acc_scr):
    f = pl.program_id(1)

    @pl.when(f == 0)
    def _():
        merged = ga_ref[...] * oa_ref[...] + cm_ref[...]
        x1 = x_ref[...] + _dot(merged.astype(BF16), wout_ref[...])
        x1_scr[...] = x1
        h_scr[...] = _rmsnorm(x1, gffn_ref[...]).astype(BF16)
        acc_scr[...] = jnp.zeros_like(acc_scr)

    h = h_scr[...]
    g = _dot(h, wg_ref[...])
    u = _dot(h, wu_ref[...])
    acc_scr[...] += _dot((g * jax.nn.sigmoid(g) * u).astype(BF16), wd_ref[...])

    @pl.when(f == pl.num_programs(1) - 1)
    def _():
        x2 = x1_scr[...] + acc_scr[...]
        gate = jax.nn.sigmoid(_dot(_rmsnorm(x2, gple_ref[...]).astype(BF16), wpg_ref[...]))
        x3 = x2 + _dot(p_ref[...].astype(BF16), wple_ref[...]) * gate
        y_ref[...] = _rmsnorm(x3, gfin_ref[...])


def _ffn(x, oa, ga, cm, p, wout, gffn, wg, wu, wd, gple, wple, wpg, gfin, *, tm, tf):
    n = x.shape[0]
    tm = min(tm, n)
    tok = lambda w: pl.BlockSpec((tm, w), lambda i, f: (i, 0))
    vec = lambda a: a.reshape(1, D_MODEL)
    return pl.pallas_call(
        _ffn_kernel,
        grid=(n // tm, D_FF // tf),
        in_specs=[tok(D_MODEL), tok(D_MODEL), tok(D_MODEL), tok(D_MODEL), tok(D_PLE),
                  _const_spec((D_MODEL, D_MODEL)), _const_spec((1, D_MODEL)),
                  pl.BlockSpec((D_MODEL, tf), lambda i, f: (0, f)),
                  pl.BlockSpec((D_MODEL, tf), lambda i, f: (0, f)),
                  pl.BlockSpec((tf, D_MODEL), lambda i, f: (f, 0)),
                  _const_spec((1, D_MODEL)), _const_spec((D_PLE, D_MODEL)), _const_spec((D_MODEL, D_MODEL)),
                  _const_spec((1, D_MODEL))],
        out_specs=tok(D_MODEL),
        out_shape=jax.ShapeDtypeStruct((n, D_MODEL), F32),
        scratch_shapes=[pltpu.VMEM((tm, D_MODEL), F32), pltpu.VMEM((tm, D_MODEL), BF16),
                        pltpu.VMEM((tm, D_MODEL), F32)],
        compiler_params=pltpu.CompilerParams(dimension_semantics=("arbitrary", "arbitrary"),
                                             vmem_limit_bytes=VMEM_LIMIT),
        name="ffn",
    )(x, oa, ga, cm, p, wout, vec(gffn), wg, wu, wd, vec(gple), wple, wpg, vec(gfin))


def _split_w_in(w_in):
    n_att = N_HEADS * HEAD_DIM + 2 * N_KV_HEADS * HEAD_DIM + N_IDX_HEADS * IDX_DIM + IDX_DIM + N_IDX_HEADS
    return w_in[:, :W1_COLS].astype(BF16), w_in[:, n_att:].astype(BF16)


def kernel(x_prompt, x_sample, cache_k, cache_v, cache_kidx, state_conv, page_table, p_prompt, p_sample, norm_mix, w_in, conv_w, w_out, norm_ffn, w_gate_up, w_down, norm_ple, w_ple, w_ple_gate, norm_final):
    Bp, Tp, _ = x_prompt.shape
    Bs, Ts, _ = x_sample.shape
    n_pages = page_table.shape[1]
    past_len = n_pages * PAGE_SIZE
    n_phys = cache_k.shape[1]
    kvw = N_KV_HEADS * HEAD_DIM
    l = 0

    w1, w2 = _split_w_in(w_in[l])
    wout = w_out[l].astype(BF16)
    wg = w_gate_up[l][:, :D_FF].astype(BF16)
    wu = w_gate_up[l][:, D_FF:].astype(BF16)
    wd = w_down[l].astype(BF16)
    wple = w_ple[l].astype(BF16)
    wpg = w_ple_gate[l].astype(BF16)

    tab_p = _rope_table(np.arange(Tp))
    (q_p, k_p, v_p, kbf_p, vt_p, qi_p, ki_p, kibf_p, wit_p, ga_p, cm_p, cnew_p) = _inproj(
        x_prompt, norm_mix[l], w1, w2, conv_w[l], tab_p, None, mode="prompt", tm=min(INPROJ_ROWS, Tp))
    oa_p = _prompt_attention(qi_p, wit_p, q_p, kibf_p, kbf_p, vt_p)
    n_p = Bp * Tp
    y_p = _ffn(x_prompt.reshape(n_p, D_MODEL), oa_p.reshape(n_p, D_MODEL), ga_p.reshape(n_p, D_MODEL),
               cm_p.reshape(n_p, D_MODEL), p_prompt[l].reshape(n_p, D_PLE), wout, norm_ffn[l], wg, wu, wd,
               norm_ple[l], wple, wpg, norm_final, tm=512, tf=D_FF // 2)

    n_s = Bs * Ts
    tm_s = min(INPROJ_ROWS, n_s)
    tab_s = _rope_table(past_len + (np.arange(tm_s) % Ts))
    prev = jnp.concatenate([state_conv[l], jnp.zeros((Bs, Ts - (CONV_WIDTH - 1), D_CONV), F32)], axis=1)
    (q_s, k_s, v_s, qi_s, ki_s, kibf_s, wit_s, ga_s, cm_s, u_s) = _inproj(
        x_sample.reshape(n_s // tm_s, tm_s, D_MODEL), norm_mix[l], w1, w2, conv_w[l], tab_s,
        prev.reshape(n_s // tm_s, tm_s, D_CONV), mode="sample", tm=tm_s)
    wcol = wit_s.transpose(1, 0, 2).reshape(N_IDX_HEADS, Bs, Ts).transpose(1, 0, 2).reshape(Bs, N_IDX_HEADS * Ts, 1)
    scores = _sample_scores(page_table, qi_s.reshape(Bs, Ts, -1), wcol, ki_s.reshape(Bs, Ts, IDX_DIM),
                            jnp.swapaxes(cache_kidx[l], 1, 2))
    n_sel = min(TOPK_MAX, (past_len + Ts) // 4)
    bias = _sample_select(scores.reshape(n_s, -1), n_sel, 256)
    oa_s = _sample_attention(page_table, q_s.reshape(Bs, Ts, D_MODEL), bias.reshape(Bs, Ts, -1),
                             k_s.reshape(Bs, Ts * N_KV_HEADS, HEAD_DIM), v_s.reshape(Bs, Ts * N_KV_HEADS, HEAD_DIM),
                             cache_k[l].reshape(n_phys, PAGE_SIZE * N_KV_HEADS, HEAD_DIM),
                             cache_v[l].reshape(n_phys, PAGE_SIZE * N_KV_HEADS, HEAD_DIM))
    y_s = _ffn(x_sample.reshape(n_s, D_MODEL), oa_s.reshape(n_s, D_MODEL), ga_s.reshape(n_s, D_MODEL),
               cm_s.reshape(n_s, D_MODEL), p_sample[l].reshape(n_s, D_PLE), wout, norm_ffn[l], wg, wu, wd,
               norm_ple[l], wple, wpg, norm_final, tm=512, tf=D_FF // 2)

    return (y_p.reshape(Bp, Tp, D_MODEL), y_s.reshape(Bs, Ts, D_MODEL),
            k_p.reshape(1, Bp, Tp, N_KV_HEADS, HEAD_DIM), v_p.reshape(1, Bp, Tp, N_KV_HEADS, HEAD_DIM),
            ki_p.reshape(1, Bp, Tp, IDX_DIM), cnew_p.reshape(1, Bp, CONV_WIDTH - 1, D_CONV),
            k_s.reshape(1, Bs, Ts, N_KV_HEADS, HEAD_DIM), v_s.reshape(1, Bs, Ts, N_KV_HEADS, HEAD_DIM),
            ki_s.reshape(1, Bs, Ts, IDX_DIM),
            u_s.reshape(Bs, Ts, D_CONV)[:, Ts - (CONV_WIDTH - 1):, :].reshape(1, Bs, CONV_WIDTH - 1, D_CONV))
```

```python
import functools

import numpy as np
import jax
import jax.numpy as jnp
from jax import lax
from jax.experimental import pallas as pl
from jax.experimental.pallas import tpu as pltpu

D_MODEL = 1024
N_HEADS = 8
N_KV_HEADS = 2
GROUP = N_HEADS // N_KV_HEADS
HEAD_DIM = 128
N_IDX_HEADS = 8
IDX_DIM = 64
IDX_SCALE = (N_IDX_HEADS * IDX_DIM) ** -0.5
QK_SCALE = HEAD_DIM ** -0.5
TOPK_MAX = 256
D_CONV = D_MODEL
CONV_WIDTH = 3
D_FF = 2816
D_PLE = 256
PAGE_SIZE = 128
ROPE_THETA = 10000.0
EPS = 1e-6

LANES = 128
SUBLANES = 8
KEY_STEP = 256
Q_BLOCK = 128
COUNT_STEP = 1024
COUNT_LANES = 8
VT_ROWS = HEAD_DIM + 16
CONV_SLAB = 256
INPROJ_ROWS = 512
SCORE_SEQS = 4
VMEM_LIMIT = 56 * 1024 * 1024

W1_COLS = N_HEADS * HEAD_DIM + 2 * N_KV_HEADS * HEAD_DIM + N_IDX_HEADS * IDX_DIM + LANES
W2_COLS = 5 * D_MODEL

F32 = jnp.float32
BF16 = jnp.bfloat16
NEG_INF = float("-inf")
INT_MIN = -2 ** 31
INT_MAX = 2 ** 31 - 1
KEY_BITS = 32
SEARCH_BLIND = 24
M_FLOOR = -1e30
LOG2E = 1.4426950408889634


def _dot(a, b):
    return jnp.dot(a, b, preferred_element_type=F32)


def _dot_nt(a, b):
    return lax.dot_general(a, b, (((1,), (1,)), ((), ())), preferred_element_type=F32)


def _rmsnorm(x, g):
    var = jnp.mean(x * x, axis=-1, keepdims=True)
    return (x * lax.rsqrt(var + EPS)) * g


def _rope_table(pos):
    def tab(half, reps):
        freqs = np.float32(ROPE_THETA) ** (-np.arange(half, dtype=np.float32) / np.float32(half))
        ang = pos.astype(np.float32)[:, None] * freqs[None, :].astype(np.float32)
        c, s = np.cos(ang).astype(np.float32), np.sin(ang).astype(np.float32)
        return np.tile(np.concatenate([c, c], 1), (1, reps)), np.tile(np.concatenate([-s, s], 1), (1, reps))
    c128, s128 = tab(HEAD_DIM // 2, 1)
    c64, s64 = tab(IDX_DIM // 2, LANES // IDX_DIM)
    return jnp.asarray(np.concatenate([c128, s128, c64, s64], axis=1))


def _rope128(x, cos, sin):
    return x * cos + pltpu.roll(x, HEAD_DIM // 2, axis=1) * sin


def _rope64(x, cos, sin, first_half):
    partner = jnp.where(first_half, pltpu.roll(x, LANES - IDX_DIM // 2, axis=1), pltpu.roll(x, IDX_DIM // 2, axis=1))
    return x * cos + partner * sin


def _inproj_kernel(*refs, mode, tm):
    if mode == "prompt":
        (x_ref, g_ref, w1_ref, w2_ref, cw_ref, tab_ref,
         q_ref, k_ref, v_ref, kbf_ref, vt_ref, qi_ref, ki_ref, kibf_ref, wit_ref, ga_ref, cm_ref, cnew_ref,
         carry_ref) = refs
    else:
        (x_ref, g_ref, w1_ref, w2_ref, cw_ref, tab_ref, prev_ref,
         q_ref, k_ref, v_ref, qi_ref, ki_ref, kibf_ref, wit_ref, ga_ref, cm_ref, u_ref) = refs

    x = x_ref[0]
    h = _rmsnorm(x, g_ref[...]).astype(BF16)
    z1 = _dot(h, w1_ref[...])
    tab = tab_ref[...]
    cos128, sin128 = tab[:, 0:LANES], tab[:, LANES:2 * LANES]
    cos64, sin64 = tab[:, 2 * LANES:3 * LANES], tab[:, 3 * LANES:4 * LANES]
    lane = lax.broadcasted_iota(jnp.int32, (tm, LANES), 1)
    first_half = (lane % IDX_DIM) < (IDX_DIM // 2)

    off = 0
    for hh in range(N_HEADS):
        sl = z1[:, off:off + HEAD_DIM]
        q_ref[0, :, hh * HEAD_DIM:(hh + 1) * HEAD_DIM] = (_rope128(sl, cos128, sin128) * (QK_SCALE * LOG2E)).astype(BF16)
        off += HEAD_DIM
    for hh in range(N_KV_HEADS):
        kr = _rope128(z1[:, off:off + HEAD_DIM], cos128, sin128)
        k_ref[0, pl.ds(hh, tm, stride=N_KV_HEADS), :] = kr
        if mode == "prompt":
            kbf_ref[0, :, hh * HEAD_DIM:(hh + 1) * HEAD_DIM] = kr.astype(BF16)
        off += HEAD_DIM
    v = z1[:, off:off + N_KV_HEADS * HEAD_DIM]
    for hh in range(N_KV_HEADS):
        v_ref[0, pl.ds(hh, tm, stride=N_KV_HEADS), :] = v[:, hh * HEAD_DIM:(hh + 1) * HEAD_DIM]
    if mode == "prompt":
        for c in range(tm // KEY_STEP):
            vt = v[c * KEY_STEP:(c + 1) * KEY_STEP, :].T
            ones_row = jnp.where(lax.broadcasted_iota(jnp.int32, (VT_ROWS - HEAD_DIM, KEY_STEP), 0) == 0, 1.0, 0.0)
            for hh in range(N_KV_HEADS):
                vt_ref[0, c, hh * VT_ROWS:(hh + 1) * VT_ROWS, :] = jnp.concatenate(
                    [vt[hh * HEAD_DIM:(hh + 1) * HEAD_DIM, :], ones_row], axis=0).astype(BF16)
    off += N_KV_HEADS * HEAD_DIM
    for hh in range(N_IDX_HEADS * IDX_DIM // LANES):
        sl = z1[:, off:off + LANES]
        qi_ref[0, :, hh * LANES:(hh + 1) * LANES] = _rope64(sl, cos64, sin64, first_half).astype(BF16)
        off += LANES
    kiw = z1[:, off:off + LANES]
    kir = _rope64(kiw, cos64, sin64, first_half)[:, 0:IDX_DIM]
    ki_ref[0] = kir
    kibf_ref[0] = kir.astype(BF16)
    wit_ref[0] = kiw.T[IDX_DIM:IDX_DIM + N_IDX_HEADS, :] * IDX_SCALE

    if mode == "prompt":
        @pl.when(pl.program_id(1) == 0)
        def _():
            carry_ref[...] = jnp.zeros_like(carry_ref)

    row = lax.broadcasted_iota(jnp.int32, (tm, CONV_SLAB), 0)
    for c in range(D_CONV // CONV_SLAB):
        cols = slice(c * CONV_SLAB, (c + 1) * CONV_SLAB)
        bg, cg, xc, ga, gb = (_dot(h, w2_ref[:, k * D_MODEL + c * CONV_SLAB:k * D_MODEL + (c + 1) * CONV_SLAB])
                              for k in range(5))
        u = cg * xc
        r1 = pltpu.roll(u, 1, axis=0)
        r2 = pltpu.roll(u, 2, axis=0)
        if mode == "prompt":
            c0 = carry_ref[0:1, cols]
            c1 = carry_ref[1:2, cols]
            um1 = jnp.where(row == 0, c1, r1)
            um2 = jnp.where(row == 0, c0, jnp.where(row == 1, c1, r2))
            carry_ref[0:2, cols] = u[tm - 2:tm, :]
            cnew_ref[0, :, cols] = u[tm - 2:tm, :]
        else:
            prev = prev_ref[0, :, cols]
            seq_row = row % SUBLANES
            um1 = jnp.where(seq_row == 0, pltpu.roll(prev, tm - 1, axis=0), r1)
            um2 = jnp.where(seq_row < 2, prev, r2)
            u_ref[0, :, cols] = u
        cw = cw_ref[:, cols]
        conv = cw[0:1, :] * um2 + cw[1:2, :] * um1 + cw[2:3, :] * u
        ga_ref[0, :, cols] = jax.nn.sigmoid(ga)
        cm_ref[0, :, cols] = jax.nn.sigmoid(gb) * (bg * conv)


def _const_spec(shape):
    nd = len(shape)
    return pl.BlockSpec(shape, lambda *_: (0,) * nd, pipeline_mode=pl.Buffered(1))


def _inproj(x, norm_g, w1, w2, conv_w, tab, prev, *, mode, tm):
    B, T, _ = x.shape
    nt = T // tm
    tok = lambda w: pl.BlockSpec((1, tm, w), lambda b, t: (b, t, 0))
    in_specs = [tok(D_MODEL), _const_spec((1, D_MODEL)), _const_spec((D_MODEL, W1_COLS)),
                _const_spec((D_MODEL, W2_COLS)), _const_spec((CONV_WIDTH, D_CONV)),
                pl.BlockSpec((tm, 4 * LANES), lambda b, t: (t, 0))]
    args = [x, norm_g.reshape(1, D_MODEL), w1, w2, conv_w, tab]
    kvw = N_KV_HEADS * HEAD_DIM
    qiw = N_IDX_HEADS * IDX_DIM
    wit_spec = pl.BlockSpec((1, N_IDX_HEADS, tm), lambda b, t: (b, 0, t))
    kv_spec = pl.BlockSpec((1, tm * N_KV_HEADS, HEAD_DIM), lambda b, t: (b, t, 0))
    if mode == "prompt":
        out_shape = [
            jax.ShapeDtypeStruct((B, T, D_MODEL), BF16),
            jax.ShapeDtypeStruct((B, T * N_KV_HEADS, HEAD_DIM), F32),
            jax.ShapeDtypeStruct((B, T * N_KV_HEADS, HEAD_DIM), F32),
            jax.ShapeDtypeStruct((B, T, kvw), BF16),
            jax.ShapeDtypeStruct((B, T // KEY_STEP, N_KV_HEADS * VT_ROWS, KEY_STEP), BF16),
            jax.ShapeDtypeStruct((B, T, qiw), BF16),
            jax.ShapeDtypeStruct((B, T, IDX_DIM), F32),
            jax.ShapeDtypeStruct((B, T, IDX_DIM), BF16),
            jax.ShapeDtypeStruct((B, N_IDX_HEADS, T), F32),
            jax.ShapeDtypeStruct((B, T, D_MODEL), F32),
            jax.ShapeDtypeStruct((B, T, D_MODEL), F32),
            jax.ShapeDtypeStruct((B, CONV_WIDTH - 1, D_CONV), F32),
        ]
        out_specs = [tok(D_MODEL), kv_spec, kv_spec, tok(kvw),
                     pl.BlockSpec((1, tm // KEY_STEP, N_KV_HEADS * VT_ROWS, KEY_STEP), lambda b, t: (b, t, 0, 0)),
                     tok(qiw), tok(IDX_DIM), tok(IDX_DIM), wit_spec, tok(D_MODEL), tok(D_MODEL),
                     pl.BlockSpec((1, CONV_WIDTH - 1, D_CONV), lambda b, t: (b, 0, 0))]
        scratch = [pltpu.VMEM((SUBLANES, D_CONV), F32)]
    else:
        in_specs.append(tok(D_CONV))
        args.append(prev)
        out_shape = [
            jax.ShapeDtypeStruct((B, T, D_MODEL), BF16),
            jax.ShapeDtypeStruct((B, T * N_KV_HEADS, HEAD_DIM), F32),
            jax.ShapeDtypeStruct((B, T * N_KV_HEADS, HEAD_DIM), F32),
            jax.ShapeDtypeStruct((B, T, qiw), BF16),
            jax.ShapeDtypeStruct((B, T, IDX_DIM), F32),
            jax.ShapeDtypeStruct((B, T, IDX_DIM), BF16),
            jax.ShapeDtypeStruct((B, N_IDX_HEADS, T), F32),
            jax.ShapeDtypeStruct((B, T, D_MODEL), F32),
            jax.ShapeDtypeStruct((B, T, D_MODEL), F32),
            jax.ShapeDtypeStruct((B, T, D_CONV), F32),
        ]
        out_specs = [tok(D_MODEL), kv_spec, kv_spec, tok(qiw), tok(IDX_DIM), tok(IDX_DIM), wit_spec,
                     tok(D_MODEL), tok(D_MODEL), tok(D_CONV)]
        scratch = []
    return pl.pallas_call(
        functools.partial(_inproj_kernel, mode=mode, tm=tm),
        grid=(B, nt),
        in_specs=in_specs,
        out_specs=out_specs,
        out_shape=out_shape,
        scratch_shapes=scratch,
        compiler_params=pltpu.CompilerParams(dimension_semantics=("arbitrary", "arbitrary"),
                                             vmem_limit_bytes=VMEM_LIMIT),
        name="inproj_" + mode,
    )(*args)


def _sort_key(x):
    bits = lax.bitcast_convert_type(x, jnp.int32)
    return bits ^ ((bits >> 31) & 0x7FFFFFFF)


def _search_keys(count_ge, lo0, hi0, kk, n_lo0):
    def body(_, c):
        lo, hi, n_lo, n_hi, done = c
        mid = (lo & hi) + ((lo ^ hi) >> 1)
        stuck = mid == lo
        n_mid = count_ge(mid)
        upd = (done < 0.5) & jnp.logical_not(stuck)
        go_lo = n_mid >= kk
        up_lo = upd & go_lo
        lo = jnp.where(up_lo, mid, lo)
        n_lo = jnp.where(up_lo, n_mid, n_lo)
        up_hi = upd & jnp.logical_not(go_lo)
        hi = jnp.where(up_hi, mid, hi)
        n_hi = jnp.where(up_hi, n_mid, n_hi)
        done = jnp.where(stuck | (upd & (n_mid == kk)), 1.0, done)
        return lo, hi, n_lo, n_hi, done

    done0 = jnp.where(n_lo0 <= kk, 1.0, 0.0)
    state = (lo0, hi0, n_lo0, jnp.zeros_like(kk), done0)
    state = lax.fori_loop(0, SEARCH_BLIND, body, state)
    state = lax.while_loop(lambda c: jnp.min(c[4]) < 0.5, lambda c: body(0, body(0, c)), state)
    return state[:4]


def _fold_rows(x, op):
    return op(x.reshape(x.shape[0] // SUBLANES, SUBLANES, x.shape[1]), axis=0)


def _prompt_attn_kernel(qi_ref, wit_ref, q_ref, ki_ref, k_ref, vt_ref, o_ref,
                        key_scr, b_scr, d_a, d_b, s_a, s_b, p_a, p_b, acc_scr, *, n_sel, count_step):
    j = pl.program_id(1)
    per_step = KEY_STEP // Q_BLOCK
    n_steps = (j + per_step) // per_step
    n_pairs = (n_steps + 1) // 2
    last = n_steps - 1
    per_count = count_step // KEY_STEP
    n_count = (n_steps + per_count - 1) // per_count
    qi = qi_ref[0]
    q_stack = jnp.concatenate([qi[:, h * IDX_DIM:(h + 1) * IDX_DIM] for h in range(N_IDX_HEADS)], axis=0)
    wit = wit_ref[0]
    q = q_ref[0]
    q_grp = [jnp.concatenate([q[:, (kv * GROUP + g) * HEAD_DIM:(kv * GROUP + g + 1) * HEAD_DIM]
                              for g in range(GROUP)], axis=0) for kv in range(N_KV_HEADS)]
    key_pos = lax.broadcasted_iota(jnp.int32, (KEY_STEP, Q_BLOCK), 0)
    q_pos = lax.broadcasted_iota(jnp.int32, (KEY_STEP, Q_BLOCK), 1) + j * Q_BLOCK

    def step(i):
        return pl.ds(pl.multiple_of(i * KEY_STEP, KEY_STEP), KEY_STEP)

    def idx_dots(i):
        return _dot_nt(ki_ref[0, step(i), :], q_stack)

    def score_trip(i, d_cur, d_nxt, carry):
        mn, mx = carry
        ic = jnp.minimum(i, last)
        d_nxt[...] = idx_dots(jnp.minimum(i + 1, last))
        acc = wit[0:1, :] * jnp.maximum(d_cur[:, 0:Q_BLOCK], 0.0)
        for h in range(1, N_IDX_HEADS):
            acc = acc + wit[h:h + 1, :] * jnp.maximum(d_cur[:, h * Q_BLOCK:(h + 1) * Q_BLOCK], 0.0)
        key = _sort_key(acc)
        masked = jnp.where(key_pos + ic * KEY_STEP <= q_pos, key, INT_MIN)
        key_scr[step(ic), :] = masked
        return jnp.minimum(mn, _fold_rows(key, jnp.min)), jnp.maximum(mx, _fold_rows(masked, jnp.max))

    def logits(i, kv):
        return _dot_nt(k_ref[0, step(i), kv * HEAD_DIM:(kv + 1) * HEAD_DIM], q_grp[kv])

    d_a[...] = idx_dots(0)
    for kv in range(N_KV_HEADS):
        s_a[kv] = logits(0, kv)

    def score_pair(pi, carry):
        carry = score_trip(2 * pi, d_a, d_b, carry)
        return score_trip(2 * pi + 1, d_b, d_a, carry)

    mn, mx = lax.fori_loop(0, n_pairs, score_pair, (jnp.full((SUBLANES, Q_BLOCK), INT_MAX, jnp.int32),
                                                    jnp.full((SUBLANES, Q_BLOCK), INT_MIN, jnp.int32)))
    lo0 = jnp.min(mn, axis=0, keepdims=True)
    hi0 = jnp.max(mx, axis=0, keepdims=True) + 1

    def fill(i, c):
        key_scr[step(i), :] = jnp.full((KEY_STEP, Q_BLOCK), INT_MIN, jnp.int32)
        return c

    lax.fori_loop(n_steps, n_count * per_count, fill, 0)

    def count_ge(t):
        tb = jnp.broadcast_to(t, (SUBLANES, Q_BLOCK))

        def body(i, accs):
            accs = list(accs)
            base = pl.multiple_of(i * count_step, count_step)
            for r in range(count_step // SUBLANES):
                keys = key_scr[pl.ds(base + r * SUBLANES, SUBLANES), :]
                accs[r % COUNT_LANES] = accs[r % COUNT_LANES] + jnp.where(keys >= tb, 1.0, 0.0)
            return tuple(accs)

        accs = lax.fori_loop(0, n_count, body, tuple(jnp.zeros((SUBLANES, Q_BLOCK), F32) for _ in range(COUNT_LANES)))
        return jnp.sum(functools.reduce(lambda a, b: a + b, accs), axis=0, keepdims=True)

    n_valid = (j * Q_BLOCK + 1 + lax.broadcasted_iota(jnp.int32, (1, Q_BLOCK), 1)).astype(F32)
    kk = jnp.minimum(n_valid, float(n_sel))
    lo, hi, n_lo, n_hi = lax.cond((j + 1) * Q_BLOCK <= n_sel,
                                  lambda: (lo0, hi0, n_valid, jnp.zeros_like(kk)),
                                  lambda: _search_keys(count_ge, lo0, hi0, kk, n_valid))
    need = kk - n_hi

    def mask_plain(i, c):
        b_scr[step(i), :] = jnp.where(key_scr[step(i), :] >= lo, 0.0, NEG_INF)
        return c

    def mask_ties(i, run):
        tri = jnp.where(lax.broadcasted_iota(jnp.int32, (KEY_STEP, KEY_STEP), 1)
                        <= lax.broadcasted_iota(jnp.int32, (KEY_STEP, KEY_STEP), 0), 1.0, 0.0).astype(BF16)
        key = key_scr[step(i), :]
        above = key >= hi
        elig = (key >= lo) & jnp.logical_not(above)
        rank = run + _dot(tri, jnp.where(elig, 1.0, 0.0).astype(BF16))
        b_scr[step(i), :] = jnp.where(above | (elig & (rank <= need)), 0.0, NEG_INF)
        return rank[KEY_STEP - 1:KEY_STEP, :]

    any_tie = jnp.max(jnp.where(n_lo > kk, 1.0, 0.0)) > 0.5

    @pl.when(any_tie)
    def _():
        lax.fori_loop(0, n_steps, mask_ties, jnp.zeros((1, Q_BLOCK), F32))

    @pl.when(jnp.logical_not(any_tie))
    def _():
        lax.fori_loop(0, n_steps, mask_plain, 0)

    def pv_dot(i, kv, p_ref):
        return _dot(vt_ref[0, i, kv * VT_ROWS:(kv + 1) * VT_ROWS, :], p_ref[kv])

    def attn_trip(i, s_cur, s_nxt, p_cur, p_prv, m):
        ic = jnp.minimum(i, last)
        pv = [pv_dot(jnp.clip(i - 1, 0, last), kv, p_prv) for kv in range(N_KV_HEADS)]
        for kv in range(N_KV_HEADS):
            s_nxt[kv] = logits(jnp.minimum(i + 1, last), kv)
        b = b_scr[step(ic), :] + jnp.where(i <= last, 0.0, NEG_INF)
        m_rows = []
        for kv in range(N_KV_HEADS):
            ps, alphas = [], []
            for g in range(GROUP):
                h = kv * GROUP + g
                sg = s_cur[kv, :, g * Q_BLOCK:(g + 1) * Q_BLOCK] + b
                m_old = m[h:h + 1, :]
                m_new = jnp.maximum(m_old, jnp.max(_fold_rows(sg, jnp.max), axis=0, keepdims=True))
                ps.append(jnp.exp2(sg - m_new).astype(BF16))
                alphas.append(jnp.exp2(m_old - m_new))
                m_rows.append(m_new)
            p_cur[kv] = jnp.concatenate(ps, axis=1)
            acc_scr[kv] = (acc_scr[kv] + pv[kv]) * jnp.concatenate(alphas, axis=1)
        return jnp.concatenate(m_rows, axis=0)

    acc_scr[...] = jnp.zeros_like(acc_scr)
    p_b[...] = jnp.zeros_like(p_b)

    def attn_pair(pi, m):
        m = attn_trip(2 * pi, s_a, s_b, p_a, p_b, m)
        return attn_trip(2 * pi + 1, s_b, s_a, p_b, p_a, m)

    lax.fori_loop(0, n_pairs, attn_pair, jnp.full((N_HEADS, Q_BLOCK), M_FLOOR, F32))
    i_end = jnp.minimum(2 * n_pairs - 1, last)
    for h in range(N_HEADS):
        kv, g = divmod(h, GROUP)
        if g == 0:
            acc_kv = acc_scr[kv] + pv_dot(i_end, kv, p_b)
        cols = slice(g * Q_BLOCK, (g + 1) * Q_BLOCK)
        inv_l = 1.0 / acc_kv[HEAD_DIM:HEAD_DIM + 1, cols]
        o_ref[0, :, h * HEAD_DIM:(h + 1) * HEAD_DIM] = (acc_kv[0:HEAD_DIM, cols] * inv_l).T


def _prompt_attention(qi, wit, q, ki_bf, k_bf, vt):
    B, T, _ = q.shape
    nq = T // Q_BLOCK
    n_sel = min(TOPK_MAX, T // 4)
    kvw = N_KV_HEADS * HEAD_DIM
    count_step = min(COUNT_STEP, T)
    gq = GROUP * Q_BLOCK
    return pl.pallas_call(
        functools.partial(_prompt_attn_kernel, n_sel=n_sel, count_step=count_step),
        grid=(B, nq),
        in_specs=[pl.BlockSpec((1, Q_BLOCK, N_IDX_HEADS * IDX_DIM), lambda b, j: (b, j, 0)),
                  pl.BlockSpec((1, N_IDX_HEADS, Q_BLOCK), lambda b, j: (b, 0, j)),
                  pl.BlockSpec((1, Q_BLOCK, D_MODEL), lambda b, j: (b, j, 0)),
                  pl.BlockSpec((1, T, IDX_DIM), lambda b, j: (b, 0, 0)),
                  pl.BlockSpec((1, T, kvw), lambda b, j: (b, 0, 0)),
                  pl.BlockSpec((1, T // KEY_STEP, N_KV_HEADS * VT_ROWS, KEY_STEP), lambda b, j: (b, 0, 0, 0))],
        out_specs=pl.BlockSpec((1, Q_BLOCK, D_MODEL), lambda b, j: (b, j, 0)),
        out_shape=jax.ShapeDtypeStruct((B, T, D_MODEL), F32),
        scratch_shapes=[pltpu.VMEM((T, Q_BLOCK), jnp.int32), pltpu.VMEM((T, Q_BLOCK), F32),
                        pltpu.VMEM((KEY_STEP, N_IDX_HEADS * Q_BLOCK), F32),
                        pltpu.VMEM((KEY_STEP, N_IDX_HEADS * Q_BLOCK), F32),
                        pltpu.VMEM((N_KV_HEADS, KEY_STEP, gq), F32), pltpu.VMEM((N_KV_HEADS, KEY_STEP, gq), F32),
                        pltpu.VMEM((N_KV_HEADS, KEY_STEP, gq), BF16), pltpu.VMEM((N_KV_HEADS, KEY_STEP, gq), BF16),
                        pltpu.VMEM((N_KV_HEADS, VT_ROWS, gq), F32)],
        compiler_params=pltpu.CompilerParams(dimension_semantics=("arbitrary", "arbitrary"),
                                             vmem_limit_bytes=VMEM_LIMIT),
        name="prompt_attention",
    )(qi, wit, q, ki_bf, k_bf, vt)


def _sample_score_kernel(pt_ref, qi_ref, wcol_ref, kin_ref, *rest, n_pages, ts, n_seq):
    page_refs, s_ref = rest[:n_seq * n_pages], rest[n_seq * n_pages]
    lane = lax.broadcasted_iota(jnp.int32, (ts, PAGE_SIZE), 1)
    row = lax.broadcasted_iota(jnp.int32, (ts, PAGE_SIZE), 0)
    for q in range(n_seq):
        qi = qi_ref[q].astype(F32)
        q_all = jnp.concatenate([qi[:, h * IDX_DIM:(h + 1) * IDX_DIM] for h in range(N_IDX_HEADS)],
                                axis=0).astype(BF16)
        wcol = wcol_ref[q]

        def score(dots, wcol=wcol):
            r = jnp.maximum(dots, 0.0) * wcol
            acc = r[0:ts]
            for h in range(1, N_IDX_HEADS):
                acc = acc + r[h * ts:(h + 1) * ts]
            return acc

        past = jnp.concatenate([page_refs[q * n_pages + p][0] for p in range(n_pages)], axis=1).astype(BF16)
        s_ref[q, :, 0:n_pages * PAGE_SIZE] = score(_dot(q_all, past))
        new_keys = jnp.concatenate([kin_ref[q], jnp.zeros((PAGE_SIZE - ts, IDX_DIM), F32)], axis=0).astype(BF16)
        s_ref[q, :, n_pages * PAGE_SIZE:(n_pages + 1) * PAGE_SIZE] = jnp.where(
            lane <= row, score(_dot_nt(q_all, new_keys)), NEG_INF)


def _sample_scores(page_table, qi, wcol, ki_bf, cache_kidx):
    nb, ts, _ = qi.shape
    n_pages = page_table.shape[1]
    n_seq = SCORE_SEQS if nb % SCORE_SEQS == 0 else 1

    def page_spec(q, p):
        return pl.BlockSpec((1, IDX_DIM, PAGE_SIZE), lambda b, pt: (pt[b * n_seq + q, p], 0, 0))

    grid_spec = pltpu.PrefetchScalarGridSpec(
        num_scalar_prefetch=1,
        grid=(nb // n_seq,),
        in_specs=[pl.BlockSpec((n_seq, ts, N_IDX_HEADS * IDX_DIM), lambda b, pt: (b, 0, 0)),
                  pl.BlockSpec((n_seq, N_IDX_HEADS * ts, 1), lambda b, pt: (b, 0, 0)),
                  pl.BlockSpec((n_seq, ts, IDX_DIM), lambda b, pt: (b, 0, 0))]
                 + [page_spec(q, p) for q in range(n_seq) for p in range(n_pages)],
        out_specs=pl.BlockSpec((n_seq, ts, (n_pages + 1) * PAGE_SIZE), lambda b, pt: (b, 0, 0)),
    )
    return pl.pallas_call(
        functools.partial(_sample_score_kernel, n_pages=n_pages, ts=ts, n_seq=n_seq),
        grid_spec=grid_spec,
        out_shape=jax.ShapeDtypeStruct((nb, ts, (n_pages + 1) * PAGE_SIZE), F32),
        compiler_params=pltpu.CompilerParams(dimension_semantics=("arbitrary",), vmem_limit_bytes=VMEM_LIMIT),
        name="sample_scores",
    )(page_table, qi, wcol, ki_bf, *([cache_kidx] * (n_seq * n_pages)))


def _sample_select_kernel(s_ref, b_ref, key_scr, *, n_sel, n_chunks):
    rows = s_ref.shape[0]
    tri = jnp.where(lax.broadcasted_iota(jnp.int32, (LANES, LANES), 0)
                    <= lax.broadcasted_iota(jnp.int32, (LANES, LANES), 1), 1.0, 0.0).astype(BF16)

    def cols(c):
        return slice(c * LANES, (c + 1) * LANES)

    mn = jnp.full((rows, LANES), INT_MAX, jnp.int32)
    mx = jnp.full((rows, LANES), INT_MIN, jnp.int32)
    for c in range(n_chunks):
        s = s_ref[:, cols(c)]
        valid = s > NEG_INF
        key = _sort_key(s)
        key_scr[:, cols(c)] = jnp.where(valid, key, INT_MIN)
        mn = jnp.minimum(mn, jnp.where(valid, key, INT_MAX))
        mx = jnp.maximum(mx, jnp.where(valid, key, INT_MIN))
    lo0 = jnp.min(mn, axis=1, keepdims=True)
    hi0 = jnp.max(mx, axis=1, keepdims=True) + 1

    def count_ge(t):
        acc = jnp.where(key_scr[:, cols(0)] >= t, 1.0, 0.0)
        for c in range(1, n_chunks):
            acc = acc + jnp.where(key_scr[:, cols(c)] >= t, 1.0, 0.0)
        return jnp.sum(acc, axis=1, keepdims=True)

    kk = jnp.full((rows, 1), float(n_sel), F32)
    lo, hi, _, n_hi = _search_keys(count_ge, lo0, hi0, kk, count_ge(lo0))
    need = kk - n_hi
    run = jnp.zeros((rows, 1), F32)
    for c in range(n_chunks):
        key = key_scr[:, cols(c)]
        above = key >= hi
        elig = (key >= lo) & jnp.logical_not(above)
        rank = run + _dot(jnp.where(elig, 1.0, 0.0).astype(BF16), tri)
        sel = above | (elig & (rank <= need))
        b_ref[:, cols(c)] = jnp.where(sel, 0.0, NEG_INF)
        run = rank[:, LANES - 1:LANES]


def _sample_select(scores, n_sel, rows_per_step):
    rows, width = scores.shape
    return pl.pallas_call(
        functools.partial(_sample_select_kernel, n_sel=n_sel, n_chunks=width // LANES),
        grid=(rows // rows_per_step,),
        in_specs=[pl.BlockSpec((rows_per_step, width), lambda i: (i, 0))],
        out_specs=pl.BlockSpec((rows_per_step, width), lambda i: (i, 0)),
        out_shape=jax.ShapeDtypeStruct((rows, width), F32),
        scratch_shapes=[pltpu.VMEM((rows_per_step, width), jnp.int32)],
        compiler_params=pltpu.CompilerParams(dimension_semantics=("arbitrary",), vmem_limit_bytes=VMEM_LIMIT),
        name="sample_select",
    )(scores)


def _sample_attn_kernel(pt_ref, q_ref, b_ref, kn_ref, vn_ref, *rest, n_pages, ts):
    k_refs, v_refs, o_ref = rest[:n_pages], rest[n_pages:2 * n_pages], rest[2 * n_pages]
    q = q_ref[0].astype(F32)
    bias = jnp.concatenate([b_ref[0]] * GROUP, axis=0)
    pad = jnp.zeros((PAGE_SIZE - ts, HEAD_DIM), F32)

    def head_rows(refs, new_ref, kv):
        tiles = [r[0, pl.ds(kv, PAGE_SIZE, stride=N_KV_HEADS), :] for r in refs]
        tiles += [new_ref[0, pl.ds(kv, ts, stride=N_KV_HEADS), :], pad]
        return jnp.concatenate(tiles, axis=0).astype(BF16)

    s = []
    for kv in range(N_KV_HEADS):
        qg = jnp.concatenate([q[:, (kv * GROUP + g) * HEAD_DIM:(kv * GROUP + g + 1) * HEAD_DIM]
                              for g in range(GROUP)], axis=0).astype(BF16)
        s.append(_dot_nt(qg, head_rows(k_refs, kn_ref, kv)) + bias)
    p_ = [jnp.exp2(x - jnp.max(x, axis=1, keepdims=True)) for x in s]
    for kv in range(N_KV_HEADS):
        o = _dot(p_[kv].astype(BF16), head_rows(v_refs, vn_ref, kv)) / jnp.sum(p_[kv], axis=1, keepdims=True)
        for g in range(GROUP):
            hh = kv * GROUP + g
            o_ref[0, :, hh * HEAD_DIM:(hh + 1) * HEAD_DIM] = o[g * ts:(g + 1) * ts, :]


def _sample_attention(page_table, q, bias, k_new, v_new, cache_k, cache_v):
    nb, ts, _ = q.shape
    n_pages = page_table.shape[1]
    kvw = N_KV_HEADS * HEAD_DIM

    def page_spec(p):
        return pl.BlockSpec((1, PAGE_SIZE * N_KV_HEADS, HEAD_DIM), lambda b, pt: (pt[b, p], 0, 0))

    grid_spec = pltpu.PrefetchScalarGridSpec(
        num_scalar_prefetch=1,
        grid=(nb,),
        in_specs=[pl.BlockSpec((1, ts, D_MODEL), lambda b, pt: (b, 0, 0)),
                  pl.BlockSpec((1, ts, (n_pages + 1) * PAGE_SIZE), lambda b, pt: (b, 0, 0)),
                  pl.BlockSpec((1, ts * N_KV_HEADS, HEAD_DIM), lambda b, pt: (b, 0, 0)),
                  pl.BlockSpec((1, ts * N_KV_HEADS, HEAD_DIM), lambda b, pt: (b, 0, 0))]
                 + [page_spec(p) for p in range(n_pages)] * 2,
        out_specs=pl.BlockSpec((1, ts, D_MODEL), lambda b, pt: (b, 0, 0)),
    )
    return pl.pallas_call(
        functools.partial(_sample_attn_kernel, n_pages=n_pages, ts=ts),
        grid_spec=grid_spec,
        out_shape=jax.ShapeDtypeStruct((nb, ts, D_MODEL), F32),
        compiler_params=pltpu.CompilerParams(dimension_semantics=("arbitrary",), vmem_limit_bytes=VMEM_LIMIT),
        name="sample_attention",
    )(page_table, q, bias, k_new, v_new, *([cache_k] * n_pages), *([cache_v] * n_pages))


def _ffn_kernel(x_ref, oa_ref, ga_ref, cm_ref, p_ref, wout_ref, gffn_ref, wg_ref, wu_ref, wd_ref,
                gple_ref, wple_ref, wpg_ref, gfin_ref, y_ref, x1_scr, h_scr, acc_scr):
    f = pl.program_id(1)

    @pl.when(f == 0)
    def _():
        merged = ga_ref[...] * oa_ref[...] + cm_ref[...]
        x1 = x_ref[...] + _dot(merged.astype(BF16), wout_ref[...])
        x1_scr[...] = x1
        h_scr[...] = _rmsnorm(x1, gffn_ref[...]).astype(BF16)
        acc_scr[...] = jnp.zeros_like(acc_scr)

    h = h_scr[...]
    g = _dot(h, wg_ref[...])
    u = _dot(h, wu_ref[...])
    acc_scr[...] += _dot((g * jax.nn.sigmoid(g) * u).astype(BF16), wd_ref[...])

    @pl.when(f == pl.num_programs(1) - 1)
    def _():
        x2 = x1_scr[...] + acc_scr[...]
        gate = jax.nn.sigmoid(_dot(_rmsnorm(x2, gple_ref[...]).astype(BF16), wpg_ref[...]))
        x3 = x2 + _dot(p_ref[...].astype(BF16), wple_ref[...]) * gate
        y_ref[...] = _rmsnorm(x3, gfin_ref[...])


def _ffn(x, oa, ga, cm, p, wout, gffn, wg, wu, wd, gple, wple, wpg, gfin, *, tm, tf):
    n = x.shape[0]
    tm = min(tm, n)
    tok = lambda w: pl.BlockSpec((tm, w), lambda i, f: (i, 0))
    vec = lambda a: a.reshape(1, D_MODEL)
    return pl.pallas_call(
        _ffn_kernel,
        grid=(n // tm, D_FF // tf),
        in_specs=[tok(D_MODEL), tok(D_MODEL), tok(D_MODEL), tok(D_MODEL), tok(D_PLE),
                  _const_spec((D_MODEL, D_MODEL)), _const_spec((1, D_MODEL)),
                  pl.BlockSpec((D_MODEL, tf), lambda i, f: (0, f)),
                  pl.BlockSpec((D_MODEL, tf), lambda i, f: (0, f)),
                  pl.BlockSpec((tf, D_MODEL), lambda i, f: (f, 0)),
                  _const_spec((1, D_MODEL)), _const_spec((D_PLE, D_MODEL)), _const_spec((D_MODEL, D_MODEL)),
                  _const_spec((1, D_MODEL))],
        out_specs=tok(D_MODEL),
        out_shape=jax.ShapeDtypeStruct((n, D_MODEL), F32),
        scratch_shapes=[pltpu.VMEM((tm, D_MODEL), F32), pltpu.VMEM((tm, D_MODEL), BF16),
                        pltpu.VMEM((tm, D_MODEL), F32)],
        compiler_params=pltpu.CompilerParams(dimension_semantics=("arbitrary", "arbitrary"),
                                             vmem_limit_bytes=VMEM_LIMIT),
        name="ffn",
    )(x, oa, ga, cm, p, wout, vec(gffn), wg, wu, wd, vec(gple), wple, wpg, vec(gfin))


def _split_w_in(w_in):
    n_att = N_HEADS * HEAD_DIM + 2 * N_KV_HEADS * HEAD_DIM + N_IDX_HEADS * IDX_DIM + IDX_DIM + N_IDX_HEADS
    return w_in[:, :W1_COLS].astype(BF16), w_in[:, n_att:].astype(BF16)


def kernel(x_prompt, x_sample, cache_k, cache_v, cache_kidx, state_conv, page_table, p_prompt, p_sample, norm_mix, w_in, conv_w, w_out, norm_ffn, w_gate_up, w_down, norm_ple, w_ple, w_ple_gate, norm_final):
    Bp, Tp, _ = x_prompt.shape
    Bs, Ts, _ = x_sample.shape
    n_pages = page_table.shape[1]
    past_len = n_pages * PAGE_SIZE
    n_phys = cache_k.shape[1]
    kvw = N_KV_HEADS * HEAD_DIM
    l = 0

    w1, w2 = _split_w_in(w_in[l])
    wout = w_out[l].astype(BF16)
    wg = w_gate_up[l][:, :D_FF].astype(BF16)
    wu = w_gate_up[l][:, D_FF:].astype(BF16)
    wd = w_down[l].astype(BF16)
    wple = w_ple[l].astype(BF16)
    wpg = w_ple_gate[l].astype(BF16)

    tab_p = _rope_table(np.arange(Tp))
    (q_p, k_p, v_p, kbf_p, vt_p, qi_p, ki_p, kibf_p, wit_p, ga_p, cm_p, cnew_p) = _inproj(
        x_prompt, norm_mix[l], w1, w2, conv_w[l], tab_p, None, mode="prompt", tm=min(INPROJ_ROWS, Tp))
    oa_p = _prompt_attention(qi_p, wit_p, q_p, kibf_p, kbf_p, vt_p)
    n_p = Bp * Tp
    y_p = _ffn(x_prompt.reshape(n_p, D_MODEL), oa_p.reshape(n_p, D_MODEL), ga_p.reshape(n_p, D_MODEL),
               cm_p.reshape(n_p, D_MODEL), p_prompt[l].reshape(n_p, D_PLE), wout, norm_ffn[l], wg, wu, wd,
               norm_ple[l], wple, wpg, norm_final, tm=512, tf=D_FF // 2)

    n_s = Bs * Ts
    tm_s = min(INPROJ_ROWS, n_s)
    tab_s = _rope_table(past_len + (np.arange(tm_s) % Ts))
    prev = jnp.concatenate([state_conv[l], jnp.zeros((Bs, Ts - (CONV_WIDTH - 1), D_CONV), F32)], axis=1)
    (q_s, k_s, v_s, qi_s, ki_s, kibf_s, wit_s, ga_s, cm_s, u_s) = _inproj(
        x_sample.reshape(n_s // tm_s, tm_s, D_MODEL), norm_mix[l], w1, w2, conv_w[l], tab_s,
        prev.reshape(n_s // tm_s, tm_s, D_CONV), mode="sample", tm=tm_s)
    wcol = wit_s.transpose(1, 0, 2).reshape(N_IDX_HEADS, Bs, Ts).transpose(1, 0, 2).reshape(Bs, N_IDX_HEADS * Ts, 1)
    scores = _sample_scores(page_table, qi_s.reshape(Bs, Ts, -1), wcol, ki_s.reshape(Bs, Ts, IDX_DIM),
                            jnp.swapaxes(cache_kidx[l], 1, 2))
    n_sel = min(TOPK_MAX, (past_len + Ts) // 4)
    bias = _sample_select(scores.reshape(n_s, -1), n_sel, 256)
    oa_s = _sample_attention(page_table, q_s.reshape(Bs, Ts, D_MODEL), bias.reshape(Bs, Ts, -1),
                             k_s.reshape(Bs, Ts * N_KV_HEADS, HEAD_DIM), v_s.reshape(Bs, Ts * N_KV_HEADS, HEAD_DIM),
                             cache_k[l].reshape(n_phys, PAGE_SIZE * N_KV_HEADS, HEAD_DIM),
                             cache_v[l].reshape(n_phys, PAGE_SIZE * N_KV_HEADS, HEAD_DIM))
    y_s = _ffn(x_sample.reshape(n_s, D_MODEL), oa_s.reshape(n_s, D_MODEL), ga_s.reshape(n_s, D_MODEL),
               cm_s.reshape(n_s, D_MODEL), p_sample[l].reshape(n_s, D_PLE), wout, norm_ffn[l], wg, wu, wd,
               norm_ple[l], wple, wpg, norm_final, tm=512, tf=D_FF // 2)

    return (y_p.reshape(Bp, Tp, D_MODEL), y_s.reshape(Bs, Ts, D_MODEL),
            k_p.reshape(1, Bp, Tp, N_KV_HEADS, HEAD_DIM), v_p.reshape(1, Bp, Tp, N_KV_HEADS, HEAD_DIM),
            ki_p.reshape(1, Bp, Tp, IDX_DIM), cnew_p.reshape(1, Bp, CONV_WIDTH - 1, D_CONV),
            k_s.reshape(1, Bs, Ts, N_KV_HEADS, HEAD_DIM), v_s.reshape(1, Bs, Ts, N_KV_HEADS, HEAD_DIM),
            ki_s.reshape(1, Bs, Ts, IDX_DIM),
            u_s.reshape(Bs, Ts, D_CONV)[:, Ts - (CONV_WIDTH - 1):, :].reshape(1, Bs, CONV_WIDTH - 1, D_CONV))
```

```python
import functools

import numpy as np
import jax
import jax.numpy as jnp
from jax import lax
from jax.experimental import pallas as pl
from jax.experimental.pallas import tpu as pltpu

D_MODEL = 1024
N_HEADS = 8
N_KV_HEADS = 2
GROUP = N_HEADS // N_KV_HEADS
HEAD_DIM = 128
N_IDX_HEADS = 8
IDX_DIM = 64
IDX_SCALE = (N_IDX_HEADS * IDX_DIM) ** -0.5
QK_SCALE = HEAD_DIM ** -0.5
TOPK_MAX = 256
D_CONV = D_MODEL
CONV_WIDTH = 3
D_FF = 2816
D_PLE = 256
PAGE_SIZE = 128
ROPE_THETA = 10000.0
EPS = 1e-6

LANES = 128
SUBLANES = 8
KEY_STEP = 256
Q_BLOCK = 128
COUNT_STEP = 1024
COUNT_LANES = 8
VT_ROWS = HEAD_DIM + 16
CONV_SLAB = 256
INPROJ_ROWS = 512
SCORE_SEQS = 4
ATTN_SEQS = 2
VMEM_LIMIT = 56 * 1024 * 1024

W1_COLS = N_HEADS * HEAD_DIM + 2 * N_KV_HEADS * HEAD_DIM + N_IDX_HEADS * IDX_DIM + LANES
W2_COLS = 5 * D_MODEL

F32 = jnp.float32
BF16 = jnp.bfloat16
NEG_INF = float("-inf")
INT_MIN = -2 ** 31
INT_MAX = 2 ** 31 - 1
KEY_BITS = 32
SEARCH_BLIND = 24
M_FLOOR = -1e30
LOG2E = 1.4426950408889634


def _dot(a, b):
    return jnp.dot(a, b, preferred_element_type=F32)


def _dot_nt(a, b):
    return lax.dot_general(a, b, (((1,), (1,)), ((), ())), preferred_element_type=F32)


def _rmsnorm(x, g):
    var = jnp.mean(x * x, axis=-1, keepdims=True)
    return (x * lax.rsqrt(var + EPS)) * g


def _rope_table(pos):
    def tab(half, reps):
        freqs = np.float32(ROPE_THETA) ** (-np.arange(half, dtype=np.float32) / np.float32(half))
        ang = pos.astype(np.float32)[:, None] * freqs[None, :].astype(np.float32)
        c, s = np.cos(ang).astype(np.float32), np.sin(ang).astype(np.float32)
        return np.tile(np.concatenate([c, c], 1), (1, reps)), np.tile(np.concatenate([-s, s], 1), (1, reps))
    c128, s128 = tab(HEAD_DIM // 2, 1)
    c64, s64 = tab(IDX_DIM // 2, LANES // IDX_DIM)
    return jnp.asarray(np.concatenate([c128, s128, c64, s64], axis=1))


def _rope128(x, cos, sin):
    return x * cos + pltpu.roll(x, HEAD_DIM // 2, axis=1) * sin


def _rope64(x, cos, sin, first_half):
    partner = jnp.where(first_half, pltpu.roll(x, LANES - IDX_DIM // 2, axis=1), pltpu.roll(x, IDX_DIM // 2, axis=1))
    return x * cos + partner * sin


def _inproj_kernel(*refs, mode, tm):
    if mode == "prompt":
        (x_ref, g_ref, w1_ref, w2_ref, cw_ref, tab_ref,
         q_ref, k_ref, v_ref, kbf_ref, vt_ref, qi_ref, ki_ref, kibf_ref, wit_ref, ga_ref, cm_ref, cnew_ref,
         carry_ref) = refs
    else:
        (x_ref, g_ref, w1_ref, w2_ref, cw_ref, tab_ref, prev_ref,
         q_ref, k_ref, v_ref, qi_ref, ki_ref, kibf_ref, wit_ref, ga_ref, cm_ref, u_ref) = refs

    x = x_ref[0]
    h = _rmsnorm(x, g_ref[...]).astype(BF16)
    z1 = _dot(h, w1_ref[...])
    tab = tab_ref[...]
    cos128, sin128 = tab[:, 0:LANES], tab[:, LANES:2 * LANES]
    cos64, sin64 = tab[:, 2 * LANES:3 * LANES], tab[:, 3 * LANES:4 * LANES]
    lane = lax.broadcasted_iota(jnp.int32, (tm, LANES), 1)
    first_half = (lane % IDX_DIM) < (IDX_DIM // 2)

    off = 0
    for hh in range(N_HEADS):
        sl = z1[:, off:off + HEAD_DIM]
        q_ref[0, :, hh * HEAD_DIM:(hh + 1) * HEAD_DIM] = (_rope128(sl, cos128, sin128) * (QK_SCALE * LOG2E)).astype(BF16)
        off += HEAD_DIM
    for hh in range(N_KV_HEADS):
        kr = _rope128(z1[:, off:off + HEAD_DIM], cos128, sin128)
        k_ref[0, pl.ds(hh, tm, stride=N_KV_HEADS), :] = kr
        if mode == "prompt":
            kbf_ref[0, :, hh * HEAD_DIM:(hh + 1) * HEAD_DIM] = kr.astype(BF16)
        off += HEAD_DIM
    v = z1[:, off:off + N_KV_HEADS * HEAD_DIM]
    for hh in range(N_KV_HEADS):
        v_ref[0, pl.ds(hh, tm, stride=N_KV_HEADS), :] = v[:, hh * HEAD_DIM:(hh + 1) * HEAD_DIM]
    if mode == "prompt":
        for c in range(tm // KEY_STEP):
            vt = v[c * KEY_STEP:(c + 1) * KEY_STEP, :].T
            ones_row = jnp.where(lax.broadcasted_iota(jnp.int32, (VT_ROWS - HEAD_DIM, KEY_STEP), 0) == 0, 1.0, 0.0)
            for hh in range(N_KV_HEADS):
                vt_ref[0, c, hh * VT_ROWS:(hh + 1) * VT_ROWS, :] = jnp.concatenate(
                    [vt[hh * HEAD_DIM:(hh + 1) * HEAD_DIM, :], ones_row], axis=0).astype(BF16)
    off += N_KV_HEADS * HEAD_DIM
    for hh in range(N_IDX_HEADS * IDX_DIM // LANES):
        sl = z1[:, off:off + LANES]
        qi_ref[0, :, hh * LANES:(hh + 1) * LANES] = _rope64(sl, cos64, sin64, first_half).astype(BF16)
        off += LANES
    kiw = z1[:, off:off + LANES]
    kir = _rope64(kiw, cos64, sin64, first_half)[:, 0:IDX_DIM]
    ki_ref[0] = kir
    kibf_ref[0] = kir.astype(BF16)
    wit_ref[0] = kiw.T[IDX_DIM:IDX_DIM + N_IDX_HEADS, :] * IDX_SCALE

    if mode == "prompt":
        @pl.when(pl.program_id(1) == 0)
        def _():
            carry_ref[...] = jnp.zeros_like(carry_ref)

    row = lax.broadcasted_iota(jnp.int32, (tm, CONV_SLAB), 0)
    for c in range(D_CONV // CONV_SLAB):
        cols = slice(c * CONV_SLAB, (c + 1) * CONV_SLAB)
        bg, cg, xc, ga, gb = (_dot(h, w2_ref[:, k * D_MODEL + c * CONV_SLAB:k * D_MODEL + (c + 1) * CONV_SLAB])
                              for k in range(5))
        u = cg * xc
        r1 = pltpu.roll(u, 1, axis=0)
        r2 = pltpu.roll(u, 2, axis=0)
        if mode == "prompt":
            c0 = carry_ref[0:1, cols]
            c1 = carry_ref[1:2, cols]
            um1 = jnp.where(row == 0, c1, r1)
            um2 = jnp.where(row == 0, c0, jnp.where(row == 1, c1, r2))
            carry_ref[0:2, cols] = u[tm - 2:tm, :]
            cnew_ref[0, :, cols] = u[tm - 2:tm, :]
        else:
            prev = prev_ref[0, :, cols]
            seq_row = row % SUBLANES
            um1 = jnp.where(seq_row == 0, pltpu.roll(prev, tm - 1, axis=0), r1)
            um2 = jnp.where(seq_row < 2, prev, r2)
            u_ref[0, :, cols] = u
        cw = cw_ref[:, cols]
        conv = cw[0:1, :] * um2 + cw[1:2, :] * um1 + cw[2:3, :] * u
        ga_ref[0, :, cols] = jax.nn.sigmoid(ga)
        cm_ref[0, :, cols] = jax.nn.sigmoid(gb) * (bg * conv)


def _const_spec(shape):
    nd = len(shape)
    return pl.BlockSpec(shape, lambda *_: (0,) * nd, pipeline_mode=pl.Buffered(1))


def _inproj(x, norm_g, w1, w2, conv_w, tab, prev, *, mode, tm):
    B, T, _ = x.shape
    assert T % tm == 0 and tm % KEY_STEP == 0 and D_CONV % CONV_SLAB == 0
    nt = T // tm
    tok = lambda w: pl.BlockSpec((1, tm, w), lambda b, t: (b, t, 0))
    in_specs = [tok(D_MODEL), _const_spec((1, D_MODEL)), _const_spec((D_MODEL, W1_COLS)),
                _const_spec((D_MODEL, W2_COLS)), _const_spec((CONV_WIDTH, D_CONV)),
                pl.BlockSpec((tm, 4 * LANES), lambda b, t: (t, 0))]
    args = [x, norm_g.reshape(1, D_MODEL), w1, w2, conv_w, tab]
    kvw = N_KV_HEADS * HEAD_DIM
    qiw = N_IDX_HEADS * IDX_DIM
    wit_spec = pl.BlockSpec((1, N_IDX_HEADS, tm), lambda b, t: (b, 0, t))
    kv_spec = pl.BlockSpec((1, tm * N_KV_HEADS, HEAD_DIM), lambda b, t: (b, t, 0))
    if mode == "prompt":
        out_shape = [
            jax.ShapeDtypeStruct((B, T, D_MODEL), BF16),
            jax.ShapeDtypeStruct((B, T * N_KV_HEADS, HEAD_DIM), F32),
            jax.ShapeDtypeStruct((B, T * N_KV_HEADS, HEAD_DIM), F32),
            jax.ShapeDtypeStruct((B, T, kvw), BF16),
            jax.ShapeDtypeStruct((B, T // KEY_STEP, N_KV_HEADS * VT_ROWS, KEY_STEP), BF16),
            jax.ShapeDtypeStruct((B, T, qiw), BF16),
            jax.ShapeDtypeStruct((B, T, IDX_DIM), F32),
            jax.ShapeDtypeStruct((B, T, IDX_DIM), BF16),
            jax.ShapeDtypeStruct((B, N_IDX_HEADS, T), F32),
            jax.ShapeDtypeStruct((B, T, D_MODEL), F32),
            jax.ShapeDtypeStruct((B, T, D_MODEL), F32),
            jax.ShapeDtypeStruct((B, CONV_WIDTH - 1, D_CONV), F32),
        ]
        out_specs = [tok(D_MODEL), kv_spec, kv_spec, tok(kvw),
                     pl.BlockSpec((1, tm // KEY_STEP, N_KV_HEADS * VT_ROWS, KEY_STEP), lambda b, t: (b, t, 0, 0)),
                     tok(qiw), tok(IDX_DIM), tok(IDX_DIM), wit_spec, tok(D_MODEL), tok(D_MODEL),
                     pl.BlockSpec((1, CONV_WIDTH - 1, D_CONV), lambda b, t: (b, 0, 0))]
        scratch = [pltpu.VMEM((SUBLANES, D_CONV), F32)]
    else:
        in_specs.append(tok(D_CONV))
        args.append(prev)
        out_shape = [
            jax.ShapeDtypeStruct((B, T, D_MODEL), BF16),
            jax.ShapeDtypeStruct((B, T * N_KV_HEADS, HEAD_DIM), F32),
            jax.ShapeDtypeStruct((B, T * N_KV_HEADS, HEAD_DIM), F32),
            jax.ShapeDtypeStruct((B, T, qiw), BF16),
            jax.ShapeDtypeStruct((B, T, IDX_DIM), F32),
            jax.ShapeDtypeStruct((B, T, IDX_DIM), BF16),
            jax.ShapeDtypeStruct((B, N_IDX_HEADS, T), F32),
            jax.ShapeDtypeStruct((B, T, D_MODEL), F32),
            jax.ShapeDtypeStruct((B, T, D_MODEL), F32),
            jax.ShapeDtypeStruct((B, T, D_CONV), F32),
        ]
        out_specs = [tok(D_MODEL), kv_spec, kv_spec, tok(qiw), tok(IDX_DIM), tok(IDX_DIM), wit_spec,
                     tok(D_MODEL), tok(D_MODEL), tok(D_CONV)]
        scratch = []
    return pl.pallas_call(
        functools.partial(_inproj_kernel, mode=mode, tm=tm),
        grid=(B, nt),
        in_specs=in_specs,
        out_specs=out_specs,
        out_shape=out_shape,
        scratch_shapes=scratch,
        compiler_params=pltpu.CompilerParams(dimension_semantics=("arbitrary", "arbitrary"),
                                             vmem_limit_bytes=VMEM_LIMIT),
        name="inproj_" + mode,
    )(*args)


def _sort_key(x):
    bits = lax.bitcast_convert_type(x, jnp.int32)
    return bits ^ ((bits >> 31) & 0x7FFFFFFF)


def _search_keys(count_ge, lo0, hi0, kk, n_lo0):
    def body(_, c):
        lo, hi, n_lo, n_hi, done = c
        mid = (lo & hi) + ((lo ^ hi) >> 1)
        stuck = mid == lo
        n_mid = count_ge(mid)
        upd = (done < 0.5) & jnp.logical_not(stuck)
        go_lo = n_mid >= kk
        up_lo = upd & go_lo
        lo = jnp.where(up_lo, mid, lo)
        n_lo = jnp.where(up_lo, n_mid, n_lo)
        up_hi = upd & jnp.logical_not(go_lo)
        hi = jnp.where(up_hi, mid, hi)
        n_hi = jnp.where(up_hi, n_mid, n_hi)
        done = jnp.where(stuck | (upd & (n_mid == kk)), 1.0, done)
        return lo, hi, n_lo, n_hi, done

    done0 = jnp.where(n_lo0 <= kk, 1.0, 0.0)
    state = (lo0, hi0, n_lo0, jnp.zeros_like(kk), done0)
    state = lax.fori_loop(0, SEARCH_BLIND, body, state)
    state = lax.while_loop(lambda c: jnp.min(c[4]) < 0.5, lambda c: body(0, body(0, c)), state)
    return state[:4]


def _fold_rows(x, op):
    return op(x.reshape(x.shape[0] // SUBLANES, SUBLANES, x.shape[1]), axis=0)


def _prompt_attn_kernel(qi_ref, wit_ref, q_ref, ki_ref, k_ref, vt_ref, o_ref,
                        key_scr, b_scr, d_a, d_b, s_a, s_b, p_a, p_b, acc_scr, *, n_sel, count_step):
    j = pl.program_id(1)
    per_step = KEY_STEP // Q_BLOCK
    n_steps = (j + per_step) // per_step
    n_pairs = n_steps // 2
    odd = n_steps % 2 == 1
    last = n_steps - 1
    per_count = count_step // KEY_STEP
    n_count = (n_steps + per_count - 1) // per_count
    qi = qi_ref[0]
    q_stack = jnp.concatenate([qi[:, h * IDX_DIM:(h + 1) * IDX_DIM] for h in range(N_IDX_HEADS)], axis=0)
    wit = wit_ref[0]
    q = q_ref[0]
    q_grp = [jnp.concatenate([q[:, (kv * GROUP + g) * HEAD_DIM:(kv * GROUP + g + 1) * HEAD_DIM]
                              for g in range(GROUP)], axis=0) for kv in range(N_KV_HEADS)]
    key_pos = lax.broadcasted_iota(jnp.int32, (KEY_STEP, Q_BLOCK), 0)
    q_pos = lax.broadcasted_iota(jnp.int32, (KEY_STEP, Q_BLOCK), 1) + j * Q_BLOCK

    def step(i):
        return pl.ds(pl.multiple_of(i * KEY_STEP, KEY_STEP), KEY_STEP)

    def idx_dots(i):
        return _dot_nt(ki_ref[0, step(i), :], q_stack)

    def score_trip(i, d_cur, d_nxt, carry):
        mn, mx = carry
        ic = jnp.minimum(i, last)
        d_nxt[...] = idx_dots(jnp.minimum(i + 1, last))
        acc = wit[0:1, :] * jnp.maximum(d_cur[:, 0:Q_BLOCK], 0.0)
        for h in range(1, N_IDX_HEADS):
            acc = acc + wit[h:h + 1, :] * jnp.maximum(d_cur[:, h * Q_BLOCK:(h + 1) * Q_BLOCK], 0.0)
        key = _sort_key(acc)
        masked = jnp.where(key_pos + ic * KEY_STEP <= q_pos, key, INT_MIN)
        key_scr[step(ic), :] = masked
        return jnp.minimum(mn, _fold_rows(key, jnp.min)), jnp.maximum(mx, _fold_rows(masked, jnp.max))

    def logits(i, kv):
        return _dot_nt(k_ref[0, step(i), kv * HEAD_DIM:(kv + 1) * HEAD_DIM], q_grp[kv])

    d_a[...] = idx_dots(0)
    for kv in range(N_KV_HEADS):
        s_a[kv] = logits(0, kv)

    def score_pair(pi, carry):
        carry = score_trip(2 * pi, d_a, d_b, carry)
        return score_trip(2 * pi + 1, d_b, d_a, carry)

    mn, mx = lax.fori_loop(0, n_pairs, score_pair, (jnp.full((SUBLANES, Q_BLOCK), INT_MAX, jnp.int32),
                                                    jnp.full((SUBLANES, Q_BLOCK), INT_MIN, jnp.int32)))
    mn, mx = lax.cond(odd, lambda c: score_trip(last, d_a, d_b, c), lambda c: c, (mn, mx))
    lo0 = jnp.min(mn, axis=0, keepdims=True)
    hi0 = jnp.max(mx, axis=0, keepdims=True) + 1

    def fill(i, c):
        key_scr[step(i), :] = jnp.full((KEY_STEP, Q_BLOCK), INT_MIN, jnp.int32)
        return c

    lax.fori_loop(n_steps, n_count * per_count, fill, 0)

    def count_ge(t):
        tb = jnp.broadcast_to(t, (SUBLANES, Q_BLOCK))

        def body(i, accs):
            accs = list(accs)
            base = pl.multiple_of(i * count_step, count_step)
            for r in range(count_step // SUBLANES):
                keys = key_scr[pl.ds(base + r * SUBLANES, SUBLANES), :]
                accs[r % COUNT_LANES] = accs[r % COUNT_LANES] + jnp.where(keys >= tb, 1.0, 0.0)
            return tuple(accs)

        accs = lax.fori_loop(0, n_count, body, tuple(jnp.zeros((SUBLANES, Q_BLOCK), F32) for _ in range(COUNT_LANES)))
        return jnp.sum(functools.reduce(lambda a, b: a + b, accs), axis=0, keepdims=True)

    n_valid = (j * Q_BLOCK + 1 + lax.broadcasted_iota(jnp.int32, (1, Q_BLOCK), 1)).astype(F32)
    kk = jnp.minimum(n_valid, float(n_sel))
    lo, hi, n_lo, n_hi = lax.cond((j + 1) * Q_BLOCK <= n_sel,
                                  lambda: (lo0, hi0, n_valid, jnp.zeros_like(kk)),
                                  lambda: _search_keys(count_ge, lo0, hi0, kk, n_valid))
    need = kk - n_hi

    def mask_plain(i, c):
        b_scr[step(i), :] = jnp.where(key_scr[step(i), :] >= lo, 0.0, NEG_INF)
        return c

    def mask_ties(i, run):
        tri = jnp.where(lax.broadcasted_iota(jnp.int32, (KEY_STEP, KEY_STEP), 1)
                        <= lax.broadcasted_iota(jnp.int32, (KEY_STEP, KEY_STEP), 0), 1.0, 0.0).astype(BF16)
        key = key_scr[step(i), :]
        above = key >= hi
        elig = (key >= lo) & jnp.logical_not(above)
        rank = run + _dot(tri, jnp.where(elig, 1.0, 0.0).astype(BF16))
        b_scr[step(i), :] = jnp.where(above | (elig & (rank <= need)), 0.0, NEG_INF)
        return rank[KEY_STEP - 1:KEY_STEP, :]

    any_tie = jnp.max(jnp.where(n_lo > kk, 1.0, 0.0)) > 0.5

    @pl.when(any_tie)
    def _():
        lax.fori_loop(0, n_steps, mask_ties, jnp.zeros((1, Q_BLOCK), F32))

    @pl.when(jnp.logical_not(any_tie))
    def _():
        lax.fori_loop(0, n_steps, mask_plain, 0)

    def pv_dot(i, kv, p_ref):
        return _dot(vt_ref[0, i, kv * VT_ROWS:(kv + 1) * VT_ROWS, :], p_ref[kv])

    def attn_trip(i, s_cur, s_nxt, p_cur, p_prv, m):
        ic = jnp.minimum(i, last)
        pv = [pv_dot(jnp.clip(i - 1, 0, last), kv, p_prv) for kv in range(N_KV_HEADS)]
        for kv in range(N_KV_HEADS):
            s_nxt[kv] = logits(jnp.minimum(i + 1, last), kv)
        b = b_scr[step(ic), :] + jnp.where(i <= last, 0.0, NEG_INF)
        m_rows = []
        for kv in range(N_KV_HEADS):
            ps, alphas = [], []
            for g in range(GROUP):
                h = kv * GROUP + g
                sg = s_cur[kv, :, g * Q_BLOCK:(g + 1) * Q_BLOCK] + b
                m_old = m[h:h + 1, :]
                m_new = jnp.maximum(m_old, jnp.max(_fold_rows(sg, jnp.max), axis=0, keepdims=True))
                ps.append(jnp.exp2(sg - m_new).astype(BF16))
                alphas.append(jnp.exp2(m_old - m_new))
                m_rows.append(m_new)
            p_cur[kv] = jnp.concatenate(ps, axis=1)
            acc_scr[kv] = (acc_scr[kv] + pv[kv]) * jnp.concatenate(alphas, axis=1)
        return jnp.concatenate(m_rows, axis=0)

    acc_scr[...] = jnp.zeros_like(acc_scr)
    p_b[...] = jnp.zeros_like(p_b)

    def attn_pair(pi, m):
        m = attn_trip(2 * pi, s_a, s_b, p_a, p_b, m)
        return attn_trip(2 * pi + 1, s_b, s_a, p_b, p_a, m)

    m_end = lax.fori_loop(0, n_pairs, attn_pair, jnp.full((N_HEADS, Q_BLOCK), M_FLOOR, F32))

    @pl.when(odd)
    def _():
        attn_trip(last, s_a, s_b, p_a, p_b, m_end)
        p_b[...] = p_a[...]

    for h in range(N_HEADS):
        kv, g = divmod(h, GROUP)
        if g == 0:
            acc_kv = acc_scr[kv] + pv_dot(last, kv, p_b)
        cols = slice(g * Q_BLOCK, (g + 1) * Q_BLOCK)
        inv_l = 1.0 / acc_kv[HEAD_DIM:HEAD_DIM + 1, cols]
        o_ref[0, :, h * HEAD_DIM:(h + 1) * HEAD_DIM] = (acc_kv[0:HEAD_DIM, cols] * inv_l).T


def _prompt_attention(qi, wit, q, ki_bf, k_bf, vt):
    B, T, _ = q.shape
    nq = T // Q_BLOCK
    n_sel = min(TOPK_MAX, T // 4)
    kvw = N_KV_HEADS * HEAD_DIM
    assert T % KEY_STEP == 0 and KEY_STEP % Q_BLOCK == 0
    count_step = next(c for c in (COUNT_STEP, COUNT_STEP // 2, KEY_STEP) if T % c == 0)
    gq = GROUP * Q_BLOCK
    return pl.pallas_call(
        functools.partial(_prompt_attn_kernel, n_sel=n_sel, count_step=count_step),
        grid=(B, nq),
        in_specs=[pl.BlockSpec((1, Q_BLOCK, N_IDX_HEADS * IDX_DIM), lambda b, j: (b, j, 0)),
                  pl.BlockSpec((1, N_IDX_HEADS, Q_BLOCK), lambda b, j: (b, 0, j)),
                  pl.BlockSpec((1, Q_BLOCK, D_MODEL), lambda b, j: (b, j, 0)),
                  pl.BlockSpec((1, T, IDX_DIM), lambda b, j: (b, 0, 0)),
                  pl.BlockSpec((1, T, kvw), lambda b, j: (b, 0, 0)),
                  pl.BlockSpec((1, T // KEY_STEP, N_KV_HEADS * VT_ROWS, KEY_STEP), lambda b, j: (b, 0, 0, 0))],
        out_specs=pl.BlockSpec((1, Q_BLOCK, D_MODEL), lambda b, j: (b, j, 0)),
        out_shape=jax.ShapeDtypeStruct((B, T, D_MODEL), F32),
        scratch_shapes=[pltpu.VMEM((T, Q_BLOCK), jnp.int32), pltpu.VMEM((T, Q_BLOCK), F32),
                        pltpu.VMEM((KEY_STEP, N_IDX_HEADS * Q_BLOCK), F32),
                        pltpu.VMEM((KEY_STEP, N_IDX_HEADS * Q_BLOCK), F32),
                        pltpu.VMEM((N_KV_HEADS, KEY_STEP, gq), F32), pltpu.VMEM((N_KV_HEADS, KEY_STEP, gq), F32),
                        pltpu.VMEM((N_KV_HEADS, KEY_STEP, gq), BF16), pltpu.VMEM((N_KV_HEADS, KEY_STEP, gq), BF16),
                        pltpu.VMEM((N_KV_HEADS, VT_ROWS, gq), F32)],
        compiler_params=pltpu.CompilerParams(dimension_semantics=("arbitrary", "arbitrary"),
                                             vmem_limit_bytes=VMEM_LIMIT),
        name="prompt_attention",
    )(qi, wit, q, ki_bf, k_bf, vt)


def _sample_score_kernel(pt_ref, qi_ref, wcol_ref, kin_ref, *rest, n_pages, ts, n_seq):
    page_refs, s_ref = rest[:n_seq * n_pages], rest[n_seq * n_pages]
    lane = lax.broadcasted_iota(jnp.int32, (ts, PAGE_SIZE), 1)
    row = lax.broadcasted_iota(jnp.int32, (ts, PAGE_SIZE), 0)
    for q in range(n_seq):
        qi = qi_ref[q].astype(F32)
        q_all = jnp.concatenate([qi[:, h * IDX_DIM:(h + 1) * IDX_DIM] for h in range(N_IDX_HEADS)],
                                axis=0).astype(BF16)
        wcol = wcol_ref[q]

        def score(dots, wcol=wcol):
            r = jnp.maximum(dots, 0.0) * wcol
            acc = r[0:ts]
            for h in range(1, N_IDX_HEADS):
                acc = acc + r[h * ts:(h + 1) * ts]
            return acc

        past = jnp.concatenate([page_refs[q * n_pages + p][0] for p in range(n_pages)], axis=1).astype(BF16)
        s_ref[q, :, 0:n_pages * PAGE_SIZE] = score(_dot(q_all, past))
        new_keys = jnp.concatenate([kin_ref[q], jnp.zeros((PAGE_SIZE - ts, IDX_DIM), F32)], axis=0).astype(BF16)
        s_ref[q, :, n_pages * PAGE_SIZE:(n_pages + 1) * PAGE_SIZE] = jnp.where(
            lane <= row, score(_dot_nt(q_all, new_keys)), NEG_INF)


def _sample_scores(page_table, qi, wcol, ki_bf, cache_kidx):
    nb, ts, _ = qi.shape
    n_pages = page_table.shape[1]
    n_seq = SCORE_SEQS if nb % SCORE_SEQS == 0 else 1

    def page_spec(q, p):
        return pl.BlockSpec((1, IDX_DIM, PAGE_SIZE), lambda b, pt: (pt[b * n_seq + q, p], 0, 0))

    grid_spec = pltpu.PrefetchScalarGridSpec(
        num_scalar_prefetch=1,
        grid=(nb // n_seq,),
        in_specs=[pl.BlockSpec((n_seq, ts, N_IDX_HEADS * IDX_DIM), lambda b, pt: (b, 0, 0)),
                  pl.BlockSpec((n_seq, N_IDX_HEADS * ts, 1), lambda b, pt: (b, 0, 0)),
                  pl.BlockSpec((n_seq, ts, IDX_DIM), lambda b, pt: (b, 0, 0))]
                 + [page_spec(q, p) for q in range(n_seq) for p in range(n_pages)],
        out_specs=pl.BlockSpec((n_seq, ts, (n_pages + 1) * PAGE_SIZE), lambda b, pt: (b, 0, 0)),
    )
    return pl.pallas_call(
        functools.partial(_sample_score_kernel, n_pages=n_pages, ts=ts, n_seq=n_seq),
        grid_spec=grid_spec,
        out_shape=jax.ShapeDtypeStruct((nb, ts, (n_pages + 1) * PAGE_SIZE), F32),
        compiler_params=pltpu.CompilerParams(dimension_semantics=("arbitrary",), vmem_limit_bytes=VMEM_LIMIT),
        name="sample_scores",
    )(page_table, qi, wcol, ki_bf, *([cache_kidx] * (n_seq * n_pages)))


def _sample_select_kernel(s_ref, b_ref, key_scr, *, n_sel, n_chunks):
    rows = s_ref.shape[0]
    tri = jnp.where(lax.broadcasted_iota(jnp.int32, (LANES, LANES), 0)
                    <= lax.broadcasted_iota(jnp.int32, (LANES, LANES), 1), 1.0, 0.0).astype(BF16)

    def cols(c):
        return slice(c * LANES, (c + 1) * LANES)

    mn = jnp.full((rows, LANES), INT_MAX, jnp.int32)
    mx = jnp.full((rows, LANES), INT_MIN, jnp.int32)
    for c in range(n_chunks):
        s = s_ref[:, cols(c)]
        valid = s > NEG_INF
        key = _sort_key(s)
        key_scr[:, cols(c)] = jnp.where(valid, key, INT_MIN)
        mn = jnp.minimum(mn, jnp.where(valid, key, INT_MAX))
        mx = jnp.maximum(mx, jnp.where(valid, key, INT_MIN))
    lo0 = jnp.min(mn, axis=1, keepdims=True)
    hi0 = jnp.max(mx, axis=1, keepdims=True) + 1

    def count_ge(t):
        acc = jnp.where(key_scr[:, cols(0)] >= t, 1.0, 0.0)
        for c in range(1, n_chunks):
            acc = acc + jnp.where(key_scr[:, cols(c)] >= t, 1.0, 0.0)
        return jnp.sum(acc, axis=1, keepdims=True)

    kk = jnp.full((rows, 1), float(n_sel), F32)
    lo, hi, _, n_hi = _search_keys(count_ge, lo0, hi0, kk, count_ge(lo0))
    need = kk - n_hi
    run = jnp.zeros((rows, 1), F32)
    for c in range(n_chunks):
        key = key_scr[:, cols(c)]
        above = key >= hi
        elig = (key >= lo) & jnp.logical_not(above)
        rank = run + _dot(jnp.where(elig, 1.0, 0.0).astype(BF16), tri)
        sel = above | (elig & (rank <= need))
        b_ref[:, cols(c)] = jnp.where(sel, 0.0, NEG_INF)
        run = rank[:, LANES - 1:LANES]


def _sample_select(scores, n_sel, rows_per_step):
    rows, width = scores.shape
    return pl.pallas_call(
        functools.partial(_sample_select_kernel, n_sel=n_sel, n_chunks=width // LANES),
        grid=(rows // rows_per_step,),
        in_specs=[pl.BlockSpec((rows_per_step, width), lambda i: (i, 0))],
        out_specs=pl.BlockSpec((rows_per_step, width), lambda i: (i, 0)),
        out_shape=jax.ShapeDtypeStruct((rows, width), F32),
        scratch_shapes=[pltpu.VMEM((rows_per_step, width), jnp.int32)],
        compiler_params=pltpu.CompilerParams(dimension_semantics=("arbitrary",), vmem_limit_bytes=VMEM_LIMIT),
        name="sample_select",
    )(scores)


def _sample_attn_kernel(pt_ref, q_ref, b_ref, kn_ref, vn_ref, *rest, n_pages, ts, n_seq):
    k_refs, v_refs, o_ref = rest[:n_seq * n_pages], rest[n_seq * n_pages:2 * n_seq * n_pages], rest[2 * n_seq * n_pages]
    pad = jnp.zeros((PAGE_SIZE - ts, HEAD_DIM), F32)

    def head_rows(refs, new_ref, u, kv):
        tiles = [r[0, pl.ds(kv, PAGE_SIZE, stride=N_KV_HEADS), :] for r in refs[u * n_pages:(u + 1) * n_pages]]
        tiles += [new_ref[u, pl.ds(kv, ts, stride=N_KV_HEADS), :], pad]
        return jnp.concatenate(tiles, axis=0).astype(BF16)

    units = [(u, kv) for u in range(n_seq) for kv in range(N_KV_HEADS)]
    s = []
    for u, kv in units:
        q = q_ref[u].astype(F32)
        qg = jnp.concatenate([q[:, (kv * GROUP + g) * HEAD_DIM:(kv * GROUP + g + 1) * HEAD_DIM]
                              for g in range(GROUP)], axis=0).astype(BF16)
        bias = jnp.concatenate([b_ref[u]] * GROUP, axis=0)
        s.append(_dot_nt(qg, head_rows(k_refs, kn_ref, u, kv)) + bias)
    p_ = [jnp.exp2(x - jnp.max(x, axis=1, keepdims=True)) for x in s]
    for (u, kv), pr in zip(units, p_):
        o = _dot(pr.astype(BF16), head_rows(v_refs, vn_ref, u, kv)) / jnp.sum(pr, axis=1, keepdims=True)
        for g in range(GROUP):
            hh = kv * GROUP + g
            o_ref[u, :, hh * HEAD_DIM:(hh + 1) * HEAD_DIM] = o[g * ts:(g + 1) * ts, :]


def _sample_attention(page_table, q, bias, k_new, v_new, cache_k, cache_v):
    nb, ts, _ = q.shape
    n_pages = page_table.shape[1]
    n_seq = ATTN_SEQS if nb % ATTN_SEQS == 0 else 1

    def page_spec(u, p):
        return pl.BlockSpec((1, PAGE_SIZE * N_KV_HEADS, HEAD_DIM), lambda b, pt: (pt[b * n_seq + u, p], 0, 0))

    pages = [page_spec(u, p) for u in range(n_seq) for p in range(n_pages)]
    grid_spec = pltpu.PrefetchScalarGridSpec(
        num_scalar_prefetch=1,
        grid=(nb // n_seq,),
        in_specs=[pl.BlockSpec((n_seq, ts, D_MODEL), lambda b, pt: (b, 0, 0)),
                  pl.BlockSpec((n_seq, ts, (n_pages + 1) * PAGE_SIZE), lambda b, pt: (b, 0, 0)),
                  pl.BlockSpec((n_seq, ts * N_KV_HEADS, HEAD_DIM), lambda b, pt: (b, 0, 0)),
                  pl.BlockSpec((n_seq, ts * N_KV_HEADS, HEAD_DIM), lambda b, pt: (b, 0, 0))] + pages * 2,
        out_specs=pl.BlockSpec((n_seq, ts, D_MODEL), lambda b, pt: (b, 0, 0)),
    )
    return pl.pallas_call(
        functools.partial(_sample_attn_kernel, n_pages=n_pages, ts=ts, n_seq=n_seq),
        grid_spec=grid_spec,
        out_shape=jax.ShapeDtypeStruct((nb, ts, D_MODEL), F32),
        compiler_params=pltpu.CompilerParams(dimension_semantics=("arbitrary",), vmem_limit_bytes=VMEM_LIMIT),
        name="sample_attention",
    )(page_table, q, bias, k_new, v_new, *([cache_k] * (n_seq * n_pages)), *([cache_v] * (n_seq * n_pages)))


def _ffn_kernel(x_ref, oa_ref, ga_ref, cm_ref, p_ref, wout_ref, gffn_ref, wg_ref, wu_ref, wd_ref,
                gple_ref, wple_ref, wpg_ref, gfin_ref, y_ref, x1_scr, h_scr, acc_scr):
    f = pl.program_id(1)

    @pl.when(f == 0)
    def _():
        merged = ga_ref[...] * oa_ref[...] + cm_ref[...]
        x1 = x_ref[...] + _dot(merged.astype(BF16), wout_ref[...])
        x1_scr[...] = x1
        h_scr[...] = _rmsnorm(x1, gffn_ref[...]).astype(BF16)
        acc_scr[...] = jnp.zeros_like(acc_scr)

    h = h_scr[...]
    g = _dot(h, wg_ref[...])
    u = _dot(h, wu_ref[...])
    acc_scr[...] += _dot((g * jax.nn.sigmoid(g) * u).astype(BF16), wd_ref[...])

    @pl.when(f == pl.num_programs(1) - 1)
    def _():
        x2 = x1_scr[...] + acc_scr[...]
        gate = jax.nn.sigmoid(_dot(_rmsnorm(x2, gple_ref[...]).astype(BF16), wpg_ref[...]))
        x3 = x2 + _dot(p_ref[...].astype(BF16), wple_ref[...]) * gate
        y_ref[...] = _rmsnorm(x3, gfin_ref[...])


def _ffn(x, oa, ga, cm, p, wout, gffn, wg, wu, wd, gple, wple, wpg, gfin, *, tm, tf):
    n = x.shape[0]
    tm = min(tm, n)
    tok = lambda w: pl.BlockSpec((tm, w), lambda i, f: (i, 0))
    vec = lambda a: a.reshape(1, D_MODEL)
    return pl.pallas_call(
        _ffn_kernel,
        grid=(n // tm, D_FF // tf),
        in_specs=[tok(D_MODEL), tok(D_MODEL), tok(D_MODEL), tok(D_MODEL), tok(D_PLE),
                  _const_spec((D_MODEL, D_MODEL)), _const_spec((1, D_MODEL)),
                  pl.BlockSpec((D_MODEL, tf), lambda i, f: (0, f)),
                  pl.BlockSpec((D_MODEL, tf), lambda i, f: (0, f)),
                  pl.BlockSpec((tf, D_MODEL), lambda i, f: (f, 0)),
                  _const_spec((1, D_MODEL)), _const_spec((D_PLE, D_MODEL)), _const_spec((D_MODEL, D_MODEL)),
                  _const_spec((1, D_MODEL))],
        out_specs=tok(D_MODEL),
        out_shape=jax.ShapeDtypeStruct((n, D_MODEL), F32),
        scratch_shapes=[pltpu.VMEM((tm, D_MODEL), F32), pltpu.VMEM((tm, D_MODEL), BF16),
                        pltpu.VMEM((tm, D_MODEL), F32)],
        compiler_params=pltpu.CompilerParams(dimension_semantics=("arbitrary", "arbitrary"),
                                             vmem_limit_bytes=VMEM_LIMIT),
        name="ffn",
    )(x, oa, ga, cm, p, wout, vec(gffn), wg, wu, wd, vec(gple), wple, wpg, vec(gfin))


def _split_w_in(w_in):
    n_att = N_HEADS * HEAD_DIM + 2 * N_KV_HEADS * HEAD_DIM + N_IDX_HEADS * IDX_DIM + IDX_DIM + N_IDX_HEADS
    return w_in[:, :W1_COLS].astype(BF16), w_in[:, n_att:].astype(BF16)


def kernel(x_prompt, x_sample, cache_k, cache_v, cache_kidx, state_conv, page_table, p_prompt, p_sample, norm_mix, w_in, conv_w, w_out, norm_ffn, w_gate_up, w_down, norm_ple, w_ple, w_ple_gate, norm_final):
    Bp, Tp, _ = x_prompt.shape
    Bs, Ts, _ = x_sample.shape
    n_pages = page_table.shape[1]
    past_len = n_pages * PAGE_SIZE
    n_phys = cache_k.shape[1]
    kvw = N_KV_HEADS * HEAD_DIM
    l = 0

    w1, w2 = _split_w_in(w_in[l])
    wout = w_out[l].astype(BF16)
    wg = w_gate_up[l][:, :D_FF].astype(BF16)
    wu = w_gate_up[l][:, D_FF:].astype(BF16)
    wd = w_down[l].astype(BF16)
    wple = w_ple[l].astype(BF16)
    wpg = w_ple_gate[l].astype(BF16)

    tab_p = _rope_table(np.arange(Tp))
    (q_p, k_p, v_p, kbf_p, vt_p, qi_p, ki_p, kibf_p, wit_p, ga_p, cm_p, cnew_p) = _inproj(
        x_prompt, norm_mix[l], w1, w2, conv_w[l], tab_p, None, mode="prompt", tm=min(INPROJ_ROWS, Tp))
    oa_p = _prompt_attention(qi_p, wit_p, q_p, kibf_p, kbf_p, vt_p)
    n_p = Bp * Tp
    y_p = _ffn(x_prompt.reshape(n_p, D_MODEL), oa_p.reshape(n_p, D_MODEL), ga_p.reshape(n_p, D_MODEL),
               cm_p.reshape(n_p, D_MODEL), p_prompt[l].reshape(n_p, D_PLE), wout, norm_ffn[l], wg, wu, wd,
               norm_ple[l], wple, wpg, norm_final, tm=512, tf=D_FF // 2)

    n_s = Bs * Ts
    tm_s = min(INPROJ_ROWS, n_s)
    tab_s = _rope_table(past_len + (np.arange(tm_s) % Ts))
    prev = jnp.concatenate([state_conv[l], jnp.zeros((Bs, Ts - (CONV_WIDTH - 1), D_CONV), F32)], axis=1)
    (q_s, k_s, v_s, qi_s, ki_s, kibf_s, wit_s, ga_s, cm_s, u_s) = _inproj(
        x_sample.reshape(n_s // tm_s, tm_s, D_MODEL), norm_mix[l], w1, w2, conv_w[l], tab_s,
        prev.reshape(n_s // tm_s, tm_s, D_CONV), mode="sample", tm=tm_s)
    wcol = wit_s.transpose(1, 0, 2).reshape(N_IDX_HEADS, Bs, Ts).transpose(1, 0, 2).reshape(Bs, N_IDX_HEADS * Ts, 1)
    scores = _sample_scores(page_table, qi_s.reshape(Bs, Ts, -1), wcol, ki_s.reshape(Bs, Ts, IDX_DIM),
                            jnp.swapaxes(cache_kidx[l], 1, 2))
    n_sel = min(TOPK_MAX, (past_len + Ts) // 4)
    bias = _sample_select(scores.reshape(n_s, -1), n_sel, 256)
    oa_s = _sample_attention(page_table, q_s.reshape(Bs, Ts, D_MODEL), bias.reshape(Bs, Ts, -1),
                             k_s.reshape(Bs, Ts * N_KV_HEADS, HEAD_DIM), v_s.reshape(Bs, Ts * N_KV_HEADS, HEAD_DIM),
                             cache_k[l].reshape(n_phys, PAGE_SIZE * N_KV_HEADS, HEAD_DIM),
                             cache_v[l].reshape(n_phys, PAGE_SIZE * N_KV_HEADS, HEAD_DIM))
    y_s = _ffn(x_sample.reshape(n_s, D_MODEL), oa_s.reshape(n_s, D_MODEL), ga_s.reshape(n_s, D_MODEL),
               cm_s.reshape(n_s, D_MODEL), p_sample[l].reshape(n_s, D_PLE), wout, norm_ffn[l], wg, wu, wd,
               norm_ple[l], wple, wpg, norm_final, tm=512, tf=D_FF // 2)

    return (y_p.reshape(Bp, Tp, D_MODEL), y_s.reshape(Bs, Ts, D_MODEL),
            k_p.reshape(1, Bp, Tp, N_KV_HEADS, HEAD_DIM), v_p.reshape(1, Bp, Tp, N_KV_HEADS, HEAD_DIM),
            ki_p.reshape(1, Bp, Tp, IDX_DIM), cnew_p.reshape(1, Bp, CONV_WIDTH - 1, D_CONV),
            k_s.reshape(1, Bs, Ts, N_KV_HEADS, HEAD_DIM), v_s.reshape(1, Bs, Ts, N_KV_HEADS, HEAD_DIM),
            ki_s.reshape(1, Bs, Ts, IDX_DIM),
            u_s.reshape(Bs, Ts, D_CONV)[:, Ts - (CONV_WIDTH - 1):, :].reshape(1, Bs, CONV_WIDTH - 1, D_CONV))
```

```python
import functools

import jax
import jax.numpy as jnp
from jax import lax
from jax.experimental import pallas as pl
from jax.experimental.pallas import tpu as pltpu

D_MODEL = 1024
N_HEADS = 8
N_KV_HEADS = 2
GROUP = N_HEADS // N_KV_HEADS
HEAD_DIM = 128
N_IDX_HEADS = 8
IDX_DIM = 64
IDX_SCALE = (N_IDX_HEADS * IDX_DIM) ** -0.5
QK_SCALE = HEAD_DIM ** -0.5
TOPK_MAX = 256
D_CONV = D_MODEL
CONV_WIDTH = 3
D_FF = 2816
D_PLE = 256
PAGE_SIZE = 128
ROPE_THETA = 10000.0
EPS = 1e-6

LANES = 128
SUBLANES = 8
KEY_STEP = 256
Q_BLOCK = 128
COUNT_STEP = 512
COUNT_LANES = 8
VT_ROWS = HEAD_DIM + 16
CONV_SLAB = 256
INPROJ_ROWS = 512
SCORE_SEQS = 4
ATTN_SEQS = 2
VMEM_LIMIT = 56 * 1024 * 1024

W1_COLS = N_HEADS * HEAD_DIM + 2 * N_KV_HEADS * HEAD_DIM + N_IDX_HEADS * IDX_DIM + LANES
W2_COLS = 5 * D_MODEL

F32 = jnp.float32
BF16 = jnp.bfloat16
NEG_INF = float("-inf")
LOWEST = -3.4028234663852886e38
KEY_BITS = 32
KEY_LOWEST = -0x7F800000
KEY_INF = 0x7F800000
SEARCH_BLIND = 24
M_FLOOR = -1e30
LOG2E = 1.4426950408889634


def _dot(a, b):
    return jnp.dot(a, b, preferred_element_type=F32)


def _dot_nt(a, b):
    return lax.dot_general(a, b, (((1,), (1,)), ((), ())), preferred_element_type=F32)


def _rmsnorm(x, g):
    var = jnp.mean(x * x, axis=-1, keepdims=True)
    return (x * lax.rsqrt(var + EPS)) * g


def _rope_table(pos):
    def tab(half):
        freqs = ROPE_THETA ** (-jnp.arange(half, dtype=F32) / half)
        ang = pos.astype(F32)[:, None] * freqs[None, :]
        return [jnp.cos(ang), jnp.sin(ang)]
    return jnp.concatenate(tab(HEAD_DIM // 2) + tab(IDX_DIM // 2), axis=1)


def _rope128(x, cos, sin):
    return x * cos + pltpu.roll(x, HEAD_DIM // 2, axis=1) * sin


def _rope64(x, cos, sin, first_half):
    partner = jnp.where(first_half, pltpu.roll(x, LANES - IDX_DIM // 2, axis=1), pltpu.roll(x, IDX_DIM // 2, axis=1))
    return x * cos + partner * sin


def _inproj_kernel(*refs, mode, tm):
    if mode == "prompt":
        (x_ref, g_ref, w1_ref, w2_ref, cw_ref, tab_ref,
         q_ref, k_ref, v_ref, kbf_ref, vt_ref, qi_ref, ki_ref, kibf_ref, wit_ref, ga_ref, cm_ref, cnew_ref,
         carry_ref) = refs
    else:
        (x_ref, g_ref, w1_ref, w2_ref, cw_ref, tab_ref, prev_ref,
         q_ref, k_ref, v_ref, qi_ref, ki_ref, kibf_ref, wit_ref, ga_ref, cm_ref, u_ref) = refs

    x = x_ref[0]
    h = _rmsnorm(x, g_ref[...]).astype(BF16)
    z1 = _dot(h, w1_ref[...])
    tab = tab_ref[...]
    h128, h64 = HEAD_DIM // 2, IDX_DIM // 2
    c, sn = tab[:, 0:h128], tab[:, h128:2 * h128]
    cos128, sin128 = jnp.concatenate([c, c], axis=1), jnp.concatenate([-sn, sn], axis=1)
    c, sn = tab[:, 2 * h128:2 * h128 + h64], tab[:, 2 * h128 + h64:2 * h128 + 2 * h64]
    cos64 = jnp.concatenate([c, c] * (LANES // IDX_DIM), axis=1)
    sin64 = jnp.concatenate([-sn, sn] * (LANES // IDX_DIM), axis=1)
    lane = lax.broadcasted_iota(jnp.int32, (tm, LANES), 1)
    first_half = (lane % IDX_DIM) < (IDX_DIM // 2)

    off = 0
    for hh in range(N_HEADS):
        sl = z1[:, off:off + HEAD_DIM]
        q_ref[0, :, hh * HEAD_DIM:(hh + 1) * HEAD_DIM] = (_rope128(sl, cos128, sin128) * (QK_SCALE * LOG2E)).astype(BF16)
        off += HEAD_DIM
    for hh in range(N_KV_HEADS):
        kr = _rope128(z1[:, off:off + HEAD_DIM], cos128, sin128)
        k_ref[0, pl.ds(hh, tm, stride=N_KV_HEADS), :] = kr
        if mode == "prompt":
            kbf_ref[0, :, hh * HEAD_DIM:(hh + 1) * HEAD_DIM] = kr.astype(BF16)
        off += HEAD_DIM
    v = z1[:, off:off + N_KV_HEADS * HEAD_DIM]
    for hh in range(N_KV_HEADS):
        v_ref[0, pl.ds(hh, tm, stride=N_KV_HEADS), :] = v[:, hh * HEAD_DIM:(hh + 1) * HEAD_DIM]
    if mode == "prompt":
        for c in range(tm // KEY_STEP):
            vt = v[c * KEY_STEP:(c + 1) * KEY_STEP, :].T
            ones_row = jnp.where(lax.broadcasted_iota(jnp.int32, (VT_ROWS - HEAD_DIM, KEY_STEP), 0) == 0, 1.0, 0.0)
            for hh in range(N_KV_HEADS):
                vt_ref[0, c, hh * VT_ROWS:(hh + 1) * VT_ROWS, :] = jnp.concatenate(
                    [vt[hh * HEAD_DIM:(hh + 1) * HEAD_DIM, :], ones_row], axis=0).astype(BF16)
    off += N_KV_HEADS * HEAD_DIM
    for hh in range(N_IDX_HEADS * IDX_DIM // LANES):
        sl = z1[:, off:off + LANES]
        qi_ref[0, :, hh * LANES:(hh + 1) * LANES] = _rope64(sl, cos64, sin64, first_half).astype(BF16)
        off += LANES
    kiw = z1[:, off:off + LANES]
    kir = _rope64(kiw, cos64, sin64, first_half)[:, 0:IDX_DIM]
    ki_ref[0] = kir
    kibf_ref[0] = kir.astype(BF16)
    wit_ref[0] = kiw.T[IDX_DIM:IDX_DIM + N_IDX_HEADS, :] * IDX_SCALE

    if mode == "prompt":
        @pl.when(pl.program_id(1) == 0)
        def _():
            carry_ref[...] = jnp.zeros_like(carry_ref)

    row = lax.broadcasted_iota(jnp.int32, (tm, CONV_SLAB), 0)
    for c in range(D_CONV // CONV_SLAB):
        cols = slice(c * CONV_SLAB, (c + 1) * CONV_SLAB)
        bg, cg, xc, ga, gb = (_dot(h, w2_ref[:, k * D_MODEL + c * CONV_SLAB:k * D_MODEL + (c + 1) * CONV_SLAB])
                              for k in range(5))
        u = cg * xc
        r1 = pltpu.roll(u, 1, axis=0)
        r2 = pltpu.roll(u, 2, axis=0)
        if mode == "prompt":
            c0 = carry_ref[0:1, cols]
            c1 = carry_ref[1:2, cols]
            um1 = jnp.where(row == 0, c1, r1)
            um2 = jnp.where(row == 0, c0, jnp.where(row == 1, c1, r2))
            carry_ref[0:2, cols] = u[tm - 2:tm, :]
            cnew_ref[0, :, cols] = u[tm - 2:tm, :]
        else:
            prev = prev_ref[0, :, cols]
            seq_row = row % SUBLANES
            um1 = jnp.where(seq_row == 0, pltpu.roll(prev, tm - 1, axis=0), r1)
            um2 = jnp.where(seq_row < 2, prev, r2)
            u_ref[0, :, cols] = u
        cw = cw_ref[:, cols]
        conv = cw[0:1, :] * um2 + cw[1:2, :] * um1 + cw[2:3, :] * u
        ga_ref[0, :, cols] = jax.nn.sigmoid(ga)
        cm_ref[0, :, cols] = jax.nn.sigmoid(gb) * (bg * conv)


def _const_spec(shape):
    nd = len(shape)
    return pl.BlockSpec(shape, lambda *_: (0,) * nd, pipeline_mode=pl.Buffered(1))


def _inproj(x, norm_g, w1, w2, conv_w, tab, prev, *, mode, tm):
    B, T, _ = x.shape
    assert T % tm == 0 and tm % KEY_STEP == 0 and D_CONV % CONV_SLAB == 0
    nt = T // tm
    tok = lambda w: pl.BlockSpec((1, tm, w), lambda b, t: (b, t, 0))
    in_specs = [tok(D_MODEL), _const_spec((1, D_MODEL)), _const_spec((D_MODEL, W1_COLS)),
                _const_spec((D_MODEL, W2_COLS)), _const_spec((CONV_WIDTH, D_CONV)),
                pl.BlockSpec((tm, HEAD_DIM + IDX_DIM), lambda b, t: (t, 0))]
    args = [x, norm_g.reshape(1, D_MODEL), w1, w2, conv_w, tab]
    kvw = N_KV_HEADS * HEAD_DIM
    qiw = N_IDX_HEADS * IDX_DIM
    wit_spec = pl.BlockSpec((1, N_IDX_HEADS, tm), lambda b, t: (b, 0, t))
    kv_spec = pl.BlockSpec((1, tm * N_KV_HEADS, HEAD_DIM), lambda b, t: (b, t, 0))
    if mode == "prompt":
        out_shape = [
            jax.ShapeDtypeStruct((B, T, D_MODEL), BF16),
            jax.ShapeDtypeStruct((B, T * N_KV_HEADS, HEAD_DIM), F32),
            jax.ShapeDtypeStruct((B, T * N_KV_HEADS, HEAD_DIM), F32),
            jax.ShapeDtypeStruct((B, T, kvw), BF16),
            jax.ShapeDtypeStruct((B, T // KEY_STEP, N_KV_HEADS * VT_ROWS, KEY_STEP), BF16),
            jax.ShapeDtypeStruct((B, T, qiw), BF16),
            jax.ShapeDtypeStruct((B, T, IDX_DIM), F32),
            jax.ShapeDtypeStruct((B, T, IDX_DIM), BF16),
            jax.ShapeDtypeStruct((B, N_IDX_HEADS, T), F32),
            jax.ShapeDtypeStruct((B, T, D_MODEL), F32),
            jax.ShapeDtypeStruct((B, T, D_MODEL), F32),
            jax.ShapeDtypeStruct((B, CONV_WIDTH - 1, D_CONV), F32),
        ]
        out_specs = [tok(D_MODEL), kv_spec, kv_spec, tok(kvw),
                     pl.BlockSpec((1, tm // KEY_STEP, N_KV_HEADS * VT_ROWS, KEY_STEP), lambda b, t: (b, t, 0, 0)),
                     tok(qiw), tok(IDX_DIM), tok(IDX_DIM), wit_spec, tok(D_MODEL), tok(D_MODEL),
                     pl.BlockSpec((1, CONV_WIDTH - 1, D_CONV), lambda b, t: (b, 0, 0))]
        scratch = [pltpu.VMEM((SUBLANES, D_CONV), F32)]
    else:
        in_specs.append(tok(D_CONV))
        args.append(prev)
        out_shape = [
            jax.ShapeDtypeStruct((B, T, D_MODEL), BF16),
            jax.ShapeDtypeStruct((B, T * N_KV_HEADS, HEAD_DIM), F32),
            jax.ShapeDtypeStruct((B, T * N_KV_HEADS, HEAD_DIM), F32),
            jax.ShapeDtypeStruct((B, T, qiw), BF16),
            jax.ShapeDtypeStruct((B, T, IDX_DIM), F32),
            jax.ShapeDtypeStruct((B, T, IDX_DIM), BF16),
            jax.ShapeDtypeStruct((B, N_IDX_HEADS, T), F32),
            jax.ShapeDtypeStruct((B, T, D_MODEL), F32),
            jax.ShapeDtypeStruct((B, T, D_MODEL), F32),
            jax.ShapeDtypeStruct((B, T, D_CONV), F32),
        ]
        out_specs = [tok(D_MODEL), kv_spec, kv_spec, tok(qiw), tok(IDX_DIM), tok(IDX_DIM), wit_spec,
                     tok(D_MODEL), tok(D_MODEL), tok(D_CONV)]
        scratch = []
    return pl.pallas_call(
        functools.partial(_inproj_kernel, mode=mode, tm=tm),
        grid=(B, nt),
        in_specs=in_specs,
        out_specs=out_specs,
        out_shape=out_shape,
        scratch_shapes=scratch,
        compiler_params=pltpu.CompilerParams(dimension_semantics=("arbitrary", "arbitrary"),
                                             vmem_limit_bytes=VMEM_LIMIT),
        name="inproj_" + mode,
    )(*args)


def _key_value(key):
    return lax.bitcast_convert_type(key ^ ((key >> 31) & 0x7FFFFFFF), F32)


def _search_keys(count_ge, kk, n_all):
    lo0 = jnp.full(kk.shape, KEY_LOWEST, jnp.int32)
    hi0 = jnp.full(kk.shape, KEY_INF, jnp.int32)
    n_lo0 = n_all
    def body(_, c):
        lo, hi, n_lo, n_hi, done = c
        mid = (lo & hi) + ((lo ^ hi) >> 1)
        stuck = mid == lo
        n_mid = count_ge(mid)
        upd = (done < 0.5) & jnp.logical_not(stuck)
        go_lo = n_mid >= kk
        up_lo = upd & go_lo
        lo = jnp.where(up_lo, mid, lo)
        n_lo = jnp.where(up_lo, n_mid, n_lo)
        up_hi = upd & jnp.logical_not(go_lo)
        hi = jnp.where(up_hi, mid, hi)
        n_hi = jnp.where(up_hi, n_mid, n_hi)
        done = jnp.where(stuck | (upd & (n_mid == kk)), 1.0, done)
        return lo, hi, n_lo, n_hi, done

    done0 = jnp.where(n_lo0 <= kk, 1.0, 0.0)
    state = (lo0, hi0, n_lo0, jnp.zeros_like(kk), done0)
    state = lax.fori_loop(0, SEARCH_BLIND, body, state)
    lo, hi, n_lo, n_hi, _ = lax.while_loop(lambda c: jnp.min(c[4]) < 0.5, lambda c: body(0, body(0, c)), state)
    return _key_value(lo), _key_value(hi), n_lo, n_hi


def _spread_ties(sc, lo, hi):
    return jnp.where(sc >= hi, -LOWEST, jnp.where(sc >= lo, sc - lo, NEG_INF))


def _fold_rows(x, op):
    return op(x.reshape(x.shape[0] // SUBLANES, SUBLANES, x.shape[1]), axis=0)


def _prompt_attn_kernel(qi_ref, wit_ref, q_ref, ki_ref, k_ref, vt_ref, o_ref,
                        sc_scr, b_scr, d_a, d_b, s_a, s_b, p_a, p_b, acc_scr, *, n_sel, count_step):
    j = pl.program_id(1)
    per_step = KEY_STEP // Q_BLOCK
    n_steps = (j + per_step) // per_step
    n_pairs = n_steps // 2
    odd = n_steps % 2 == 1
    last = n_steps - 1
    per_count = count_step // KEY_STEP
    n_count = (n_steps + per_count - 1) // per_count
    qi = qi_ref[0]
    q_stack = jnp.concatenate([qi[:, h * IDX_DIM:(h + 1) * IDX_DIM] for h in range(N_IDX_HEADS)], axis=0)
    wit = wit_ref[0]
    q = q_ref[0]
    q_grp = [jnp.concatenate([q[:, (kv * GROUP + g) * HEAD_DIM:(kv * GROUP + g + 1) * HEAD_DIM]
                              for g in range(GROUP)], axis=0) for kv in range(N_KV_HEADS)]
    key_pos = lax.broadcasted_iota(jnp.int32, (KEY_STEP, Q_BLOCK), 0)
    q_pos = lax.broadcasted_iota(jnp.int32, (KEY_STEP, Q_BLOCK), 1) + j * Q_BLOCK

    def step(i):
        return pl.ds(pl.multiple_of(i * KEY_STEP, KEY_STEP), KEY_STEP)

    def idx_dots(i):
        return _dot_nt(ki_ref[0, step(i), :], q_stack)

    def score_trip(i, d_cur, d_nxt):
        d_nxt[...] = idx_dots(jnp.minimum(i + 1, last))
        acc = wit[0:1, :] * jnp.maximum(d_cur[:, 0:Q_BLOCK], 0.0)
        for h in range(1, N_IDX_HEADS):
            acc = acc + wit[h:h + 1, :] * jnp.maximum(d_cur[:, h * Q_BLOCK:(h + 1) * Q_BLOCK], 0.0)
        sc_scr[step(i), :] = jnp.where(key_pos + i * KEY_STEP <= q_pos, acc, NEG_INF)

    def logits(i, kv):
        return _dot_nt(k_ref[0, step(i), kv * HEAD_DIM:(kv + 1) * HEAD_DIM], q_grp[kv])

    d_a[...] = idx_dots(0)
    for kv in range(N_KV_HEADS):
        s_a[kv] = logits(0, kv)

    def score_pair(pi, c):
        score_trip(2 * pi, d_a, d_b)
        score_trip(2 * pi + 1, d_b, d_a)
        return c

    lax.fori_loop(0, n_pairs, score_pair, 0)

    @pl.when(odd)
    def _():
        score_trip(last, d_a, d_b)

    def fill(i, c):
        sc_scr[step(i), :] = jnp.full((KEY_STEP, Q_BLOCK), NEG_INF, F32)
        return c

    lax.fori_loop(n_steps, n_count * per_count, fill, 0)

    def count_ge(t):
        tb = jnp.broadcast_to(_key_value(t), (SUBLANES, Q_BLOCK))

        def body(i, accs):
            accs = list(accs)
            base = pl.multiple_of(i * count_step, count_step)
            for r in range(count_step // SUBLANES):
                sc = sc_scr[pl.ds(base + r * SUBLANES, SUBLANES), :]
                accs[r % COUNT_LANES] = accs[r % COUNT_LANES] + jnp.where(sc >= tb, 1.0, 0.0)
            return tuple(accs)

        accs = lax.fori_loop(0, n_count, body, tuple(jnp.zeros((SUBLANES, Q_BLOCK), F32) for _ in range(COUNT_LANES)))
        return jnp.sum(functools.reduce(lambda a, b: a + b, accs), axis=0, keepdims=True)

    n_valid = (j * Q_BLOCK + 1 + lax.broadcasted_iota(jnp.int32, (1, Q_BLOCK), 1)).astype(F32)
    kk = jnp.minimum(n_valid, float(n_sel))
    lo, hi, n_lo, n_hi = lax.cond((j + 1) * Q_BLOCK <= n_sel,
                                  lambda: (jnp.full_like(kk, LOWEST), jnp.full_like(kk, jnp.inf), n_valid,
                                           jnp.zeros_like(kk)),
                                  lambda: _search_keys(count_ge, kk, n_valid))

    def mask_plain(i, c):
        b_scr[step(i), :] = jnp.where(sc_scr[step(i), :] >= lo, 0.0, NEG_INF)
        return c

    def spread(i, c):
        sc_scr[step(i), :] = _spread_ties(sc_scr[step(i), :], lo, hi)
        return c

    any_tie = jnp.max(jnp.where(n_lo > kk, 1.0, 0.0)) > 0.5

    @pl.when(any_tie)
    def _():
        lax.fori_loop(0, n_steps, spread, 0)
        lo2, hi2, _, n_hi2 = _search_keys(count_ge, kk, n_lo)
        tri = jnp.where(lax.broadcasted_iota(jnp.int32, (KEY_STEP, KEY_STEP), 1)
                        <= lax.broadcasted_iota(jnp.int32, (KEY_STEP, KEY_STEP), 0), 1.0, 0.0).astype(BF16)

        def mask_ties(i, run):
            sc = sc_scr[step(i), :]
            above = sc >= hi2
            elig = (sc >= lo2) & jnp.logical_not(above)
            rank = run + _dot(tri, jnp.where(elig, 1.0, 0.0).astype(BF16))
            b_scr[step(i), :] = jnp.where(above | (elig & (rank <= kk - n_hi2)), 0.0, NEG_INF)
            return rank[KEY_STEP - 1:KEY_STEP, :]

        lax.fori_loop(0, n_steps, mask_ties, jnp.zeros((1, Q_BLOCK), F32))

    @pl.when(jnp.logical_not(any_tie))
    def _():
        lax.fori_loop(0, n_steps, mask_plain, 0)

    def pv_dot(i, kv, p_ref):
        return _dot(vt_ref[0, i, kv * VT_ROWS:(kv + 1) * VT_ROWS, :], p_ref[kv])

    def attn_trip(i, s_cur, s_nxt, p_cur, p_prv, m):
        pv = [pv_dot(jnp.maximum(i - 1, 0), kv, p_prv) for kv in range(N_KV_HEADS)]
        for kv in range(N_KV_HEADS):
            s_nxt[kv] = logits(jnp.minimum(i + 1, last), kv)
        b = b_scr[step(i), :]
        m_rows = []
        for kv in range(N_KV_HEADS):
            ps, alphas = [], []
            for g in range(GROUP):
                h = kv * GROUP + g
                sg = s_cur[kv, :, g * Q_BLOCK:(g + 1) * Q_BLOCK] + b
                m_old = m[h:h + 1, :]
                m_new = jnp.maximum(m_old, jnp.max(_fold_rows(sg, jnp.max), axis=0, keepdims=True))
                ps.append(jnp.exp2(sg - m_new).astype(BF16))
                alphas.append(jnp.exp2(m_old - m_new))
                m_rows.append(m_new)
            p_cur[kv] = jnp.concatenate(ps, axis=1)
            acc_scr[kv] = (acc_scr[kv] + pv[kv]) * jnp.concatenate(alphas, axis=1)
        return jnp.concatenate(m_rows, axis=0)

    acc_scr[...] = jnp.zeros_like(acc_scr)
    p_b[...] = jnp.zeros_like(p_b)

    def attn_pair(pi, m):
        m = attn_trip(2 * pi, s_a, s_b, p_a, p_b, m)
        return attn_trip(2 * pi + 1, s_b, s_a, p_b, p_a, m)

    m_end = lax.fori_loop(0, n_pairs, attn_pair, jnp.full((N_HEADS, Q_BLOCK), M_FLOOR, F32))

    @pl.when(odd)
    def _():
        attn_trip(last, s_a, s_b, p_a, p_b, m_end)
        p_b[...] = p_a[...]

    for h in range(N_HEADS):
        kv, g = divmod(h, GROUP)
        if g == 0:
            acc_kv = acc_scr[kv] + pv_dot(last, kv, p_b)
        cols = slice(g * Q_BLOCK, (g + 1) * Q_BLOCK)
        inv_l = 1.0 / acc_kv[HEAD_DIM:HEAD_DIM + 1, cols]
        o_ref[0, :, h * HEAD_DIM:(h + 1) * HEAD_DIM] = (acc_kv[0:HEAD_DIM, cols] * inv_l).T


def _prompt_attention(qi, wit, q, ki_bf, k_bf, vt):
    B, T, _ = q.shape
    nq = T // Q_BLOCK
    n_sel = min(TOPK_MAX, T // 4)
    kvw = N_KV_HEADS * HEAD_DIM
    assert T % KEY_STEP == 0 and KEY_STEP % Q_BLOCK == 0
    count_step = next(c for c in (COUNT_STEP, COUNT_STEP // 2, KEY_STEP) if T % c == 0)
    gq = GROUP * Q_BLOCK
    return pl.pallas_call(
        functools.partial(_prompt_attn_kernel, n_sel=n_sel, count_step=count_step),
        grid=(B, nq),
        in_specs=[pl.BlockSpec((1, Q_BLOCK, N_IDX_HEADS * IDX_DIM), lambda b, j: (b, j, 0)),
                  pl.BlockSpec((1, N_IDX_HEADS, Q_BLOCK), lambda b, j: (b, 0, j)),
                  pl.BlockSpec((1, Q_BLOCK, D_MODEL), lambda b, j: (b, j, 0)),
                  pl.BlockSpec((1, T, IDX_DIM), lambda b, j: (b, 0, 0)),
                  pl.BlockSpec((1, T, kvw), lambda b, j: (b, 0, 0)),
                  pl.BlockSpec((1, T // KEY_STEP, N_KV_HEADS * VT_ROWS, KEY_STEP), lambda b, j: (b, 0, 0, 0))],
        out_specs=pl.BlockSpec((1, Q_BLOCK, D_MODEL), lambda b, j: (b, j, 0)),
        out_shape=jax.ShapeDtypeStruct((B, T, D_MODEL), F32),
        scratch_shapes=[pltpu.VMEM((T, Q_BLOCK), F32), pltpu.VMEM((T, Q_BLOCK), F32),
                        pltpu.VMEM((KEY_STEP, N_IDX_HEADS * Q_BLOCK), F32),
                        pltpu.VMEM((KEY_STEP, N_IDX_HEADS * Q_BLOCK), F32),
                        pltpu.VMEM((N_KV_HEADS, KEY_STEP, gq), F32), pltpu.VMEM((N_KV_HEADS, KEY_STEP, gq), F32),
                        pltpu.VMEM((N_KV_HEADS, KEY_STEP, gq), BF16), pltpu.VMEM((N_KV_HEADS, KEY_STEP, gq), BF16),
                        pltpu.VMEM((N_KV_HEADS, VT_ROWS, gq), F32)],
        compiler_params=pltpu.CompilerParams(dimension_semantics=("arbitrary", "arbitrary"),
                                             vmem_limit_bytes=VMEM_LIMIT),
        name="prompt_attention",
    )(qi, wit, q, ki_bf, k_bf, vt)


def _sample_score_kernel(pt_ref, qi_ref, wcol_ref, kin_ref, *rest, n_pages, ts, n_seq):
    page_refs, s_ref = rest[:n_seq * n_pages], rest[n_seq * n_pages]
    lane = lax.broadcasted_iota(jnp.int32, (ts, PAGE_SIZE), 1)
    row = lax.broadcasted_iota(jnp.int32, (ts, PAGE_SIZE), 0)
    for q in range(n_seq):
        qi = qi_ref[q].astype(F32)
        q_all = jnp.concatenate([qi[:, h * IDX_DIM:(h + 1) * IDX_DIM] for h in range(N_IDX_HEADS)],
                                axis=0).astype(BF16)
        wcol = wcol_ref[q]

        def score(dots, wcol=wcol):
            r = jnp.maximum(dots, 0.0) * wcol
            acc = r[0:ts]
            for h in range(1, N_IDX_HEADS):
                acc = acc + r[h * ts:(h + 1) * ts]
            return acc

        past = jnp.concatenate([page_refs[q * n_pages + p][0] for p in range(n_pages)], axis=1).astype(BF16)
        s_ref[q, :, 0:n_pages * PAGE_SIZE] = score(_dot(q_all, past))
        new_keys = jnp.concatenate([kin_ref[q], jnp.zeros((PAGE_SIZE - ts, IDX_DIM), F32)], axis=0).astype(BF16)
        s_ref[q, :, n_pages * PAGE_SIZE:(n_pages + 1) * PAGE_SIZE] = jnp.where(
            lane <= row, score(_dot_nt(q_all, new_keys)), NEG_INF)


def _sample_scores(page_table, qi, wcol, ki_bf, cache_kidx):
    nb, ts, _ = qi.shape
    n_pages = page_table.shape[1]
    n_seq = SCORE_SEQS if nb % SCORE_SEQS == 0 else 1

    def page_spec(q, p):
        return pl.BlockSpec((1, IDX_DIM, PAGE_SIZE), lambda b, pt: (pt[b * n_seq + q, p], 0, 0))

    grid_spec = pltpu.PrefetchScalarGridSpec(
        num_scalar_prefetch=1,
        grid=(nb // n_seq,),
        in_specs=[pl.BlockSpec((n_seq, ts, N_IDX_HEADS * IDX_DIM), lambda b, pt: (b, 0, 0)),
                  pl.BlockSpec((n_seq, N_IDX_HEADS * ts, 1), lambda b, pt: (b, 0, 0)),
                  pl.BlockSpec((n_seq, ts, IDX_DIM), lambda b, pt: (b, 0, 0))]
                 + [page_spec(q, p) for q in range(n_seq) for p in range(n_pages)],
        out_specs=pl.BlockSpec((n_seq, ts, (n_pages + 1) * PAGE_SIZE), lambda b, pt: (b, 0, 0)),
    )
    return pl.pallas_call(
        functools.partial(_sample_score_kernel, n_pages=n_pages, ts=ts, n_seq=n_seq),
        grid_spec=grid_spec,
        out_shape=jax.ShapeDtypeStruct((nb, ts, (n_pages + 1) * PAGE_SIZE), F32),
        compiler_params=pltpu.CompilerParams(dimension_semantics=("arbitrary",), vmem_limit_bytes=VMEM_LIMIT),
        name="sample_scores",
    )(page_table, qi, wcol, ki_bf, *([cache_kidx] * (n_seq * n_pages)))


def _sample_select_kernel(s_ref, b_ref, sc_scr, *, n_sel, n_chunks):
    rows = s_ref.shape[0]

    def cols(c):
        return slice(c * LANES, (c + 1) * LANES)

    def counter(ref):
        def count_ge(t):
            tf = _key_value(t)
            acc = jnp.where(ref[:, cols(0)] >= tf, 1.0, 0.0)
            for c in range(1, n_chunks):
                acc = acc + jnp.where(ref[:, cols(c)] >= tf, 1.0, 0.0)
            return jnp.sum(acc, axis=1, keepdims=True)
        return count_ge

    kk = jnp.full((rows, 1), float(n_sel), F32)
    n_all = counter(s_ref)(jnp.full((rows, 1), KEY_LOWEST, jnp.int32))
    lo, hi, n_lo, _ = _search_keys(counter(s_ref), kk, n_all)
    any_tie = jnp.max(jnp.where(n_lo > kk, 1.0, 0.0)) > 0.5

    @pl.when(jnp.logical_not(any_tie))
    def _():
        for c in range(n_chunks):
            b_ref[:, cols(c)] = jnp.where(s_ref[:, cols(c)] >= lo, 0.0, NEG_INF)

    @pl.when(any_tie)
    def _():
        for c in range(n_chunks):
            sc_scr[:, cols(c)] = _spread_ties(s_ref[:, cols(c)], lo, hi)
        lo2, hi2, _, n_hi2 = _search_keys(counter(sc_scr), kk, n_lo)
        tri = jnp.where(lax.broadcasted_iota(jnp.int32, (LANES, LANES), 0)
                        <= lax.broadcasted_iota(jnp.int32, (LANES, LANES), 1), 1.0, 0.0).astype(BF16)
        run = jnp.zeros((rows, 1), F32)
        for c in range(n_chunks):
            sc = sc_scr[:, cols(c)]
            above = sc >= hi2
            elig = (sc >= lo2) & jnp.logical_not(above)
            rank = run + _dot(jnp.where(elig, 1.0, 0.0).astype(BF16), tri)
            b_ref[:, cols(c)] = jnp.where(above | (elig & (rank <= kk - n_hi2)), 0.0, NEG_INF)
            run = rank[:, LANES - 1:LANES]


def _sample_select(scores, n_sel, rows_per_step):
    rows, width = scores.shape
    return pl.pallas_call(
        functools.partial(_sample_select_kernel, n_sel=n_sel, n_chunks=width // LANES),
        grid=(rows // rows_per_step,),
        in_specs=[pl.BlockSpec((rows_per_step, width), lambda i: (i, 0))],
        out_specs=pl.BlockSpec((rows_per_step, width), lambda i: (i, 0)),
        out_shape=jax.ShapeDtypeStruct((rows, width), F32),
        scratch_shapes=[pltpu.VMEM((rows_per_step, width), F32)],
        compiler_params=pltpu.CompilerParams(dimension_semantics=("arbitrary",), vmem_limit_bytes=VMEM_LIMIT),
        name="sample_select",
    )(scores)


def _sample_attn_kernel(pt_ref, q_ref, b_ref, kn_ref, vn_ref, *rest, n_pages, ts, n_seq):
    k_refs, v_refs, o_ref = rest[:n_seq * n_pages], rest[n_seq * n_pages:2 * n_seq * n_pages], rest[2 * n_seq * n_pages]
    pad = jnp.zeros((PAGE_SIZE - ts, HEAD_DIM), F32)

    def head_rows(refs, new_ref, u, kv):
        tiles = [r[0, pl.ds(kv, PAGE_SIZE, stride=N_KV_HEADS), :] for r in refs[u * n_pages:(u + 1) * n_pages]]
        tiles += [new_ref[u, pl.ds(kv, ts, stride=N_KV_HEADS), :], pad]
        return jnp.concatenate(tiles, axis=0).astype(BF16)

    units = [(u, kv) for u in range(n_seq) for kv in range(N_KV_HEADS)]
    s = []
    for u, kv in units:
        q = q_ref[u].astype(F32)
        qg = jnp.concatenate([q[:, (kv * GROUP + g) * HEAD_DIM:(kv * GROUP + g + 1) * HEAD_DIM]
                              for g in range(GROUP)], axis=0).astype(BF16)
        bias = jnp.concatenate([b_ref[u]] * GROUP, axis=0)
        s.append(_dot_nt(qg, head_rows(k_refs, kn_ref, u, kv)) + bias)
    p_ = [jnp.exp2(x - jnp.max(x, axis=1, keepdims=True)) for x in s]
    for (u, kv), pr in zip(units, p_):
        o = _dot(pr.astype(BF16), head_rows(v_refs, vn_ref, u, kv)) / jnp.sum(pr, axis=1, keepdims=True)
        for g in range(GROUP):
            hh = kv * GROUP + g
            o_ref[u, :, hh * HEAD_DIM:(hh + 1) * HEAD_DIM] = o[g * ts:(g + 1) * ts, :]


def _sample_attention(page_table, q, bias, k_new, v_new, cache_k, cache_v):
    nb, ts, _ = q.shape
    n_pages = page_table.shape[1]
    n_seq = ATTN_SEQS if nb % ATTN_SEQS == 0 else 1

    def page_spec(u, p):
        return pl.BlockSpec((1, PAGE_SIZE * N_KV_HEADS, HEAD_DIM), lambda b, pt: (pt[b * n_seq + u, p], 0, 0))

    pages = [page_spec(u, p) for u in range(n_seq) for p in range(n_pages)]
    grid_spec = pltpu.PrefetchScalarGridSpec(
        num_scalar_prefetch=1,
        grid=(nb // n_seq,),
        in_specs=[pl.BlockSpec((n_seq, ts, D_MODEL), lambda b, pt: (b, 0, 0)),
                  pl.BlockSpec((n_seq, ts, (n_pages + 1) * PAGE_SIZE), lambda b, pt: (b, 0, 0)),
                  pl.BlockSpec((n_seq, ts * N_KV_HEADS, HEAD_DIM), lambda b, pt: (b, 0, 0)),
                  pl.BlockSpec((n_seq, ts * N_KV_HEADS, HEAD_DIM), lambda b, pt: (b, 0, 0))] + pages * 2,
        out_specs=pl.BlockSpec((n_seq, ts, D_MODEL), lambda b, pt: (b, 0, 0)),
    )
    return pl.pallas_call(
        functools.partial(_sample_attn_kernel, n_pages=n_pages, ts=ts, n_seq=n_seq),
        grid_spec=grid_spec,
        out_shape=jax.ShapeDtypeStruct((nb, ts, D_MODEL), F32),
        compiler_params=pltpu.CompilerParams(dimension_semantics=("arbitrary",), vmem_limit_bytes=VMEM_LIMIT),
        name="sample_attention",
    )(page_table, q, bias, k_new, v_new, *([cache_k] * (n_seq * n_pages)), *([cache_v] * (n_seq * n_pages)))


def _ffn_kernel(x_ref, oa_ref, ga_ref, cm_ref, p_ref, wout_ref, gffn_ref, wg_ref, wu_ref, wd_ref,
                gple_ref, wple_ref, wpg_ref, gfin_ref, y_ref, x1_scr, h_scr, acc_scr):
    f = pl.program_id(1)

    @pl.when(f == 0)
    def _():
        merged = ga_ref[...] * oa_ref[...] + cm_ref[...]
        x1 = x_ref[...] + _dot(merged.astype(BF16), wout_ref[...])
        x1_scr[...] = x1
        h_scr[...] = _rmsnorm(x1, gffn_ref[...]).astype(BF16)
        acc_scr[...] = jnp.zeros_like(acc_scr)

    h = h_scr[...]
    g = _dot(h, wg_ref[...])
    u = _dot(h, wu_ref[...])
    acc_scr[...] += _dot((g * jax.nn.sigmoid(g) * u).astype(BF16), wd_ref[...])

    @pl.when(f == pl.num_programs(1) - 1)
    def _():
        x2 = x1_scr[...] + acc_scr[...]
        gate = jax.nn.sigmoid(_dot(_rmsnorm(x2, gple_ref[...]).astype(BF16), wpg_ref[...]))
        x3 = x2 + _dot(p_ref[...].astype(BF16), wple_ref[...]) * gate
        y_ref[...] = _rmsnorm(x3, gfin_ref[...])


def _ffn(x, oa, ga, cm, p, wout, gffn, wg, wu, wd, gple, wple, wpg, gfin, *, tm, tf):
    n = x.shape[0]
    tm = min(tm, n)
    tok = lambda w: pl.BlockSpec((tm, w), lambda i, f: (i, 0))
    vec = lambda a: a.reshape(1, D_MODEL)
    return pl.pallas_call(
        _ffn_kernel,
        grid=(n // tm, D_FF // tf),
        in_specs=[tok(D_MODEL), tok(D_MODEL), tok(D_MODEL), tok(D_MODEL), tok(D_PLE),
                  _const_spec((D_MODEL, D_MODEL)), _const_spec((1, D_MODEL)),
                  pl.BlockSpec((D_MODEL, tf), lambda i, f: (0, f)),
                  pl.BlockSpec((D_MODEL, tf), lambda i, f: (0, f)),
                  pl.BlockSpec((tf, D_MODEL), lambda i, f: (f, 0)),
                  _const_spec((1, D_MODEL)), _const_spec((D_PLE, D_MODEL)), _const_spec((D_MODEL, D_MODEL)),
                  _const_spec((1, D_MODEL))],
        out_specs=tok(D_MODEL),
        out_shape=jax.ShapeDtypeStruct((n, D_MODEL), F32),
        scratch_shapes=[pltpu.VMEM((tm, D_MODEL), F32), pltpu.VMEM((tm, D_MODEL), BF16),
                        pltpu.VMEM((tm, D_MODEL), F32)],
        compiler_params=pltpu.CompilerParams(dimension_semantics=("arbitrary", "arbitrary"),
                                             vmem_limit_bytes=VMEM_LIMIT),
        name="ffn",
    )(x, oa, ga, cm, p, wout, vec(gffn), wg, wu, wd, vec(gple), wple, wpg, vec(gfin))


def _split_w_in(w_in):
    n_att = N_HEADS * HEAD_DIM + 2 * N_KV_HEADS * HEAD_DIM + N_IDX_HEADS * IDX_DIM + IDX_DIM + N_IDX_HEADS
    return w_in[:, :W1_COLS].astype(BF16), w_in[:, n_att:].astype(BF16)


def kernel(x_prompt, x_sample, cache_k, cache_v, cache_kidx, state_conv, page_table, p_prompt, p_sample, norm_mix, w_in, conv_w, w_out, norm_ffn, w_gate_up, w_down, norm_ple, w_ple, w_ple_gate, norm_final):
    Bp, Tp, _ = x_prompt.shape
    Bs, Ts, _ = x_sample.shape
    n_pages = page_table.shape[1]
    past_len = n_pages * PAGE_SIZE
    n_phys = cache_k.shape[1]
    kvw = N_KV_HEADS * HEAD_DIM
    l = 0

    w1, w2 = _split_w_in(w_in[l])
    wout = w_out[l].astype(BF16)
    wg = w_gate_up[l][:, :D_FF].astype(BF16)
    wu = w_gate_up[l][:, D_FF:].astype(BF16)
    wd = w_down[l].astype(BF16)
    wple = w_ple[l].astype(BF16)
    wpg = w_ple_gate[l].astype(BF16)

    tab_p = _rope_table(jnp.arange(Tp))
    (q_p, k_p, v_p, kbf_p, vt_p, qi_p, ki_p, kibf_p, wit_p, ga_p, cm_p, cnew_p) = _inproj(
        x_prompt, norm_mix[l], w1, w2, conv_w[l], tab_p, None, mode="prompt", tm=min(INPROJ_ROWS, Tp))
    oa_p = _prompt_attention(qi_p, wit_p, q_p, kibf_p, kbf_p, vt_p)
    n_p = Bp * Tp
    y_p = _ffn(x_prompt.reshape(n_p, D_MODEL), oa_p.reshape(n_p, D_MODEL), ga_p.reshape(n_p, D_MODEL),
               cm_p.reshape(n_p, D_MODEL), p_prompt[l].reshape(n_p, D_PLE), wout, norm_ffn[l], wg, wu, wd,
               norm_ple[l], wple, wpg, norm_final, tm=512, tf=D_FF // 2)

    n_s = Bs * Ts
    tm_s = min(INPROJ_ROWS, n_s)
    tab_s = _rope_table(past_len + (jnp.arange(tm_s) % Ts))
    prev = jnp.concatenate([state_conv[l], jnp.zeros((Bs, Ts - (CONV_WIDTH - 1), D_CONV), F32)], axis=1)
    (q_s, k_s, v_s, qi_s, ki_s, kibf_s, wit_s, ga_s, cm_s, u_s) = _inproj(
        x_sample.reshape(n_s // tm_s, tm_s, D_MODEL), norm_mix[l], w1, w2, conv_w[l], tab_s,
        prev.reshape(n_s // tm_s, tm_s, D_CONV), mode="sample", tm=tm_s)
    wcol = wit_s.transpose(1, 0, 2).reshape(N_IDX_HEADS, Bs, Ts).transpose(1, 0, 2).reshape(Bs, N_IDX_HEADS * Ts, 1)
    scores = _sample_scores(page_table, qi_s.reshape(Bs, Ts, -1), wcol, ki_s.reshape(Bs, Ts, IDX_DIM),
                            jnp.swapaxes(cache_kidx[l], 1, 2))
    n_sel = min(TOPK_MAX, (past_len + Ts) // 4)
    bias = _sample_select(scores.reshape(n_s, -1), n_sel, 256)
    oa_s = _sample_attention(page_table, q_s.reshape(Bs, Ts, D_MODEL), bias.reshape(Bs, Ts, -1),
                             k_s.reshape(Bs, Ts * N_KV_HEADS, HEAD_DIM), v_s.reshape(Bs, Ts * N_KV_HEADS, HEAD_DIM),
                             cache_k[l].reshape(n_phys, PAGE_SIZE * N_KV_HEADS, HEAD_DIM),
                             cache_v[l].reshape(n_phys, PAGE_SIZE * N_KV_HEADS, HEAD_DIM))
    y_s = _ffn(x_sample.reshape(n_s, D_MODEL), oa_s.reshape(n_s, D_MODEL), ga_s.reshape(n_s, D_MODEL),
               cm_s.reshape(n_s, D_MODEL), p_sample[l].reshape(n_s, D_PLE), wout, norm_ffn[l], wg, wu, wd,
               norm_ple[l], wple, wpg, norm_final, tm=512, tf=D_FF // 2)

    return (y_p.reshape(Bp, Tp, D_MODEL), y_s.reshape(Bs, Ts, D_MODEL),
            k_p.reshape(1, Bp, Tp, N_KV_HEADS, HEAD_DIM), v_p.reshape(1, Bp, Tp, N_KV_HEADS, HEAD_DIM),
            ki_p.reshape(1, Bp, Tp, IDX_DIM), cnew_p.reshape(1, Bp, CONV_WIDTH - 1, D_CONV),
            k_s.reshape(1, Bs, Ts, N_KV_HEADS, HEAD_DIM), v_s.reshape(1, Bs, Ts, N_KV_HEADS, HEAD_DIM),
            ki_s.reshape(1, Bs, Ts, IDX_DIM),
            u_s.reshape(Bs, Ts, D_CONV)[:, Ts - (CONV_WIDTH - 1):, :].reshape(1, Bs, CONV_WIDTH - 1, D_CONV))
```

```python
import functools

import jax
import jax.numpy as jnp
from jax import lax
from jax.experimental import pallas as pl
from jax.experimental.pallas import tpu as pltpu

D_MODEL = 1024
N_HEADS = 8
N_KV_HEADS = 2
GROUP = N_HEADS // N_KV_HEADS
HEAD_DIM = 128
N_IDX_HEADS = 8
IDX_DIM = 64
IDX_SCALE = (N_IDX_HEADS * IDX_DIM) ** -0.5
QK_SCALE = HEAD_DIM ** -0.5
TOPK_MAX = 256
D_CONV = D_MODEL
CONV_WIDTH = 3
D_FF = 2816
D_PLE = 256
PAGE_SIZE = 128
ROPE_THETA = 10000.0
EPS = 1e-6

LANES = 128
SUBLANES = 8
KEY_STEP = 256
Q_BLOCK = 128
COUNT_STEP = 512
COUNT_LANES = 8
VT_ROWS = HEAD_DIM + 16
CONV_SLAB = 256
INPROJ_ROWS = 512
SCORE_SEQS = 4
ATTN_SEQS = 2
VMEM_LIMIT = 56 * 1024 * 1024

W1_COLS = N_HEADS * HEAD_DIM + 2 * N_KV_HEADS * HEAD_DIM + N_IDX_HEADS * IDX_DIM + LANES
W2_COLS = 5 * D_MODEL

F32 = jnp.float32
BF16 = jnp.bfloat16
NEG_INF = float("-inf")
LOWEST = -3.4028234663852886e38
TINY = 1.1754943508222875e-38
KEY_BITS = 32
KEY_LOWEST = -0x7F800000
KEY_INF = 0x7F800000
SEARCH_BLIND = 24
M_FLOOR = -1e30
LOG2E = 1.4426950408889634


def _dot(a, b):
    return jnp.dot(a, b, preferred_element_type=F32)


def _dot_nt(a, b):
    return lax.dot_general(a, b, (((1,), (1,)), ((), ())), preferred_element_type=F32)


def _rmsnorm(x, g):
    var = jnp.mean(x * x, axis=-1, keepdims=True)
    return (x * lax.rsqrt(var + EPS)) * g


def _rope_table(pos):
    def tab(half):
        freqs = ROPE_THETA ** (-jnp.arange(half, dtype=F32) / half)
        ang = pos.astype(F32)[:, None] * freqs[None, :]
        return [jnp.cos(ang), jnp.sin(ang)]
    return jnp.concatenate(tab(HEAD_DIM // 2) + tab(IDX_DIM // 2), axis=1)


def _rope128(x, cos, sin):
    return x * cos + pltpu.roll(x, HEAD_DIM // 2, axis=1) * sin


def _rope64(x, cos, sin, first_half):
    partner = jnp.where(first_half, pltpu.roll(x, LANES - IDX_DIM // 2, axis=1), pltpu.roll(x, IDX_DIM // 2, axis=1))
    return x * cos + partner * sin


def _inproj_kernel(*refs, mode, tm):
    if mode == "prompt":
        (x_ref, g_ref, w1_ref, w2_ref, cw_ref, tab_ref,
         q_ref, k_ref, v_ref, kbf_ref, vt_ref, qi_ref, ki_ref, kibf_ref, wit_ref, ga_ref, cm_ref, cnew_ref,
         carry_ref) = refs
    else:
        (x_ref, g_ref, w1_ref, w2_ref, cw_ref, tab_ref, prev_ref,
         q_ref, k_ref, v_ref, qi_ref, ki_ref, kibf_ref, wit_ref, ga_ref, cm_ref, u_ref) = refs

    x = x_ref[0]
    h = _rmsnorm(x, g_ref[...]).astype(BF16)
    z1 = _dot(h, w1_ref[...])
    tab = tab_ref[...]
    h128, h64 = HEAD_DIM // 2, IDX_DIM // 2
    c, sn = tab[:, 0:h128], tab[:, h128:2 * h128]
    cos128, sin128 = jnp.concatenate([c, c], axis=1), jnp.concatenate([-sn, sn], axis=1)
    c, sn = tab[:, 2 * h128:2 * h128 + h64], tab[:, 2 * h128 + h64:2 * h128 + 2 * h64]
    cos64 = jnp.concatenate([c, c] * (LANES // IDX_DIM), axis=1)
    sin64 = jnp.concatenate([-sn, sn] * (LANES // IDX_DIM), axis=1)
    lane = lax.broadcasted_iota(jnp.int32, (tm, LANES), 1)
    first_half = (lane % IDX_DIM) < (IDX_DIM // 2)

    off = 0
    for hh in range(N_HEADS):
        sl = z1[:, off:off + HEAD_DIM]
        q_ref[0, :, hh * HEAD_DIM:(hh + 1) * HEAD_DIM] = (_rope128(sl, cos128, sin128) * (QK_SCALE * LOG2E)).astype(BF16)
        off += HEAD_DIM
    for hh in range(N_KV_HEADS):
        kr = _rope128(z1[:, off:off + HEAD_DIM], cos128, sin128)
        k_ref[0, pl.ds(hh, tm, stride=N_KV_HEADS), :] = kr
        if mode == "prompt":
            kbf_ref[0, :, hh * HEAD_DIM:(hh + 1) * HEAD_DIM] = kr.astype(BF16)
        off += HEAD_DIM
    v = z1[:, off:off + N_KV_HEADS * HEAD_DIM]
    for hh in range(N_KV_HEADS):
        v_ref[0, pl.ds(hh, tm, stride=N_KV_HEADS), :] = v[:, hh * HEAD_DIM:(hh + 1) * HEAD_DIM]
    if mode == "prompt":
        for c in range(tm // KEY_STEP):
            vt = v[c * KEY_STEP:(c + 1) * KEY_STEP, :].T
            ones_row = jnp.where(lax.broadcasted_iota(jnp.int32, (VT_ROWS - HEAD_DIM, KEY_STEP), 0) == 0, 1.0, 0.0)
            for hh in range(N_KV_HEADS):
                vt_ref[0, c, hh * VT_ROWS:(hh + 1) * VT_ROWS, :] = jnp.concatenate(
                    [vt[hh * HEAD_DIM:(hh + 1) * HEAD_DIM, :], ones_row], axis=0).astype(BF16)
    off += N_KV_HEADS * HEAD_DIM
    for hh in range(N_IDX_HEADS * IDX_DIM // LANES):
        sl = z1[:, off:off + LANES]
        qi_ref[0, :, hh * LANES:(hh + 1) * LANES] = _rope64(sl, cos64, sin64, first_half).astype(BF16)
        off += LANES
    kiw = z1[:, off:off + LANES]
    kir = _rope64(kiw, cos64, sin64, first_half)[:, 0:IDX_DIM]
    ki_ref[0] = kir
    kibf_ref[0] = kir.astype(BF16)
    wit_ref[0] = kiw.T[IDX_DIM:IDX_DIM + N_IDX_HEADS, :] * IDX_SCALE

    if mode == "prompt":
        @pl.when(pl.program_id(1) == 0)
        def _():
            carry_ref[...] = jnp.zeros_like(carry_ref)

    row = lax.broadcasted_iota(jnp.int32, (tm, CONV_SLAB), 0)
    for c in range(D_CONV // CONV_SLAB):
        cols = slice(c * CONV_SLAB, (c + 1) * CONV_SLAB)
        bg, cg, xc, ga, gb = (_dot(h, w2_ref[:, k * D_MODEL + c * CONV_SLAB:k * D_MODEL + (c + 1) * CONV_SLAB])
                              for k in range(5))
        u = cg * xc
        r1 = pltpu.roll(u, 1, axis=0)
        r2 = pltpu.roll(u, 2, axis=0)
        if mode == "prompt":
            c0 = carry_ref[0:1, cols]
            c1 = carry_ref[1:2, cols]
            um1 = jnp.where(row == 0, c1, r1)
            um2 = jnp.where(row == 0, c0, jnp.where(row == 1, c1, r2))
            carry_ref[0:2, cols] = u[tm - 2:tm, :]
            cnew_ref[0, :, cols] = u[tm - 2:tm, :]
        else:
            prev = prev_ref[0, :, cols]
            seq_row = row % SUBLANES
            um1 = jnp.where(seq_row == 0, pltpu.roll(prev, tm - 1, axis=0), r1)
            um2 = jnp.where(seq_row < 2, prev, r2)
            u_ref[0, :, cols] = u
        cw = cw_ref[:, cols]
        conv = cw[0:1, :] * um2 + cw[1:2, :] * um1 + cw[2:3, :] * u
        ga_ref[0, :, cols] = jax.nn.sigmoid(ga)
        cm_ref[0, :, cols] = jax.nn.sigmoid(gb) * (bg * conv)


def _const_spec(shape):
    nd = len(shape)
    return pl.BlockSpec(shape, lambda *_: (0,) * nd, pipeline_mode=pl.Buffered(1))


def _inproj(x, norm_g, w1, w2, conv_w, tab, prev, *, mode, tm):
    B, T, _ = x.shape
    assert T % tm == 0 and tm % KEY_STEP == 0 and D_CONV % CONV_SLAB == 0
    nt = T // tm
    tok = lambda w: pl.BlockSpec((1, tm, w), lambda b, t: (b, t, 0))
    in_specs = [tok(D_MODEL), _const_spec((1, D_MODEL)), _const_spec((D_MODEL, W1_COLS)),
                _const_spec((D_MODEL, W2_COLS)), _const_spec((CONV_WIDTH, D_CONV)),
                pl.BlockSpec((tm, HEAD_DIM + IDX_DIM), lambda b, t: (t, 0))]
    args = [x, norm_g.reshape(1, D_MODEL), w1, w2, conv_w, tab]
    kvw = N_KV_HEADS * HEAD_DIM
    qiw = N_IDX_HEADS * IDX_DIM
    wit_spec = pl.BlockSpec((1, N_IDX_HEADS, tm), lambda b, t: (b, 0, t))
    kv_spec = pl.BlockSpec((1, tm * N_KV_HEADS, HEAD_DIM), lambda b, t: (b, t, 0))
    if mode == "prompt":
        out_shape = [
            jax.ShapeDtypeStruct((B, T, D_MODEL), BF16),
            jax.ShapeDtypeStruct((B, T * N_KV_HEADS, HEAD_DIM), F32),
            jax.ShapeDtypeStruct((B, T * N_KV_HEADS, HEAD_DIM), F32),
            jax.ShapeDtypeStruct((B, T, kvw), BF16),
            jax.ShapeDtypeStruct((B, T // KEY_STEP, N_KV_HEADS * VT_ROWS, KEY_STEP), BF16),
            jax.ShapeDtypeStruct((B, T, qiw), BF16),
            jax.ShapeDtypeStruct((B, T, IDX_DIM), F32),
            jax.ShapeDtypeStruct((B, T, IDX_DIM), BF16),
            jax.ShapeDtypeStruct((B, N_IDX_HEADS, T), F32),
            jax.ShapeDtypeStruct((B, T, D_MODEL), F32),
            jax.ShapeDtypeStruct((B, T, D_MODEL), F32),
            jax.ShapeDtypeStruct((B, CONV_WIDTH - 1, D_CONV), F32),
        ]
        out_specs = [tok(D_MODEL), kv_spec, kv_spec, tok(kvw),
                     pl.BlockSpec((1, tm // KEY_STEP, N_KV_HEADS * VT_ROWS, KEY_STEP), lambda b, t: (b, t, 0, 0)),
                     tok(qiw), tok(IDX_DIM), tok(IDX_DIM), wit_spec, tok(D_MODEL), tok(D_MODEL),
                     pl.BlockSpec((1, CONV_WIDTH - 1, D_CONV), lambda b, t: (b, 0, 0))]
        scratch = [pltpu.VMEM((SUBLANES, D_CONV), F32)]
    else:
        in_specs.append(tok(D_CONV))
        args.append(prev)
        out_shape = [
            jax.ShapeDtypeStruct((B, T, D_MODEL), BF16),
            jax.ShapeDtypeStruct((B, T * N_KV_HEADS, HEAD_DIM), F32),
            jax.ShapeDtypeStruct((B, T * N_KV_HEADS, HEAD_DIM), F32),
            jax.ShapeDtypeStruct((B, T, qiw), BF16),
            jax.ShapeDtypeStruct((B, T, IDX_DIM), F32),
            jax.ShapeDtypeStruct((B, T, IDX_DIM), BF16),
            jax.ShapeDtypeStruct((B, N_IDX_HEADS, T), F32),
            jax.ShapeDtypeStruct((B, T, D_MODEL), F32),
            jax.ShapeDtypeStruct((B, T, D_MODEL), F32),
            jax.ShapeDtypeStruct((B, T, D_CONV), F32),
        ]
        out_specs = [tok(D_MODEL), kv_spec, kv_spec, tok(qiw), tok(IDX_DIM), tok(IDX_DIM), wit_spec,
                     tok(D_MODEL), tok(D_MODEL), tok(D_CONV)]
        scratch = []
    return pl.pallas_call(
        functools.partial(_inproj_kernel, mode=mode, tm=tm),
        grid=(B, nt),
        in_specs=in_specs,
        out_specs=out_specs,
        out_shape=out_shape,
        scratch_shapes=scratch,
        compiler_params=pltpu.CompilerParams(dimension_semantics=("arbitrary", "arbitrary"),
                                             vmem_limit_bytes=VMEM_LIMIT),
        name="inproj_" + mode,
    )(*args)


def _key_value(key):
    return lax.bitcast_convert_type(key ^ ((key >> 31) & 0x7FFFFFFF), F32)


def _search_keys(count_ge, kk, n_all):
    lo0 = jnp.full(kk.shape, KEY_LOWEST, jnp.int32)
    hi0 = jnp.full(kk.shape, KEY_INF, jnp.int32)
    n_lo0 = n_all
    def body(_, c):
        lo, hi, n_lo, n_hi, done = c
        mid = (lo & hi) + ((lo ^ hi) >> 1)
        stuck = mid == lo
        n_mid = count_ge(mid)
        upd = (done < 0.5) & jnp.logical_not(stuck)
        go_lo = n_mid >= kk
        up_lo = upd & go_lo
        lo = jnp.where(up_lo, mid, lo)
        n_lo = jnp.where(up_lo, n_mid, n_lo)
        up_hi = upd & jnp.logical_not(go_lo)
        hi = jnp.where(up_hi, mid, hi)
        n_hi = jnp.where(up_hi, n_mid, n_hi)
        done = jnp.where(stuck | (upd & (n_mid == kk)), 1.0, done)
        return lo, hi, n_lo, n_hi, done

    done0 = jnp.where(n_lo0 <= kk, 1.0, 0.0)
    state = (lo0, hi0, n_lo0, jnp.zeros_like(kk), done0)
    state = lax.fori_loop(0, SEARCH_BLIND, body, state)
    lo, hi, n_lo, n_hi, _ = lax.while_loop(lambda c: jnp.min(c[4]) < 0.5, lambda c: body(0, body(0, c)), state)
    return _key_value(lo), _key_value(hi), n_lo, n_hi


def _resolve_ties(count_ge, kk, n_lo, n_hi, tied, top):
    def by_position():
        return jnp.zeros_like(kk), jnp.where(tied, TINY, -LOWEST), n_hi

    def by_value():
        lo2, hi2, _, n_hi2 = _search_keys(count_ge, kk, n_lo)
        return lo2, hi2, n_hi2

    return lax.cond(jnp.max(jnp.where(tied & (top > 0.0), 1.0, 0.0)) > 0.5, by_value, by_position)


def _spread_ties(sc, lo, hi):
    return jnp.where(sc >= hi, -LOWEST, jnp.where(sc >= lo, sc - lo, NEG_INF))


def _fold_rows(x, op):
    return op(x.reshape(x.shape[0] // SUBLANES, SUBLANES, x.shape[1]), axis=0)


def _prompt_attn_kernel(qi_ref, wit_ref, q_ref, ki_ref, k_ref, vt_ref, o_ref,
                        sc_scr, b_scr, d_a, d_b, s_a, s_b, p_a, p_b, acc_scr, *, n_sel, count_step):
    j = pl.program_id(1)
    per_step = KEY_STEP // Q_BLOCK
    n_steps = (j + per_step) // per_step
    n_pairs = n_steps // 2
    odd = n_steps % 2 == 1
    last = n_steps - 1
    per_count = count_step // KEY_STEP
    n_count = (n_steps + per_count - 1) // per_count
    qi = qi_ref[0]
    q_stack = jnp.concatenate([qi[:, h * IDX_DIM:(h + 1) * IDX_DIM] for h in range(N_IDX_HEADS)], axis=0)
    wit = wit_ref[0]
    q = q_ref[0]
    q_grp = [jnp.concatenate([q[:, (kv * GROUP + g) * HEAD_DIM:(kv * GROUP + g + 1) * HEAD_DIM]
                              for g in range(GROUP)], axis=0) for kv in range(N_KV_HEADS)]
    key_pos = lax.broadcasted_iota(jnp.int32, (KEY_STEP, Q_BLOCK), 0)
    q_pos = lax.broadcasted_iota(jnp.int32, (KEY_STEP, Q_BLOCK), 1) + j * Q_BLOCK

    def step(i):
        return pl.ds(pl.multiple_of(i * KEY_STEP, KEY_STEP), KEY_STEP)

    def idx_dots(i):
        return _dot_nt(ki_ref[0, step(i), :], q_stack)

    def score_trip(i, d_cur, d_nxt):
        d_nxt[...] = idx_dots(jnp.minimum(i + 1, last))
        acc = wit[0:1, :] * jnp.maximum(d_cur[:, 0:Q_BLOCK], 0.0)
        for h in range(1, N_IDX_HEADS):
            acc = acc + wit[h:h + 1, :] * jnp.maximum(d_cur[:, h * Q_BLOCK:(h + 1) * Q_BLOCK], 0.0)
        sc_scr[step(i), :] = jnp.where(key_pos + i * KEY_STEP <= q_pos, acc, NEG_INF)

    def logits(i, kv):
        return _dot_nt(k_ref[0, step(i), kv * HEAD_DIM:(kv + 1) * HEAD_DIM], q_grp[kv])

    d_a[...] = idx_dots(0)
    for kv in range(N_KV_HEADS):
        s_a[kv] = logits(0, kv)

    def score_pair(pi, c):
        score_trip(2 * pi, d_a, d_b)
        score_trip(2 * pi + 1, d_b, d_a)
        return c

    lax.fori_loop(0, n_pairs, score_pair, 0)

    @pl.when(odd)
    def _():
        score_trip(last, d_a, d_b)

    def fill(i, c):
        sc_scr[step(i), :] = jnp.full((KEY_STEP, Q_BLOCK), NEG_INF, F32)
        return c

    lax.fori_loop(n_steps, n_count * per_count, fill, 0)

    def count_ge(t):
        tb = jnp.broadcast_to(_key_value(t), (SUBLANES, Q_BLOCK))

        def body(i, accs):
            accs = list(accs)
            base = pl.multiple_of(i * count_step, count_step)
            for r in range(count_step // SUBLANES):
                sc = sc_scr[pl.ds(base + r * SUBLANES, SUBLANES), :]
                accs[r % COUNT_LANES] = accs[r % COUNT_LANES] + jnp.where(sc >= tb, 1.0, 0.0)
            return tuple(accs)

        accs = lax.fori_loop(0, n_count, body, tuple(jnp.zeros((SUBLANES, Q_BLOCK), F32) for _ in range(COUNT_LANES)))
        return jnp.sum(functools.reduce(lambda a, b: a + b, accs), axis=0, keepdims=True)

    n_valid = (j * Q_BLOCK + 1 + lax.broadcasted_iota(jnp.int32, (1, Q_BLOCK), 1)).astype(F32)
    kk = jnp.minimum(n_valid, float(n_sel))
    lo, hi, n_lo, n_hi = lax.cond((j + 1) * Q_BLOCK <= n_sel,
                                  lambda: (jnp.full_like(kk, LOWEST), jnp.full_like(kk, jnp.inf), n_valid,
                                           jnp.zeros_like(kk)),
                                  lambda: _search_keys(count_ge, kk, n_valid))

    def mask_plain(i, c):
        b_scr[step(i), :] = jnp.where(sc_scr[step(i), :] >= lo, 0.0, NEG_INF)
        return c

    def spread(i, top):
        sp = _spread_ties(sc_scr[step(i), :], lo, hi)
        sc_scr[step(i), :] = sp
        return jnp.maximum(top, _fold_rows(jnp.where(sp < -LOWEST, sp, NEG_INF), jnp.max))

    tied = n_lo > kk
    any_tie = jnp.max(jnp.where(tied, 1.0, 0.0)) > 0.5

    @pl.when(any_tie)
    def _():
        top = lax.fori_loop(0, n_steps, spread, jnp.full((SUBLANES, Q_BLOCK), NEG_INF, F32))
        lo2, hi2, n_hi2 = _resolve_ties(count_ge, kk, n_lo, n_hi, tied, jnp.max(top, axis=0, keepdims=True))
        tri = jnp.where(lax.broadcasted_iota(jnp.int32, (KEY_STEP, KEY_STEP), 1)
                        <= lax.broadcasted_iota(jnp.int32, (KEY_STEP, KEY_STEP), 0), 1.0, 0.0).astype(BF16)

        def mask_ties(i, run):
            sc = sc_scr[step(i), :]
            above = sc >= hi2
            elig = (sc >= lo2) & jnp.logical_not(above)
            rank = run + _dot(tri, jnp.where(elig, 1.0, 0.0).astype(BF16))
            b_scr[step(i), :] = jnp.where(above | (elig & (rank <= kk - n_hi2)), 0.0, NEG_INF)
            return rank[KEY_STEP - 1:KEY_STEP, :]

        lax.fori_loop(0, n_steps, mask_ties, jnp.zeros((1, Q_BLOCK), F32))

    @pl.when(jnp.logical_not(any_tie))
    def _():
        lax.fori_loop(0, n_steps, mask_plain, 0)

    def pv_dot(i, kv, p_ref):
        return _dot(vt_ref[0, i, kv * VT_ROWS:(kv + 1) * VT_ROWS, :], p_ref[kv])

    def attn_trip(i, s_cur, s_nxt, p_cur, p_prv, m):
        pv = [pv_dot(jnp.maximum(i - 1, 0), kv, p_prv) for kv in range(N_KV_HEADS)]
        for kv in range(N_KV_HEADS):
            s_nxt[kv] = logits(jnp.minimum(i + 1, last), kv)
        b = b_scr[step(i), :]
        m_rows = []
        for kv in range(N_KV_HEADS):
            ps, alphas = [], []
            for g in range(GROUP):
                h = kv * GROUP + g
                sg = s_cur[kv, :, g * Q_BLOCK:(g + 1) * Q_BLOCK] + b
                m_old = m[h:h + 1, :]
                m_new = jnp.maximum(m_old, jnp.max(_fold_rows(sg, jnp.max), axis=0, keepdims=True))
                ps.append(jnp.exp2(sg - m_new).astype(BF16))
                alphas.append(jnp.exp2(m_old - m_new))
                m_rows.append(m_new)
            p_cur[kv] = jnp.concatenate(ps, axis=1)
            acc_scr[kv] = (acc_scr[kv] + pv[kv]) * jnp.concatenate(alphas, axis=1)
        return jnp.concatenate(m_rows, axis=0)

    acc_scr[...] = jnp.zeros_like(acc_scr)
    p_b[...] = jnp.zeros_like(p_b)

    def attn_pair(pi, m):
        m = attn_trip(2 * pi, s_a, s_b, p_a, p_b, m)
        return attn_trip(2 * pi + 1, s_b, s_a, p_b, p_a, m)

    m_end = lax.fori_loop(0, n_pairs, attn_pair, jnp.full((N_HEADS, Q_BLOCK), M_FLOOR, F32))

    @pl.when(odd)
    def _():
        attn_trip(last, s_a, s_b, p_a, p_b, m_end)
        p_b[...] = p_a[...]

    for h in range(N_HEADS):
        kv, g = divmod(h, GROUP)
        if g == 0:
            acc_kv = acc_scr[kv] + pv_dot(last, kv, p_b)
        cols = slice(g * Q_BLOCK, (g + 1) * Q_BLOCK)
        inv_l = 1.0 / acc_kv[HEAD_DIM:HEAD_DIM + 1, cols]
        o_ref[0, :, h * HEAD_DIM:(h + 1) * HEAD_DIM] = (acc_kv[0:HEAD_DIM, cols] * inv_l).T


def _prompt_attention(qi, wit, q, ki_bf, k_bf, vt):
    B, T, _ = q.shape
    nq = T // Q_BLOCK
    n_sel = min(TOPK_MAX, T // 4)
    kvw = N_KV_HEADS * HEAD_DIM
    assert T % KEY_STEP == 0 and KEY_STEP % Q_BLOCK == 0
    count_step = next(c for c in (COUNT_STEP, COUNT_STEP // 2, KEY_STEP) if T % c == 0)
    gq = GROUP * Q_BLOCK
    return pl.pallas_call(
        functools.partial(_prompt_attn_kernel, n_sel=n_sel, count_step=count_step),
        grid=(B, nq),
        in_specs=[pl.BlockSpec((1, Q_BLOCK, N_IDX_HEADS * IDX_DIM), lambda b, j: (b, j, 0)),
                  pl.BlockSpec((1, N_IDX_HEADS, Q_BLOCK), lambda b, j: (b, 0, j)),
                  pl.BlockSpec((1, Q_BLOCK, D_MODEL), lambda b, j: (b, j, 0)),
                  pl.BlockSpec((1, T, IDX_DIM), lambda b, j: (b, 0, 0)),
                  pl.BlockSpec((1, T, kvw), lambda b, j: (b, 0, 0)),
                  pl.BlockSpec((1, T // KEY_STEP, N_KV_HEADS * VT_ROWS, KEY_STEP), lambda b, j: (b, 0, 0, 0))],
        out_specs=pl.BlockSpec((1, Q_BLOCK, D_MODEL), lambda b, j: (b, j, 0)),
        out_shape=jax.ShapeDtypeStruct((B, T, D_MODEL), F32),
        scratch_shapes=[pltpu.VMEM((T, Q_BLOCK), F32), pltpu.VMEM((T, Q_BLOCK), F32),
                        pltpu.VMEM((KEY_STEP, N_IDX_HEADS * Q_BLOCK), F32),
                        pltpu.VMEM((KEY_STEP, N_IDX_HEADS * Q_BLOCK), F32),
                        pltpu.VMEM((N_KV_HEADS, KEY_STEP, gq), F32), pltpu.VMEM((N_KV_HEADS, KEY_STEP, gq), F32),
                        pltpu.VMEM((N_KV_HEADS, KEY_STEP, gq), BF16), pltpu.VMEM((N_KV_HEADS, KEY_STEP, gq), BF16),
                        pltpu.VMEM((N_KV_HEADS, VT_ROWS, gq), F32)],
        compiler_params=pltpu.CompilerParams(dimension_semantics=("arbitrary", "arbitrary"),
                                             vmem_limit_bytes=VMEM_LIMIT),
        name="prompt_attention",
    )(qi, wit, q, ki_bf, k_bf, vt)


def _sample_score_kernel(pt_ref, qi_ref, wcol_ref, kin_ref, *rest, n_pages, ts, n_seq):
    page_refs, s_ref = rest[:n_seq * n_pages], rest[n_seq * n_pages]
    lane = lax.broadcasted_iota(jnp.int32, (ts, PAGE_SIZE), 1)
    row = lax.broadcasted_iota(jnp.int32, (ts, PAGE_SIZE), 0)
    for q in range(n_seq):
        qi = qi_ref[q].astype(F32)
        q_all = jnp.concatenate([qi[:, h * IDX_DIM:(h + 1) * IDX_DIM] for h in range(N_IDX_HEADS)],
                                axis=0).astype(BF16)
        wcol = wcol_ref[q]

        def score(dots, wcol=wcol):
            r = jnp.maximum(dots, 0.0) * wcol
            acc = r[0:ts]
            for h in range(1, N_IDX_HEADS):
                acc = acc + r[h * ts:(h + 1) * ts]
            return acc

        past = jnp.concatenate([page_refs[q * n_pages + p][0] for p in range(n_pages)], axis=1).astype(BF16)
        s_ref[q, :, 0:n_pages * PAGE_SIZE] = score(_dot(q_all, past))
        new_keys = jnp.concatenate([kin_ref[q], jnp.zeros((PAGE_SIZE - ts, IDX_DIM), F32)], axis=0).astype(BF16)
        s_ref[q, :, n_pages * PAGE_SIZE:(n_pages + 1) * PAGE_SIZE] = jnp.where(
            lane <= row, score(_dot_nt(q_all, new_keys)), NEG_INF)


def _sample_scores(page_table, qi, wcol, ki_bf, cache_kidx):
    nb, ts, _ = qi.shape
    n_pages = page_table.shape[1]
    n_seq = SCORE_SEQS if nb % SCORE_SEQS == 0 else 1

    def page_spec(q, p):
        return pl.BlockSpec((1, IDX_DIM, PAGE_SIZE), lambda b, pt: (pt[b * n_seq + q, p], 0, 0))

    grid_spec = pltpu.PrefetchScalarGridSpec(
        num_scalar_prefetch=1,
        grid=(nb // n_seq,),
        in_specs=[pl.BlockSpec((n_seq, ts, N_IDX_HEADS * IDX_DIM), lambda b, pt: (b, 0, 0)),
                  pl.BlockSpec((n_seq, N_IDX_HEADS * ts, 1), lambda b, pt: (b, 0, 0)),
                  pl.BlockSpec((n_seq, ts, IDX_DIM), lambda b, pt: (b, 0, 0))]
                 + [page_spec(q, p) for q in range(n_seq) for p in range(n_pages)],
        out_specs=pl.BlockSpec((n_seq, ts, (n_pages + 1) * PAGE_SIZE), lambda b, pt: (b, 0, 0)),
    )
    return pl.pallas_call(
        functools.partial(_sample_score_kernel, n_pages=n_pages, ts=ts, n_seq=n_seq),
        grid_spec=grid_spec,
        out_shape=jax.ShapeDtypeStruct((nb, ts, (n_pages + 1) * PAGE_SIZE), F32),
        compiler_params=pltpu.CompilerParams(dimension_semantics=("arbitrary",), vmem_limit_bytes=VMEM_LIMIT),
        name="sample_scores",
    )(page_table, qi, wcol, ki_bf, *([cache_kidx] * (n_seq * n_pages)))


def _sample_select_kernel(s_ref, b_ref, sc_scr, *, n_sel, n_chunks):
    rows = s_ref.shape[0]

    def cols(c):
        return slice(c * LANES, (c + 1) * LANES)

    def counter(ref):
        def count_ge(t):
            tf = _key_value(t)
            acc = jnp.where(ref[:, cols(0)] >= tf, 1.0, 0.0)
            for c in range(1, n_chunks):
                acc = acc + jnp.where(ref[:, cols(c)] >= tf, 1.0, 0.0)
            return jnp.sum(acc, axis=1, keepdims=True)
        return count_ge

    kk = jnp.full((rows, 1), float(n_sel), F32)
    n_all = counter(s_ref)(jnp.full((rows, 1), KEY_LOWEST, jnp.int32))
    lo, hi, n_lo, n_hi = _search_keys(counter(s_ref), kk, n_all)
    tied = n_lo > kk
    any_tie = jnp.max(jnp.where(tied, 1.0, 0.0)) > 0.5

    @pl.when(jnp.logical_not(any_tie))
    def _():
        for c in range(n_chunks):
            b_ref[:, cols(c)] = jnp.where(s_ref[:, cols(c)] >= lo, 0.0, NEG_INF)

    @pl.when(any_tie)
    def _():
        top = jnp.full((rows, LANES), NEG_INF, F32)
        for c in range(n_chunks):
            sp = _spread_ties(s_ref[:, cols(c)], lo, hi)
            sc_scr[:, cols(c)] = sp
            top = jnp.maximum(top, jnp.where(sp < -LOWEST, sp, NEG_INF))
        lo2, hi2, n_hi2 = _resolve_ties(counter(sc_scr), kk, n_lo, n_hi, tied, jnp.max(top, axis=1, keepdims=True))
        tri = jnp.where(lax.broadcasted_iota(jnp.int32, (LANES, LANES), 0)
                        <= lax.broadcasted_iota(jnp.int32, (LANES, LANES), 1), 1.0, 0.0).astype(BF16)
        run = jnp.zeros((rows, 1), F32)
        for c in range(n_chunks):
            sc = sc_scr[:, cols(c)]
            above = sc >= hi2
            elig = (sc >= lo2) & jnp.logical_not(above)
            rank = run + _dot(jnp.where(elig, 1.0, 0.0).astype(BF16), tri)
            b_ref[:, cols(c)] = jnp.where(above | (elig & (rank <= kk - n_hi2)), 0.0, NEG_INF)
            run = rank[:, LANES - 1:LANES]


def _sample_select(scores, n_sel, rows_per_step):
    rows, width = scores.shape
    return pl.pallas_call(
        functools.partial(_sample_select_kernel, n_sel=n_sel, n_chunks=width // LANES),
        grid=(rows // rows_per_step,),
        in_specs=[pl.BlockSpec((rows_per_step, width), lambda i: (i, 0))],
        out_specs=pl.BlockSpec((rows_per_step, width), lambda i: (i, 0)),
        out_shape=jax.ShapeDtypeStruct((rows, width), F32),
        scratch_shapes=[pltpu.VMEM((rows_per_step, width), F32)],
        compiler_params=pltpu.CompilerParams(dimension_semantics=("arbitrary",), vmem_limit_bytes=VMEM_LIMIT),
        name="sample_select",
    )(scores)


def _sample_attn_kernel(pt_ref, q_ref, b_ref, kn_ref, vn_ref, *rest, n_pages, ts, n_seq):
    k_refs, v_refs, o_ref = rest[:n_seq * n_pages], rest[n_seq * n_pages:2 * n_seq * n_pages], rest[2 * n_seq * n_pages]
    pad = jnp.zeros((PAGE_SIZE - ts, HEAD_DIM), F32)

    def head_rows(refs, new_ref, u, kv):
        tiles = [r[0, pl.ds(kv, PAGE_SIZE, stride=N_KV_HEADS), :] for r in refs[u * n_pages:(u + 1) * n_pages]]
        tiles += [new_ref[u, pl.ds(kv, ts, stride=N_KV_HEADS), :], pad]
        return jnp.concatenate(tiles, axis=0).astype(BF16)

    units = [(u, kv) for u in range(n_seq) for kv in range(N_KV_HEADS)]
    s = []
    for u, kv in units:
        q = q_ref[u].astype(F32)
        qg = jnp.concatenate([q[:, (kv * GROUP + g) * HEAD_DIM:(kv * GROUP + g + 1) * HEAD_DIM]
                              for g in range(GROUP)], axis=0).astype(BF16)
        bias = jnp.concatenate([b_ref[u]] * GROUP, axis=0)
        s.append(_dot_nt(qg, head_rows(k_refs, kn_ref, u, kv)) + bias)
    p_ = [jnp.exp2(x - jnp.max(x, axis=1, keepdims=True)) for x in s]
    for (u, kv), pr in zip(units, p_):
        o = _dot(pr.astype(BF16), head_rows(v_refs, vn_ref, u, kv)) / jnp.sum(pr, axis=1, keepdims=True)
        for g in range(GROUP):
            hh = kv * GROUP + g
            o_ref[u, :, hh * HEAD_DIM:(hh + 1) * HEAD_DIM] = o[g * ts:(g + 1) * ts, :]


def _sample_attention(page_table, q, bias, k_new, v_new, cache_k, cache_v):
    nb, ts, _ = q.shape
    n_pages = page_table.shape[1]
    n_seq = ATTN_SEQS if nb % ATTN_SEQS == 0 else 1

    def page_spec(u, p):
        return pl.BlockSpec((1, PAGE_SIZE * N_KV_HEADS, HEAD_DIM), lambda b, pt: (pt[b * n_seq + u, p], 0, 0))

    pages = [page_spec(u, p) for u in range(n_seq) for p in range(n_pages)]
    grid_spec = pltpu.PrefetchScalarGridSpec(
        num_scalar_prefetch=1,
        grid=(nb // n_seq,),
        in_specs=[pl.BlockSpec((n_seq, ts, D_MODEL), lambda b, pt: (b, 0, 0)),
                  pl.BlockSpec((n_seq, ts, (n_pages + 1) * PAGE_SIZE), lambda b, pt: (b, 0, 0)),
                  pl.BlockSpec((n_seq, ts * N_KV_HEADS, HEAD_DIM), lambda b, pt: (b, 0, 0)),
                  pl.BlockSpec((n_seq, ts * N_KV_HEADS, HEAD_DIM), lambda b, pt: (b, 0, 0))] + pages * 2,
        out_specs=pl.BlockSpec((n_seq, ts, D_MODEL), lambda b, pt: (b, 0, 0)),
    )
    return pl.pallas_call(
        functools.partial(_sample_attn_kernel, n_pages=n_pages, ts=ts, n_seq=n_seq),
        grid_spec=grid_spec,
        out_shape=jax.ShapeDtypeStruct((nb, ts, D_MODEL), F32),
        compiler_params=pltpu.CompilerParams(dimension_semantics=("arbitrary",), vmem_limit_bytes=VMEM_LIMIT),
        name="sample_attention",
    )(page_table, q, bias, k_new, v_new, *([cache_k] * (n_seq * n_pages)), *([cache_v] * (n_seq * n_pages)))


def _ffn_kernel(x_ref, oa_ref, ga_ref, cm_ref, p_ref, wout_ref, gffn_ref, wg_ref, wu_ref, wd_ref,
                gple_ref, wple_ref, wpg_ref, gfin_ref, y_ref, x1_scr, h_scr, acc_scr):
    f = pl.program_id(1)

    @pl.when(f == 0)
    def _():
        merged = ga_ref[...] * oa_ref[...] + cm_ref[...]
        x1 = x_ref[...] + _dot(merged.astype(BF16), wout_ref[...])
        x1_scr[...] = x1
        h_scr[...] = _rmsnorm(x1, gffn_ref[...]).astype(BF16)
        acc_scr[...] = jnp.zeros_like(acc_scr)

    h = h_scr[...]
    g = _dot(h, wg_ref[...])
    u = _dot(h, wu_ref[...])
    acc_scr[...] += _dot((g * jax.nn.sigmoid(g) * u).astype(BF16), wd_ref[...])

    @pl.when(f == pl.num_programs(1) - 1)
    def _():
        x2 = x1_scr[...] + acc_scr[...]
        gate = jax.nn.sigmoid(_dot(_rmsnorm(x2, gple_ref[...]).astype(BF16), wpg_ref[...]))
        x3 = x2 + _dot(p_ref[...].astype(BF16), wple_ref[...]) * gate
        y_ref[...] = _rmsnorm(x3, gfin_ref[...])


def _ffn(x, oa, ga, cm, p, wout, gffn, wg, wu, wd, gple, wple, wpg, gfin, *, tm, tf):
    n = x.shape[0]
    tm = min(tm, n)
    tok = lambda w: pl.BlockSpec((tm, w), lambda i, f: (i, 0))
    vec = lambda a: a.reshape(1, D_MODEL)
    return pl.pallas_call(
        _ffn_kernel,
        grid=(n // tm, D_FF // tf),
        in_specs=[tok(D_MODEL), tok(D_MODEL), tok(D_MODEL), tok(D_MODEL), tok(D_PLE),
                  _const_spec((D_MODEL, D_MODEL)), _const_spec((1, D_MODEL)),
                  pl.BlockSpec((D_MODEL, tf), lambda i, f: (0, f)),
                  pl.BlockSpec((D_MODEL, tf), lambda i, f: (0, f)),
                  pl.BlockSpec((tf, D_MODEL), lambda i, f: (f, 0)),
                  _const_spec((1, D_MODEL)), _const_spec((D_PLE, D_MODEL)), _const_spec((D_MODEL, D_MODEL)),
                  _const_spec((1, D_MODEL))],
        out_specs=tok(D_MODEL),
        out_shape=jax.ShapeDtypeStruct((n, D_MODEL), F32),
        scratch_shapes=[pltpu.VMEM((tm, D_MODEL), F32), pltpu.VMEM((tm, D_MODEL), BF16),
                        pltpu.VMEM((tm, D_MODEL), F32)],
        compiler_params=pltpu.CompilerParams(dimension_semantics=("arbitrary", "arbitrary"),
                                             vmem_limit_bytes=VMEM_LIMIT),
        name="ffn",
    )(x, oa, ga, cm, p, wout, vec(gffn), wg, wu, wd, vec(gple), wple, wpg, vec(gfin))


def _split_w_in(w_in):
    n_att = N_HEADS * HEAD_DIM + 2 * N_KV_HEADS * HEAD_DIM + N_IDX_HEADS * IDX_DIM + IDX_DIM + N_IDX_HEADS
    return w_in[:, :W1_COLS].astype(BF16), w_in[:, n_att:].astype(BF16)


def kernel(x_prompt, x_sample, cache_k, cache_v, cache_kidx, state_conv, page_table, p_prompt, p_sample, norm_mix, w_in, conv_w, w_out, norm_ffn, w_gate_up, w_down, norm_ple, w_ple, w_ple_gate, norm_final):
    Bp, Tp, _ = x_prompt.shape
    Bs, Ts, _ = x_sample.shape
    n_pages = page_table.shape[1]
    past_len = n_pages * PAGE_SIZE
    n_phys = cache_k.shape[1]
    kvw = N_KV_HEADS * HEAD_DIM
    l = 0

    w1, w2 = _split_w_in(w_in[l])
    wout = w_out[l].astype(BF16)
    wg = w_gate_up[l][:, :D_FF].astype(BF16)
    wu = w_gate_up[l][:, D_FF:].astype(BF16)
    wd = w_down[l].astype(BF16)
    wple = w_ple[l].astype(BF16)
    wpg = w_ple_gate[l].astype(BF16)

    tab_p = _rope_table(jnp.arange(Tp))
    (q_p, k_p, v_p, kbf_p, vt_p, qi_p, ki_p, kibf_p, wit_p, ga_p, cm_p, cnew_p) = _inproj(
        x_prompt, norm_mix[l], w1, w2, conv_w[l], tab_p, None, mode="prompt", tm=min(INPROJ_ROWS, Tp))
    oa_p = _prompt_attention(qi_p, wit_p, q_p, kibf_p, kbf_p, vt_p)
    n_p = Bp * Tp
    y_p = _ffn(x_prompt.reshape(n_p, D_MODEL), oa_p.reshape(n_p, D_MODEL), ga_p.reshape(n_p, D_MODEL),
               cm_p.reshape(n_p, D_MODEL), p_prompt[l].reshape(n_p, D_PLE), wout, norm_ffn[l], wg, wu, wd,
               norm_ple[l], wple, wpg, norm_final, tm=512, tf=D_FF // 2)

    n_s = Bs * Ts
    tm_s = min(INPROJ_ROWS, n_s)
    tab_s = _rope_table(past_len + (jnp.arange(tm_s) % Ts))
    prev = jnp.concatenate([state_conv[l], jnp.zeros((Bs, Ts - (CONV_WIDTH - 1), D_CONV), F32)], axis=1)
    (q_s, k_s, v_s, qi_s, ki_s, kibf_s, wit_s, ga_s, cm_s, u_s) = _inproj(
        x_sample.reshape(n_s // tm_s, tm_s, D_MODEL), norm_mix[l], w1, w2, conv_w[l], tab_s,
        prev.reshape(n_s // tm_s, tm_s, D_CONV), mode="sample", tm=tm_s)
    wcol = wit_s.transpose(1, 0, 2).reshape(N_IDX_HEADS, Bs, Ts).transpose(1, 0, 2).reshape(Bs, N_IDX_HEADS * Ts, 1)
    scores = _sample_scores(page_table, qi_s.reshape(Bs, Ts, -1), wcol, ki_s.reshape(Bs, Ts, IDX_DIM),
                            jnp.swapaxes(cache_kidx[l], 1, 2))
    n_sel = min(TOPK_MAX, (past_len + Ts) // 4)
    bias = _sample_select(scores.reshape(n_s, -1), n_sel, 256)
    oa_s = _sample_attention(page_table, q_s.reshape(Bs, Ts, D_MODEL), bias.reshape(Bs, Ts, -1),
                             k_s.reshape(Bs, Ts * N_KV_HEADS, HEAD_DIM), v_s.reshape(Bs, Ts * N_KV_HEADS, HEAD_DIM),
                             cache_k[l].reshape(n_phys, PAGE_SIZE * N_KV_HEADS, HEAD_DIM),
                             cache_v[l].reshape(n_phys, PAGE_SIZE * N_KV_HEADS, HEAD_DIM))
    y_s = _ffn(x_sample.reshape(n_s, D_MODEL), oa_s.reshape(n_s, D_MODEL), ga_s.reshape(n_s, D_MODEL),
               cm_s.reshape(n_s, D_MODEL), p_sample[l].reshape(n_s, D_PLE), wout, norm_ffn[l], wg, wu, wd,
               norm_ple[l], wple, wpg, norm_final, tm=512, tf=D_FF // 2)

    return (y_p.reshape(Bp, Tp, D_MODEL), y_s.reshape(Bs, Ts, D_MODEL),
            k_p.reshape(1, Bp, Tp, N_KV_HEADS, HEAD_DIM), v_p.reshape(1, Bp, Tp, N_KV_HEADS, HEAD_DIM),
            ki_p.reshape(1, Bp, Tp, IDX_DIM), cnew_p.reshape(1, Bp, CONV_WIDTH - 1, D_CONV),
            k_s.reshape(1, Bs, Ts, N_KV_HEADS, HEAD_DIM), v_s.reshape(1, Bs, Ts, N_KV_HEADS, HEAD_DIM),
            ki_s.reshape(1, Bs, Ts, IDX_DIM),
            u_s.reshape(Bs, Ts, D_CONV)[:, Ts - (CONV_WIDTH - 1):, :].reshape(1, Bs, CONV_WIDTH - 1, D_CONV))
```

```python
import functools

import jax
import jax.numpy as jnp
from jax import lax
from jax.experimental import pallas as pl
from jax.experimental.pallas import tpu as pltpu

D_MODEL = 1024
N_HEADS = 8
N_KV_HEADS = 2
GROUP = N_HEADS // N_KV_HEADS
HEAD_DIM = 128
N_IDX_HEADS = 8
IDX_DIM = 64
IDX_SCALE = (N_IDX_HEADS * IDX_DIM) ** -0.5
QK_SCALE = HEAD_DIM ** -0.5
TOPK_MAX = 256
D_CONV = D_MODEL
CONV_WIDTH = 3
D_FF = 2816
D_PLE = 256
PAGE_SIZE = 128
ROPE_THETA = 10000.0
EPS = 1e-6

LANES = 128
SUBLANES = 8
KEY_STEP = 256
Q_BLOCK = 128
COUNT_STEP = 512
COUNT_LANES = 8
VT_ROWS = HEAD_DIM + 16
CONV_SLAB = 256
INPROJ_ROWS = 512
SCORE_SEQS = 4
ATTN_SEQS = 2
VMEM_LIMIT = 56 * 1024 * 1024

W1_COLS = N_HEADS * HEAD_DIM + 2 * N_KV_HEADS * HEAD_DIM + N_IDX_HEADS * IDX_DIM + LANES
W2_COLS = 5 * D_MODEL

F32 = jnp.float32
BF16 = jnp.bfloat16
NEG_INF = float("-inf")
LOWEST = -3.4028234663852886e38
TINY = 1.1754943508222875e-38
KEY_LOWEST = -0x7F800000
KEY_INF = 0x7F800000
SEARCH_BLIND = 24
M_FLOOR = -1e30
LOG2E = 1.4426950408889634


def _dot(a, b):
    return jnp.dot(a, b, preferred_element_type=F32)


def _dot_nt(a, b):
    return lax.dot_general(a, b, (((1,), (1,)), ((), ())), preferred_element_type=F32)


def _rmsnorm(x, g):
    var = jnp.mean(x * x, axis=-1, keepdims=True)
    return (x * lax.rsqrt(var + EPS)) * g


def _rope_table(pos):
    def tab(half):
        freqs = ROPE_THETA ** (-jnp.arange(half, dtype=F32) / half)
        ang = pos.astype(F32)[:, None] * freqs[None, :]
        return [jnp.cos(ang), jnp.sin(ang)]
    return jnp.concatenate(tab(HEAD_DIM // 2) + tab(IDX_DIM // 2), axis=1)


def _rope128(x, cos, sin):
    return x * cos + pltpu.roll(x, HEAD_DIM // 2, axis=1) * sin


def _rope64(x, cos, sin, first_half):
    partner = jnp.where(first_half, pltpu.roll(x, LANES - IDX_DIM // 2, axis=1), pltpu.roll(x, IDX_DIM // 2, axis=1))
    return x * cos + partner * sin


def _inproj_kernel(*refs, mode, tm):
    if mode == "prompt":
        (x_ref, g_ref, w1_ref, w2_ref, cw_ref, tab_ref,
         q_ref, k_ref, v_ref, kbf_ref, vt_ref, qi_ref, ki_ref, kibf_ref, wit_ref, ga_ref, cm_ref, cnew_ref,
         carry_ref) = refs
    else:
        (x_ref, g_ref, w1_ref, w2_ref, cw_ref, tab_ref, prev_ref,
         q_ref, k_ref, v_ref, qi_ref, ki_ref, wit_ref, ga_ref, cm_ref, u_ref) = refs

    x = x_ref[0]
    h = _rmsnorm(x, g_ref[...]).astype(BF16)
    z1 = _dot(h, w1_ref[...])
    tab = tab_ref[...]
    h128, h64 = HEAD_DIM // 2, IDX_DIM // 2
    c, sn = tab[:, 0:h128], tab[:, h128:2 * h128]
    cos128, sin128 = jnp.concatenate([c, c], axis=1), jnp.concatenate([-sn, sn], axis=1)
    c, sn = tab[:, 2 * h128:2 * h128 + h64], tab[:, 2 * h128 + h64:2 * h128 + 2 * h64]
    cos64 = jnp.concatenate([c, c] * (LANES // IDX_DIM), axis=1)
    sin64 = jnp.concatenate([-sn, sn] * (LANES // IDX_DIM), axis=1)
    lane = lax.broadcasted_iota(jnp.int32, (tm, LANES), 1)
    first_half = (lane % IDX_DIM) < (IDX_DIM // 2)

    off = 0
    for hh in range(N_HEADS):
        sl = z1[:, off:off + HEAD_DIM]
        q_ref[0, :, hh * HEAD_DIM:(hh + 1) * HEAD_DIM] = (_rope128(sl, cos128, sin128) * (QK_SCALE * LOG2E)).astype(BF16)
        off += HEAD_DIM
    for hh in range(N_KV_HEADS):
        kr = _rope128(z1[:, off:off + HEAD_DIM], cos128, sin128)
        k_ref[0, pl.ds(hh, tm, stride=N_KV_HEADS), :] = kr
        if mode == "prompt":
            kbf_ref[0, :, hh * HEAD_DIM:(hh + 1) * HEAD_DIM] = kr.astype(BF16)
        off += HEAD_DIM
    v = z1[:, off:off + N_KV_HEADS * HEAD_DIM]
    for hh in range(N_KV_HEADS):
        v_ref[0, pl.ds(hh, tm, stride=N_KV_HEADS), :] = v[:, hh * HEAD_DIM:(hh + 1) * HEAD_DIM]
    if mode == "prompt":
        for c in range(tm // KEY_STEP):
            vt = v[c * KEY_STEP:(c + 1) * KEY_STEP, :].T
            ones_row = jnp.where(lax.broadcasted_iota(jnp.int32, (VT_ROWS - HEAD_DIM, KEY_STEP), 0) == 0, 1.0, 0.0)
            for hh in range(N_KV_HEADS):
                vt_ref[0, c, hh * VT_ROWS:(hh + 1) * VT_ROWS, :] = jnp.concatenate(
                    [vt[hh * HEAD_DIM:(hh + 1) * HEAD_DIM, :], ones_row], axis=0).astype(BF16)
    off += N_KV_HEADS * HEAD_DIM
    for hh in range(N_IDX_HEADS * IDX_DIM // LANES):
        sl = z1[:, off:off + LANES]
        qi_ref[0, :, hh * LANES:(hh + 1) * LANES] = _rope64(sl, cos64, sin64, first_half).astype(BF16)
        off += LANES
    kiw = z1[:, off:off + LANES]
    kir = _rope64(kiw, cos64, sin64, first_half)[:, 0:IDX_DIM]
    ki_ref[0] = kir
    if mode == "prompt":
        kibf_ref[0] = kir.astype(BF16)
    wit_ref[0] = kiw.T[IDX_DIM:IDX_DIM + N_IDX_HEADS, :] * IDX_SCALE

    if mode == "prompt":
        @pl.when(pl.program_id(1) == 0)
        def _():
            carry_ref[...] = jnp.zeros_like(carry_ref)

    row = lax.broadcasted_iota(jnp.int32, (tm, CONV_SLAB), 0)
    for c in range(D_CONV // CONV_SLAB):
        cols = slice(c * CONV_SLAB, (c + 1) * CONV_SLAB)
        bg, cg, xc, ga, gb = (_dot(h, w2_ref[:, k * D_MODEL + c * CONV_SLAB:k * D_MODEL + (c + 1) * CONV_SLAB])
                              for k in range(5))
        u = cg * xc
        r1 = pltpu.roll(u, 1, axis=0)
        r2 = pltpu.roll(u, 2, axis=0)
        if mode == "prompt":
            c0 = carry_ref[0:1, cols]
            c1 = carry_ref[1:2, cols]
            um1 = jnp.where(row == 0, c1, r1)
            um2 = jnp.where(row == 0, c0, jnp.where(row == 1, c1, r2))
            carry_ref[0:2, cols] = u[tm - 2:tm, :]
            cnew_ref[0, :, cols] = u[tm - 2:tm, :]
        else:
            prev = prev_ref[0, :, cols]
            seq_row = row % SUBLANES
            um1 = jnp.where(seq_row == 0, pltpu.roll(prev, tm - 1, axis=0), r1)
            um2 = jnp.where(seq_row < 2, prev, r2)
            u_ref[0, :, cols] = u
        cw = cw_ref[:, cols]
        conv = cw[0:1, :] * um2 + cw[1:2, :] * um1 + cw[2:3, :] * u
        ga_ref[0, :, cols] = jax.nn.sigmoid(ga)
        cm_ref[0, :, cols] = jax.nn.sigmoid(gb) * (bg * conv)


def _const_spec(shape):
    nd = len(shape)
    return pl.BlockSpec(shape, lambda *_: (0,) * nd, pipeline_mode=pl.Buffered(1))


def _inproj(x, norm_g, w1, w2, conv_w, tab, prev, *, mode, tm):
    B, T, _ = x.shape
    assert T % tm == 0 and tm % KEY_STEP == 0 and D_CONV % CONV_SLAB == 0
    nt = T // tm
    tok = lambda w: pl.BlockSpec((1, tm, w), lambda b, t: (b, t, 0))
    in_specs = [tok(D_MODEL), _const_spec((1, D_MODEL)), _const_spec((D_MODEL, W1_COLS)),
                _const_spec((D_MODEL, W2_COLS)), _const_spec((CONV_WIDTH, D_CONV)),
                pl.BlockSpec((tm, HEAD_DIM + IDX_DIM), lambda b, t: (t, 0))]
    args = [x, norm_g.reshape(1, D_MODEL), w1, w2, conv_w, tab]
    kvw = N_KV_HEADS * HEAD_DIM
    qiw = N_IDX_HEADS * IDX_DIM
    wit_spec = pl.BlockSpec((1, N_IDX_HEADS, tm), lambda b, t: (b, 0, t))
    kv_spec = pl.BlockSpec((1, tm * N_KV_HEADS, HEAD_DIM), lambda b, t: (b, t, 0))
    if mode == "prompt":
        out_shape = [
            jax.ShapeDtypeStruct((B, T, D_MODEL), BF16),
            jax.ShapeDtypeStruct((B, T * N_KV_HEADS, HEAD_DIM), F32),
            jax.ShapeDtypeStruct((B, T * N_KV_HEADS, HEAD_DIM), F32),
            jax.ShapeDtypeStruct((B, T, kvw), BF16),
            jax.ShapeDtypeStruct((B, T // KEY_STEP, N_KV_HEADS * VT_ROWS, KEY_STEP), BF16),
            jax.ShapeDtypeStruct((B, T, qiw), BF16),
            jax.ShapeDtypeStruct((B, T, IDX_DIM), F32),
            jax.ShapeDtypeStruct((B, T, IDX_DIM), BF16),
            jax.ShapeDtypeStruct((B, N_IDX_HEADS, T), F32),
            jax.ShapeDtypeStruct((B, T, D_MODEL), F32),
            jax.ShapeDtypeStruct((B, T, D_MODEL), F32),
            jax.ShapeDtypeStruct((B, CONV_WIDTH - 1, D_CONV), F32),
        ]
        out_specs = [tok(D_MODEL), kv_spec, kv_spec, tok(kvw),
                     pl.BlockSpec((1, tm // KEY_STEP, N_KV_HEADS * VT_ROWS, KEY_STEP), lambda b, t: (b, t, 0, 0)),
                     tok(qiw), tok(IDX_DIM), tok(IDX_DIM), wit_spec, tok(D_MODEL), tok(D_MODEL),
                     pl.BlockSpec((1, CONV_WIDTH - 1, D_CONV), lambda b, t: (b, 0, 0))]
        scratch = [pltpu.VMEM((SUBLANES, D_CONV), F32)]
    else:
        in_specs.append(tok(D_CONV))
        args.append(prev)
        out_shape = [
            jax.ShapeDtypeStruct((B, T, D_MODEL), BF16),
            jax.ShapeDtypeStruct((B, T * N_KV_HEADS, HEAD_DIM), F32),
            jax.ShapeDtypeStruct((B, T * N_KV_HEADS, HEAD_DIM), F32),
            jax.ShapeDtypeStruct((B, T, qiw), BF16),
            jax.ShapeDtypeStruct((B, T, IDX_DIM), F32),
            jax.ShapeDtypeStruct((B, N_IDX_HEADS, T), F32),
            jax.ShapeDtypeStruct((B, T, D_MODEL), F32),
            jax.ShapeDtypeStruct((B, T, D_MODEL), F32),
            jax.ShapeDtypeStruct((B, T, D_CONV), F32),
        ]
        out_specs = [tok(D_MODEL), kv_spec, kv_spec, tok(qiw), tok(IDX_DIM), wit_spec,
                     tok(D_MODEL), tok(D_MODEL), tok(D_CONV)]
        scratch = []
    return pl.pallas_call(
        functools.partial(_inproj_kernel, mode=mode, tm=tm),
        grid=(B, nt),
        in_specs=in_specs,
        out_specs=out_specs,
        out_shape=out_shape,
        scratch_shapes=scratch,
        compiler_params=pltpu.CompilerParams(dimension_semantics=("arbitrary", "arbitrary"),
                                             vmem_limit_bytes=VMEM_LIMIT),
        name="inproj_" + mode,
    )(*args)


def _key_value(key):
    return lax.bitcast_convert_type(key ^ ((key >> 31) & 0x7FFFFFFF), F32)


def _search_keys(count_ge, kk, n_all):
    lo0 = jnp.full(kk.shape, KEY_LOWEST, jnp.int32)
    hi0 = jnp.full(kk.shape, KEY_INF, jnp.int32)
    n_lo0 = n_all
    def body(_, c):
        lo, hi, n_lo, n_hi, done = c
        mid = (lo & hi) + ((lo ^ hi) >> 1)
        stuck = mid == lo
        n_mid = count_ge(mid)
        upd = (done < 0.5) & jnp.logical_not(stuck)
        go_lo = n_mid >= kk
        up_lo = upd & go_lo
        lo = jnp.where(up_lo, mid, lo)
        n_lo = jnp.where(up_lo, n_mid, n_lo)
        up_hi = upd & jnp.logical_not(go_lo)
        hi = jnp.where(up_hi, mid, hi)
        n_hi = jnp.where(up_hi, n_mid, n_hi)
        done = jnp.where(stuck | (upd & (n_mid == kk)), 1.0, done)
        return lo, hi, n_lo, n_hi, done

    done0 = jnp.where(n_lo0 <= kk, 1.0, 0.0)
    state = (lo0, hi0, n_lo0, jnp.zeros_like(kk), done0)
    state = lax.fori_loop(0, SEARCH_BLIND, body, state)
    lo, hi, n_lo, n_hi, _ = lax.while_loop(lambda c: jnp.min(c[4]) < 0.5, lambda c: body(0, body(0, c)), state)
    return _key_value(lo), _key_value(hi), n_lo, n_hi


def _resolve_ties(count_ge, kk, n_lo, n_hi, tied, top):
    def by_position():
        return jnp.zeros_like(kk), jnp.where(tied, TINY, -LOWEST), n_hi

    def by_value():
        lo2, hi2, _, n_hi2 = _search_keys(count_ge, kk, n_lo)
        return lo2, hi2, n_hi2

    return lax.cond(jnp.max(jnp.where(tied & (top > 0.0), 1.0, 0.0)) > 0.5, by_value, by_position)


def _spread_ties(sc, lo, hi):
    return jnp.where(sc >= hi, -LOWEST, jnp.where(sc >= lo, sc - lo, NEG_INF))


def _fold_rows(x, op):
    return op(x.reshape(x.shape[0] // SUBLANES, SUBLANES, x.shape[1]), axis=0)


def _prompt_attn_kernel(qi_ref, wit_ref, q_ref, ki_ref, k_ref, vt_ref, o_ref,
                        sc_scr, b_scr, d_a, d_b, s_a, s_b, p_a, p_b, acc_scr, *, n_sel, count_step):
    j = pl.program_id(1)
    per_step = KEY_STEP // Q_BLOCK
    n_steps = (j + per_step) // per_step
    n_pairs = n_steps // 2
    odd = n_steps % 2 == 1
    last = n_steps - 1
    per_count = count_step // KEY_STEP
    n_count = (n_steps + per_count - 1) // per_count
    qi = qi_ref[0]
    q_stack = jnp.concatenate([qi[:, h * IDX_DIM:(h + 1) * IDX_DIM] for h in range(N_IDX_HEADS)], axis=0)
    wit = wit_ref[0]
    q = q_ref[0]
    q_grp = [jnp.concatenate([q[:, (kv * GROUP + g) * HEAD_DIM:(kv * GROUP + g + 1) * HEAD_DIM]
                              for g in range(GROUP)], axis=0) for kv in range(N_KV_HEADS)]
    key_pos = lax.broadcasted_iota(jnp.int32, (KEY_STEP, Q_BLOCK), 0)
    q_pos = lax.broadcasted_iota(jnp.int32, (KEY_STEP, Q_BLOCK), 1) + j * Q_BLOCK

    def step(i):
        return pl.ds(pl.multiple_of(i * KEY_STEP, KEY_STEP), KEY_STEP)

    def idx_dots(i):
        return _dot_nt(ki_ref[0, step(i), :], q_stack)

    def score_trip(i, d_cur, d_nxt):
        d_nxt[...] = idx_dots(jnp.minimum(i + 1, last))
        acc = wit[0:1, :] * jnp.maximum(d_cur[:, 0:Q_BLOCK], 0.0)
        for h in range(1, N_IDX_HEADS):
            acc = acc + wit[h:h + 1, :] * jnp.maximum(d_cur[:, h * Q_BLOCK:(h + 1) * Q_BLOCK], 0.0)
        sc_scr[step(i), :] = jnp.where(key_pos + i * KEY_STEP <= q_pos, acc, NEG_INF)

    def logits(i, kv):
        return _dot_nt(k_ref[0, step(i), kv * HEAD_DIM:(kv + 1) * HEAD_DIM], q_grp[kv])

    d_a[...] = idx_dots(0)
    for kv in range(N_KV_HEADS):
        s_a[kv] = logits(0, kv)

    def score_pair(pi, c):
        score_trip(2 * pi, d_a, d_b)
        score_trip(2 * pi + 1, d_b, d_a)
        return c

    lax.fori_loop(0, n_pairs, score_pair, 0)

    @pl.when(odd)
    def _():
        score_trip(last, d_a, d_b)

    def fill(i, c):
        sc_scr[step(i), :] = jnp.full((KEY_STEP, Q_BLOCK), NEG_INF, F32)
        return c

    lax.fori_loop(n_steps, n_count * per_count, fill, 0)

    def count_ge(t):
        tb = jnp.broadcast_to(_key_value(t), (SUBLANES, Q_BLOCK))

        def body(i, accs):
            accs = list(accs)
            base = pl.multiple_of(i * count_step, count_step)
            for r in range(count_step // SUBLANES):
                sc = sc_scr[pl.ds(base + r * SUBLANES, SUBLANES), :]
                accs[r % COUNT_LANES] = accs[r % COUNT_LANES] + jnp.where(sc >= tb, 1.0, 0.0)
            return tuple(accs)

        accs = lax.fori_loop(0, n_count, body, tuple(jnp.zeros((SUBLANES, Q_BLOCK), F32) for _ in range(COUNT_LANES)))
        return jnp.sum(functools.reduce(lambda a, b: a + b, accs), axis=0, keepdims=True)

    n_valid = (j * Q_BLOCK + 1 + lax.broadcasted_iota(jnp.int32, (1, Q_BLOCK), 1)).astype(F32)
    kk = jnp.minimum(n_valid, float(n_sel))
    lo, hi, n_lo, n_hi = lax.cond((j + 1) * Q_BLOCK <= n_sel,
                                  lambda: (jnp.full_like(kk, LOWEST), jnp.full_like(kk, jnp.inf), n_valid,
                                           jnp.zeros_like(kk)),
                                  lambda: _search_keys(count_ge, kk, n_valid))

    def mask_plain(i, c):
        b_scr[step(i), :] = jnp.where(sc_scr[step(i), :] >= lo, 0.0, NEG_INF)
        return c

    def spread(i, top):
        sp = _spread_ties(sc_scr[step(i), :], lo, hi)
        sc_scr[step(i), :] = sp
        return jnp.maximum(top, _fold_rows(jnp.where(sp < -LOWEST, sp, NEG_INF), jnp.max))

    tied = n_lo > kk
    any_tie = jnp.max(jnp.where(tied, 1.0, 0.0)) > 0.5

    @pl.when(any_tie)
    def _():
        top = lax.fori_loop(0, n_steps, spread, jnp.full((SUBLANES, Q_BLOCK), NEG_INF, F32))
        lo2, hi2, n_hi2 = _resolve_ties(count_ge, kk, n_lo, n_hi, tied, jnp.max(top, axis=0, keepdims=True))
        tri = jnp.where(lax.broadcasted_iota(jnp.int32, (KEY_STEP, KEY_STEP), 1)
                        <= lax.broadcasted_iota(jnp.int32, (KEY_STEP, KEY_STEP), 0), 1.0, 0.0).astype(BF16)

        def mask_ties(i, run):
            sc = sc_scr[step(i), :]
            above = sc >= hi2
            elig = (sc >= lo2) & jnp.logical_not(above)
            rank = run + _dot(tri, jnp.where(elig, 1.0, 0.0).astype(BF16))
            b_scr[step(i), :] = jnp.where(above | (elig & (rank <= kk - n_hi2)), 0.0, NEG_INF)
            return rank[KEY_STEP - 1:KEY_STEP, :]

        lax.fori_loop(0, n_steps, mask_ties, jnp.zeros((1, Q_BLOCK), F32))

    @pl.when(jnp.logical_not(any_tie))
    def _():
        lax.fori_loop(0, n_steps, mask_plain, 0)

    def pv_dot(i, kv, p_ref):
        return _dot(vt_ref[0, i, kv * VT_ROWS:(kv + 1) * VT_ROWS, :], p_ref[kv])

    def attn_trip(i, s_cur, s_nxt, p_cur, p_prv, m):
        pv = [pv_dot(jnp.maximum(i - 1, 0), kv, p_prv) for kv in range(N_KV_HEADS)]
        for kv in range(N_KV_HEADS):
            s_nxt[kv] = logits(jnp.minimum(i + 1, last), kv)
        b = b_scr[step(i), :]
        m_rows = []
        for kv in range(N_KV_HEADS):
            ps, alphas = [], []
            for g in range(GROUP):
                h = kv * GROUP + g
                sg = s_cur[kv, :, g * Q_BLOCK:(g + 1) * Q_BLOCK] + b
                m_old = m[h:h + 1, :]
                m_new = jnp.maximum(m_old, jnp.max(_fold_rows(sg, jnp.max), axis=0, keepdims=True))
                ps.append(jnp.exp2(sg - m_new).astype(BF16))
                alphas.append(jnp.exp2(m_old - m_new))
                m_rows.append(m_new)
            p_cur[kv] = jnp.concatenate(ps, axis=1)
            acc_scr[kv] = (acc_scr[kv] + pv[kv]) * jnp.concatenate(alphas, axis=1)
        return jnp.concatenate(m_rows, axis=0)

    acc_scr[...] = jnp.zeros_like(acc_scr)
    p_b[...] = jnp.zeros_like(p_b)

    def attn_pair(pi, m):
        m = attn_trip(2 * pi, s_a, s_b, p_a, p_b, m)
        return attn_trip(2 * pi + 1, s_b, s_a, p_b, p_a, m)

    m_end = lax.fori_loop(0, n_pairs, attn_pair, jnp.full((N_HEADS, Q_BLOCK), M_FLOOR, F32))

    @pl.when(odd)
    def _():
        attn_trip(last, s_a, s_b, p_a, p_b, m_end)
        p_b[...] = p_a[...]

    for h in range(N_HEADS):
        kv, g = divmod(h, GROUP)
        if g == 0:
            acc_kv = acc_scr[kv] + pv_dot(last, kv, p_b)
        cols = slice(g * Q_BLOCK, (g + 1) * Q_BLOCK)
        inv_l = 1.0 / acc_kv[HEAD_DIM:HEAD_DIM + 1, cols]
        o_ref[0, :, h * HEAD_DIM:(h + 1) * HEAD_DIM] = (acc_kv[0:HEAD_DIM, cols] * inv_l).T


def _prompt_attention(qi, wit, q, ki_bf, k_bf, vt):
    B, T, _ = q.shape
    nq = T // Q_BLOCK
    n_sel = min(TOPK_MAX, T // 4)
    kvw = N_KV_HEADS * HEAD_DIM
    assert T % KEY_STEP == 0 and KEY_STEP % Q_BLOCK == 0
    count_step = next(c for c in (COUNT_STEP, COUNT_STEP // 2, KEY_STEP) if T % c == 0)
    gq = GROUP * Q_BLOCK
    return pl.pallas_call(
        functools.partial(_prompt_attn_kernel, n_sel=n_sel, count_step=count_step),
        grid=(B, nq),
        in_specs=[pl.BlockSpec((1, Q_BLOCK, N_IDX_HEADS * IDX_DIM), lambda b, j: (b, j, 0)),
                  pl.BlockSpec((1, N_IDX_HEADS, Q_BLOCK), lambda b, j: (b, 0, j)),
                  pl.BlockSpec((1, Q_BLOCK, D_MODEL), lambda b, j: (b, j, 0)),
                  pl.BlockSpec((1, T, IDX_DIM), lambda b, j: (b, 0, 0)),
                  pl.BlockSpec((1, T, kvw), lambda b, j: (b, 0, 0)),
                  pl.BlockSpec((1, T // KEY_STEP, N_KV_HEADS * VT_ROWS, KEY_STEP), lambda b, j: (b, 0, 0, 0))],
        out_specs=pl.BlockSpec((1, Q_BLOCK, D_MODEL), lambda b, j: (b, j, 0)),
        out_shape=jax.ShapeDtypeStruct((B, T, D_MODEL), F32),
        scratch_shapes=[pltpu.VMEM((T, Q_BLOCK), F32), pltpu.VMEM((T, Q_BLOCK), F32),
                        pltpu.VMEM((KEY_STEP, N_IDX_HEADS * Q_BLOCK), F32),
                        pltpu.VMEM((KEY_STEP, N_IDX_HEADS * Q_BLOCK), F32),
                        pltpu.VMEM((N_KV_HEADS, KEY_STEP, gq), F32), pltpu.VMEM((N_KV_HEADS, KEY_STEP, gq), F32),
                        pltpu.VMEM((N_KV_HEADS, KEY_STEP, gq), BF16), pltpu.VMEM((N_KV_HEADS, KEY_STEP, gq), BF16),
                        pltpu.VMEM((N_KV_HEADS, VT_ROWS, gq), F32)],
        compiler_params=pltpu.CompilerParams(dimension_semantics=("arbitrary", "arbitrary"),
                                             vmem_limit_bytes=VMEM_LIMIT),
        name="prompt_attention",
    )(qi, wit, q, ki_bf, k_bf, vt)


def _sample_score_kernel(pt_ref, qi_ref, wcol_ref, kin_ref, *rest, n_pages, ts, n_seq):
    page_refs, s_ref = rest[:n_seq * n_pages], rest[n_seq * n_pages]
    lane = lax.broadcasted_iota(jnp.int32, (ts, PAGE_SIZE), 1)
    row = lax.broadcasted_iota(jnp.int32, (ts, PAGE_SIZE), 0)
    for q in range(n_seq):
        qi = qi_ref[q].astype(F32)
        q_all = jnp.concatenate([qi[:, h * IDX_DIM:(h + 1) * IDX_DIM] for h in range(N_IDX_HEADS)],
                                axis=0).astype(BF16)
        wcol = wcol_ref[q]

        def score(dots, wcol=wcol):
            r = jnp.maximum(dots, 0.0) * wcol
            acc = r[0:ts]
            for h in range(1, N_IDX_HEADS):
                acc = acc + r[h * ts:(h + 1) * ts]
            return acc

        past = jnp.concatenate([page_refs[q * n_pages + p][0] for p in range(n_pages)], axis=1).astype(BF16)
        s_ref[q, :, 0:n_pages * PAGE_SIZE] = score(_dot(q_all, past))
        new_keys = jnp.concatenate([kin_ref[q], jnp.zeros((PAGE_SIZE - ts, IDX_DIM), F32)], axis=0).astype(BF16)
        s_ref[q, :, n_pages * PAGE_SIZE:(n_pages + 1) * PAGE_SIZE] = jnp.where(
            lane <= row, score(_dot_nt(q_all, new_keys)), NEG_INF)


def _sample_scores(page_table, qi, wcol, ki_bf, cache_kidx):
    nb, ts, _ = qi.shape
    n_pages = page_table.shape[1]
    n_seq = SCORE_SEQS if nb % SCORE_SEQS == 0 else 1

    def page_spec(q, p):
        return pl.BlockSpec((1, IDX_DIM, PAGE_SIZE), lambda b, pt: (pt[b * n_seq + q, p], 0, 0))

    grid_spec = pltpu.PrefetchScalarGridSpec(
        num_scalar_prefetch=1,
        grid=(nb // n_seq,),
        in_specs=[pl.BlockSpec((n_seq, ts, N_IDX_HEADS * IDX_DIM), lambda b, pt: (b, 0, 0)),
                  pl.BlockSpec((n_seq, N_IDX_HEADS * ts, 1), lambda b, pt: (b, 0, 0)),
                  pl.BlockSpec((n_seq, ts, IDX_DIM), lambda b, pt: (b, 0, 0))]
                 + [page_spec(q, p) for q in range(n_seq) for p in range(n_pages)],
        out_specs=pl.BlockSpec((n_seq, ts, (n_pages + 1) * PAGE_SIZE), lambda b, pt: (b, 0, 0)),
    )
    return pl.pallas_call(
        functools.partial(_sample_score_kernel, n_pages=n_pages, ts=ts, n_seq=n_seq),
        grid_spec=grid_spec,
        out_shape=jax.ShapeDtypeStruct((nb, ts, (n_pages + 1) * PAGE_SIZE), F32),
        compiler_params=pltpu.CompilerParams(dimension_semantics=("arbitrary",), vmem_limit_bytes=VMEM_LIMIT),
        name="sample_scores",
    )(page_table, qi, wcol, ki_bf, *([cache_kidx] * (n_seq * n_pages)))


def _sample_select_kernel(s_ref, b_ref, sc_scr, *, n_sel, n_chunks):
    rows = s_ref.shape[0]

    def cols(c):
        return slice(c * LANES, (c + 1) * LANES)

    def counter(ref):
        def count_ge(t):
            tf = _key_value(t)
            acc = jnp.where(ref[:, cols(0)] >= tf, 1.0, 0.0)
            for c in range(1, n_chunks):
                acc = acc + jnp.where(ref[:, cols(c)] >= tf, 1.0, 0.0)
            return jnp.sum(acc, axis=1, keepdims=True)
        return count_ge

    kk = jnp.full((rows, 1), float(n_sel), F32)
    n_all = counter(s_ref)(jnp.full((rows, 1), KEY_LOWEST, jnp.int32))
    lo, hi, n_lo, n_hi = _search_keys(counter(s_ref), kk, n_all)
    tied = n_lo > kk
    any_tie = jnp.max(jnp.where(tied, 1.0, 0.0)) > 0.5

    @pl.when(jnp.logical_not(any_tie))
    def _():
        for c in range(n_chunks):
            b_ref[:, cols(c)] = jnp.where(s_ref[:, cols(c)] >= lo, 0.0, NEG_INF)

    @pl.when(any_tie)
    def _():
        top = jnp.full((rows, LANES), NEG_INF, F32)
        for c in range(n_chunks):
            sp = _spread_ties(s_ref[:, cols(c)], lo, hi)
            sc_scr[:, cols(c)] = sp
            top = jnp.maximum(top, jnp.where(sp < -LOWEST, sp, NEG_INF))
        lo2, hi2, n_hi2 = _resolve_ties(counter(sc_scr), kk, n_lo, n_hi, tied, jnp.max(top, axis=1, keepdims=True))
        tri = jnp.where(lax.broadcasted_iota(jnp.int32, (LANES, LANES), 0)
                        <= lax.broadcasted_iota(jnp.int32, (LANES, LANES), 1), 1.0, 0.0).astype(BF16)
        run = jnp.zeros((rows, 1), F32)
        for c in range(n_chunks):
            sc = sc_scr[:, cols(c)]
            above = sc >= hi2
            elig = (sc >= lo2) & jnp.logical_not(above)
            rank = run + _dot(jnp.where(elig, 1.0, 0.0).astype(BF16), tri)
            b_ref[:, cols(c)] = jnp.where(above | (elig & (rank <= kk - n_hi2)), 0.0, NEG_INF)
            run = rank[:, LANES - 1:LANES]


def _sample_select(scores, n_sel, rows_per_step):
    rows, width = scores.shape
    assert rows % rows_per_step == 0 and width % LANES == 0
    return pl.pallas_call(
        functools.partial(_sample_select_kernel, n_sel=n_sel, n_chunks=width // LANES),
        grid=(rows // rows_per_step,),
        in_specs=[pl.BlockSpec((rows_per_step, width), lambda i: (i, 0))],
        out_specs=pl.BlockSpec((rows_per_step, width), lambda i: (i, 0)),
        out_shape=jax.ShapeDtypeStruct((rows, width), F32),
        scratch_shapes=[pltpu.VMEM((rows_per_step, width), F32)],
        compiler_params=pltpu.CompilerParams(dimension_semantics=("arbitrary",), vmem_limit_bytes=VMEM_LIMIT),
        name="sample_select",
    )(scores)


def _sample_attn_kernel(pt_ref, q_ref, b_ref, kn_ref, vn_ref, *rest, n_pages, ts, n_seq):
    k_refs, v_refs, o_ref = rest[:n_seq * n_pages], rest[n_seq * n_pages:2 * n_seq * n_pages], rest[2 * n_seq * n_pages]
    pad = jnp.zeros((PAGE_SIZE - ts, HEAD_DIM), F32)

    def head_rows(refs, new_ref, u, kv):
        tiles = [r[0, pl.ds(kv, PAGE_SIZE, stride=N_KV_HEADS), :] for r in refs[u * n_pages:(u + 1) * n_pages]]
        tiles += [new_ref[u, pl.ds(kv, ts, stride=N_KV_HEADS), :], pad]
        return jnp.concatenate(tiles, axis=0).astype(BF16)

    units = [(u, kv) for u in range(n_seq) for kv in range(N_KV_HEADS)]
    s = []
    for u, kv in units:
        q = q_ref[u].astype(F32)
        qg = jnp.concatenate([q[:, (kv * GROUP + g) * HEAD_DIM:(kv * GROUP + g + 1) * HEAD_DIM]
                              for g in range(GROUP)], axis=0).astype(BF16)
        bias = jnp.concatenate([b_ref[u]] * GROUP, axis=0)
        s.append(_dot_nt(qg, head_rows(k_refs, kn_ref, u, kv)) + bias)
    p_ = [jnp.exp2(x - jnp.max(x, axis=1, keepdims=True)) for x in s]
    for (u, kv), pr in zip(units, p_):
        o = _dot(pr.astype(BF16), head_rows(v_refs, vn_ref, u, kv)) / jnp.sum(pr, axis=1, keepdims=True)
        for g in range(GROUP):
            hh = kv * GROUP + g
            o_ref[u, :, hh * HEAD_DIM:(hh + 1) * HEAD_DIM] = o[g * ts:(g + 1) * ts, :]


def _sample_attention(page_table, q, bias, k_new, v_new, cache_k, cache_v):
    nb, ts, _ = q.shape
    n_pages = page_table.shape[1]
    n_seq = ATTN_SEQS if nb % ATTN_SEQS == 0 else 1

    def page_spec(u, p):
        return pl.BlockSpec((1, PAGE_SIZE * N_KV_HEADS, HEAD_DIM), lambda b, pt: (pt[b * n_seq + u, p], 0, 0))

    pages = [page_spec(u, p) for u in range(n_seq) for p in range(n_pages)]
    grid_spec = pltpu.PrefetchScalarGridSpec(
        num_scalar_prefetch=1,
        grid=(nb // n_seq,),
        in_specs=[pl.BlockSpec((n_seq, ts, D_MODEL), lambda b, pt: (b, 0, 0)),
                  pl.BlockSpec((n_seq, ts, (n_pages + 1) * PAGE_SIZE), lambda b, pt: (b, 0, 0)),
                  pl.BlockSpec((n_seq, ts * N_KV_HEADS, HEAD_DIM), lambda b, pt: (b, 0, 0)),
                  pl.BlockSpec((n_seq, ts * N_KV_HEADS, HEAD_DIM), lambda b, pt: (b, 0, 0))] + pages * 2,
        out_specs=pl.BlockSpec((n_seq, ts, D_MODEL), lambda b, pt: (b, 0, 0)),
    )
    return pl.pallas_call(
        functools.partial(_sample_attn_kernel, n_pages=n_pages, ts=ts, n_seq=n_seq),
        grid_spec=grid_spec,
        out_shape=jax.ShapeDtypeStruct((nb, ts, D_MODEL), F32),
        compiler_params=pltpu.CompilerParams(dimension_semantics=("arbitrary",), vmem_limit_bytes=VMEM_LIMIT),
        name="sample_attention",
    )(page_table, q, bias, k_new, v_new, *([cache_k] * (n_seq * n_pages)), *([cache_v] * (n_seq * n_pages)))


def _ffn_kernel(x_ref, oa_ref, ga_ref, cm_ref, p_ref, wout_ref, gffn_ref, wg_ref, wu_ref, wd_ref,
                gple_ref, wple_ref, wpg_ref, gfin_ref, y_ref, x1_scr, h_scr, acc_scr):
    f = pl.program_id(1)

    @pl.when(f == 0)
    def _():
        merged = ga_ref[...] * oa_ref[...] + cm_ref[...]
        x1 = x_ref[...] + _dot(merged.astype(BF16), wout_ref[...])
        x1_scr[...] = x1
        h_scr[...] = _rmsnorm(x1, gffn_ref[...]).astype(BF16)
        acc_scr[...] = jnp.zeros_like(acc_scr)

    h = h_scr[...]
    g = _dot(h, wg_ref[...])
    u = _dot(h, wu_ref[...])
    acc_scr[...] += _dot((g * jax.nn.sigmoid(g) * u).astype(BF16), wd_ref[...])

    @pl.when(f == pl.num_programs(1) - 1)
    def _():
        x2 = x1_scr[...] + acc_scr[...]
        gate = jax.nn.sigmoid(_dot(_rmsnorm(x2, gple_ref[...]).astype(BF16), wpg_ref[...]))
        x3 = x2 + _dot(p_ref[...].astype(BF16), wple_ref[...]) * gate
        y_ref[...] = _rmsnorm(x3, gfin_ref[...])


def _ffn(x, oa, ga, cm, p, wout, gffn, wg, wu, wd, gple, wple, wpg, gfin, *, tm, tf):
    n = x.shape[0]
    tm = min(tm, n)
    assert n % tm == 0 and D_FF % tf == 0
    tok = lambda w: pl.BlockSpec((tm, w), lambda i, f: (i, 0))
    vec = lambda a: a.reshape(1, D_MODEL)
    return pl.pallas_call(
        _ffn_kernel,
        grid=(n // tm, D_FF // tf),
        in_specs=[tok(D_MODEL), tok(D_MODEL), tok(D_MODEL), tok(D_MODEL), tok(D_PLE),
                  _const_spec((D_MODEL, D_MODEL)), _const_spec((1, D_MODEL)),
                  pl.BlockSpec((D_MODEL, tf), lambda i, f: (0, f)),
                  pl.BlockSpec((D_MODEL, tf), lambda i, f: (0, f)),
                  pl.BlockSpec((tf, D_MODEL), lambda i, f: (f, 0)),
                  _const_spec((1, D_MODEL)), _const_spec((D_PLE, D_MODEL)), _const_spec((D_MODEL, D_MODEL)),
                  _const_spec((1, D_MODEL))],
        out_specs=tok(D_MODEL),
        out_shape=jax.ShapeDtypeStruct((n, D_MODEL), F32),
        scratch_shapes=[pltpu.VMEM((tm, D_MODEL), F32), pltpu.VMEM((tm, D_MODEL), BF16),
                        pltpu.VMEM((tm, D_MODEL), F32)],
        compiler_params=pltpu.CompilerParams(dimension_semantics=("arbitrary", "arbitrary"),
                                             vmem_limit_bytes=VMEM_LIMIT),
        name="ffn",
    )(x, oa, ga, cm, p, wout, vec(gffn), wg, wu, wd, vec(gple), wple, wpg, vec(gfin))


def _split_w_in(w_in):
    n_att = N_HEADS * HEAD_DIM + 2 * N_KV_HEADS * HEAD_DIM + N_IDX_HEADS * IDX_DIM + IDX_DIM + N_IDX_HEADS
    return w_in[:, :W1_COLS].astype(BF16), w_in[:, n_att:].astype(BF16)


def kernel(x_prompt, x_sample, cache_k, cache_v, cache_kidx, state_conv, page_table, p_prompt, p_sample, norm_mix, w_in, conv_w, w_out, norm_ffn, w_gate_up, w_down, norm_ple, w_ple, w_ple_gate, norm_final):
    Bp, Tp, _ = x_prompt.shape
    Bs, Ts, _ = x_sample.shape
    n_pages = page_table.shape[1]
    past_len = n_pages * PAGE_SIZE
    n_phys = cache_k.shape[1]
    assert w_in.shape[0] == 1, "one layer: the per-layer tensors are indexed at l = 0 below"
    assert Ts == SUBLANES and cache_k.shape[2] == PAGE_SIZE
    l = 0

    w1, w2 = _split_w_in(w_in[l])
    wout = w_out[l].astype(BF16)
    wg = w_gate_up[l][:, :D_FF].astype(BF16)
    wu = w_gate_up[l][:, D_FF:].astype(BF16)
    wd = w_down[l].astype(BF16)
    wple = w_ple[l].astype(BF16)
    wpg = w_ple_gate[l].astype(BF16)

    tab_p = _rope_table(jnp.arange(Tp))
    (q_p, k_p, v_p, kbf_p, vt_p, qi_p, ki_p, kibf_p, wit_p, ga_p, cm_p, cnew_p) = _inproj(
        x_prompt, norm_mix[l], w1, w2, conv_w[l], tab_p, None, mode="prompt", tm=min(INPROJ_ROWS, Tp))
    oa_p = _prompt_attention(qi_p, wit_p, q_p, kibf_p, kbf_p, vt_p)
    n_p = Bp * Tp
    y_p = _ffn(x_prompt.reshape(n_p, D_MODEL), oa_p.reshape(n_p, D_MODEL), ga_p.reshape(n_p, D_MODEL),
               cm_p.reshape(n_p, D_MODEL), p_prompt[l].reshape(n_p, D_PLE), wout, norm_ffn[l], wg, wu, wd,
               norm_ple[l], wple, wpg, norm_final, tm=512, tf=D_FF // 2)

    n_s = Bs * Ts
    tm_s = min(INPROJ_ROWS, n_s)
    tab_s = _rope_table(past_len + (jnp.arange(tm_s) % Ts))
    prev = jnp.concatenate([state_conv[l], jnp.zeros((Bs, Ts - (CONV_WIDTH - 1), D_CONV), F32)], axis=1)
    (q_s, k_s, v_s, qi_s, ki_s, wit_s, ga_s, cm_s, u_s) = _inproj(
        x_sample.reshape(n_s // tm_s, tm_s, D_MODEL), norm_mix[l], w1, w2, conv_w[l], tab_s,
        prev.reshape(n_s // tm_s, tm_s, D_CONV), mode="sample", tm=tm_s)
    wcol = wit_s.transpose(1, 0, 2).reshape(N_IDX_HEADS, Bs, Ts).transpose(1, 0, 2).reshape(Bs, N_IDX_HEADS * Ts, 1)
    scores = _sample_scores(page_table, qi_s.reshape(Bs, Ts, -1), wcol, ki_s.reshape(Bs, Ts, IDX_DIM),
                            jnp.swapaxes(cache_kidx[l], 1, 2))
    n_sel = min(TOPK_MAX, (past_len + Ts) // 4)
    bias = _sample_select(scores.reshape(n_s, -1), n_sel, 256)
    oa_s = _sample_attention(page_table, q_s.reshape(Bs, Ts, D_MODEL), bias.reshape(Bs, Ts, -1),
                             k_s.reshape(Bs, Ts * N_KV_HEADS, HEAD_DIM), v_s.reshape(Bs, Ts * N_KV_HEADS, HEAD_DIM),
                             cache_k[l].reshape(n_phys, PAGE_SIZE * N_KV_HEADS, HEAD_DIM),
                             cache_v[l].reshape(n_phys, PAGE_SIZE * N_KV_HEADS, HEAD_DIM))
    y_s = _ffn(x_sample.reshape(n_s, D_MODEL), oa_s.reshape(n_s, D_MODEL), ga_s.reshape(n_s, D_MODEL),
               cm_s.reshape(n_s, D_MODEL), p_sample[l].reshape(n_s, D_PLE), wout, norm_ffn[l], wg, wu, wd,
               norm_ple[l], wple, wpg, norm_final, tm=512, tf=D_FF // 2)

    return (y_p.reshape(Bp, Tp, D_MODEL), y_s.reshape(Bs, Ts, D_MODEL),
            k_p.reshape(1, Bp, Tp, N_KV_HEADS, HEAD_DIM), v_p.reshape(1, Bp, Tp, N_KV_HEADS, HEAD_DIM),
            ki_p.reshape(1, Bp, Tp, IDX_DIM), cnew_p.reshape(1, Bp, CONV_WIDTH - 1, D_CONV),
            k_s.reshape(1, Bs, Ts, N_KV_HEADS, HEAD_DIM), v_s.reshape(1, Bs, Ts, N_KV_HEADS, HEAD_DIM),
            ki_s.reshape(1, Bs, Ts, IDX_DIM),
            u_s.reshape(Bs, Ts, D_CONV)[:, Ts - (CONV_WIDTH - 1):, :].reshape(1, Bs, CONV_WIDTH - 1, D_CONV))
```

```python
import functools

import jax
import jax.numpy as jnp
from jax import lax
from jax.experimental import pallas as pl
from jax.experimental.pallas import tpu as pltpu

D_MODEL = 1024
N_HEADS = 8
N_KV_HEADS = 2
GROUP = N_HEADS // N_KV_HEADS
HEAD_DIM = 128
N_IDX_HEADS = 8
IDX_DIM = 64
IDX_SCALE = (N_IDX_HEADS * IDX_DIM) ** -0.5
QK_SCALE = HEAD_DIM ** -0.5
TOPK_MAX = 256
D_CONV = D_MODEL
CONV_WIDTH = 3
D_FF = 2816
D_PLE = 256
PAGE_SIZE = 128
ROPE_THETA = 10000.0
EPS = 1e-6

LANES = 128
SUBLANES = 8
KEY_STEP = 256
Q_BLOCK = 128
COUNT_STEP = 512
COUNT_LANES = 8
VT_ROWS = HEAD_DIM + 16
CONV_SLAB = 256
INPROJ_ROWS = 512
SCORE_SEQS = 4
ATTN_SEQS = 2
VMEM_LIMIT = 56 * 1024 * 1024

W1_COLS = N_HEADS * HEAD_DIM + 2 * N_KV_HEADS * HEAD_DIM + N_IDX_HEADS * IDX_DIM + LANES
W2_COLS = 5 * D_MODEL

F32 = jnp.float32
BF16 = jnp.bfloat16
NEG_INF = float("-inf")
LOWEST = -3.4028234663852886e38
TINY = 1.1754943508222875e-38
KEY_LOWEST = -0x7F800000
KEY_INF = 0x7F800000
KEY_TINY = 0x00800000
SEARCH_BLIND = 24
M_FLOOR = -1e30
LOG2E = 1.4426950408889634


def _dot(a, b):
    return jnp.dot(a, b, preferred_element_type=F32)


def _dot_nt(a, b):
    return lax.dot_general(a, b, (((1,), (1,)), ((), ())), preferred_element_type=F32)


def _rmsnorm(x, g):
    var = jnp.mean(x * x, axis=-1, keepdims=True)
    return (x * lax.rsqrt(var + EPS)) * g


def _rope_table(pos):
    def tab(half):
        freqs = ROPE_THETA ** (-jnp.arange(half, dtype=F32) / half)
        ang = pos.astype(F32)[:, None] * freqs[None, :]
        return [jnp.cos(ang), jnp.sin(ang)]
    return jnp.concatenate(tab(HEAD_DIM // 2) + tab(IDX_DIM // 2), axis=1)


def _rope128(x, cos, sin):
    return x * cos + pltpu.roll(x, HEAD_DIM // 2, axis=1) * sin


def _rope64(x, cos, sin, first_half):
    partner = jnp.where(first_half, pltpu.roll(x, LANES - IDX_DIM // 2, axis=1), pltpu.roll(x, IDX_DIM // 2, axis=1))
    return x * cos + partner * sin


def _inproj_kernel(*refs, mode, tm):
    if mode == "prompt":
        (x_ref, g_ref, w1_ref, w2_ref, cw_ref, tab_ref,
         q_ref, k_ref, v_ref, kbf_ref, vt_ref, qi_ref, ki_ref, kibf_ref, wit_ref, ga_ref, cm_ref, cnew_ref,
         carry_ref) = refs
    else:
        (x_ref, g_ref, w1_ref, w2_ref, cw_ref, tab_ref, prev_ref,
         q_ref, k_ref, v_ref, qi_ref, ki_ref, wit_ref, ga_ref, cm_ref, u_ref) = refs

    x = x_ref[0]
    h = _rmsnorm(x, g_ref[...]).astype(BF16)
    z1 = _dot(h, w1_ref[...])
    tab = tab_ref[...]
    h128, h64 = HEAD_DIM // 2, IDX_DIM // 2
    c, sn = tab[:, 0:h128], tab[:, h128:2 * h128]
    cos128, sin128 = jnp.concatenate([c, c], axis=1), jnp.concatenate([-sn, sn], axis=1)
    c, sn = tab[:, 2 * h128:2 * h128 + h64], tab[:, 2 * h128 + h64:2 * h128 + 2 * h64]
    cos64 = jnp.concatenate([c, c] * (LANES // IDX_DIM), axis=1)
    sin64 = jnp.concatenate([-sn, sn] * (LANES // IDX_DIM), axis=1)
    lane = lax.broadcasted_iota(jnp.int32, (tm, LANES), 1)
    first_half = (lane % IDX_DIM) < (IDX_DIM // 2)

    off = 0
    for hh in range(N_HEADS):
        sl = z1[:, off:off + HEAD_DIM]
        q_ref[0, :, hh * HEAD_DIM:(hh + 1) * HEAD_DIM] = (_rope128(sl, cos128, sin128) * (QK_SCALE * LOG2E)).astype(BF16)
        off += HEAD_DIM
    for hh in range(N_KV_HEADS):
        kr = _rope128(z1[:, off:off + HEAD_DIM], cos128, sin128)
        k_ref[0, pl.ds(hh, tm, stride=N_KV_HEADS), :] = kr
        if mode == "prompt":
            kbf_ref[0, :, hh * HEAD_DIM:(hh + 1) * HEAD_DIM] = kr.astype(BF16)
        off += HEAD_DIM
    v = z1[:, off:off + N_KV_HEADS * HEAD_DIM]
    for hh in range(N_KV_HEADS):
        v_ref[0, pl.ds(hh, tm, stride=N_KV_HEADS), :] = v[:, hh * HEAD_DIM:(hh + 1) * HEAD_DIM]
    if mode == "prompt":
        for c in range(tm // KEY_STEP):
            vt = v[c * KEY_STEP:(c + 1) * KEY_STEP, :].T
            ones_row = jnp.where(lax.broadcasted_iota(jnp.int32, (VT_ROWS - HEAD_DIM, KEY_STEP), 0) == 0, 1.0, 0.0)
            for hh in range(N_KV_HEADS):
                vt_ref[0, c, hh * VT_ROWS:(hh + 1) * VT_ROWS, :] = jnp.concatenate(
                    [vt[hh * HEAD_DIM:(hh + 1) * HEAD_DIM, :], ones_row], axis=0).astype(BF16)
    off += N_KV_HEADS * HEAD_DIM
    for hh in range(N_IDX_HEADS * IDX_DIM // LANES):
        sl = z1[:, off:off + LANES]
        qi_ref[0, :, hh * LANES:(hh + 1) * LANES] = _rope64(sl, cos64, sin64, first_half).astype(BF16)
        off += LANES
    kiw = z1[:, off:off + LANES]
    kir = _rope64(kiw, cos64, sin64, first_half)[:, 0:IDX_DIM]
    ki_ref[0] = kir
    if mode == "prompt":
        kibf_ref[0] = kir.astype(BF16)
    wit_ref[0] = kiw.T[IDX_DIM:IDX_DIM + N_IDX_HEADS, :] * IDX_SCALE

    if mode == "prompt":
        @pl.when(pl.program_id(1) == 0)
        def _():
            carry_ref[...] = jnp.zeros_like(carry_ref)

    row = lax.broadcasted_iota(jnp.int32, (tm, CONV_SLAB), 0)
    for c in range(D_CONV // CONV_SLAB):
        cols = slice(c * CONV_SLAB, (c + 1) * CONV_SLAB)
        bg, cg, xc, ga, gb = (_dot(h, w2_ref[:, k * D_MODEL + c * CONV_SLAB:k * D_MODEL + (c + 1) * CONV_SLAB])
                              for k in range(5))
        u = cg * xc
        r1 = pltpu.roll(u, 1, axis=0)
        r2 = pltpu.roll(u, 2, axis=0)
        if mode == "prompt":
            c0 = carry_ref[0:1, cols]
            c1 = carry_ref[1:2, cols]
            um1 = jnp.where(row == 0, c1, r1)
            um2 = jnp.where(row == 0, c0, jnp.where(row == 1, c1, r2))
            carry_ref[0:2, cols] = u[tm - 2:tm, :]
            cnew_ref[0, :, cols] = u[tm - 2:tm, :]
        else:
            prev = prev_ref[0, :, cols]
            seq_row = row % SUBLANES
            um1 = jnp.where(seq_row == 0, pltpu.roll(prev, tm - 1, axis=0), r1)
            um2 = jnp.where(seq_row < 2, prev, r2)
            u_ref[0, :, cols] = u
        cw = cw_ref[:, cols]
        conv = cw[0:1, :] * um2 + cw[1:2, :] * um1 + cw[2:3, :] * u
        ga_ref[0, :, cols] = jax.nn.sigmoid(ga)
        cm_ref[0, :, cols] = jax.nn.sigmoid(gb) * (bg * conv)


def _const_spec(shape):
    nd = len(shape)
    return pl.BlockSpec(shape, lambda *_: (0,) * nd, pipeline_mode=pl.Buffered(1))


def _inproj(x, norm_g, w1, w2, conv_w, tab, prev, *, mode, tm):
    B, T, _ = x.shape
    assert T % tm == 0 and tm % KEY_STEP == 0 and D_CONV % CONV_SLAB == 0
    nt = T // tm
    tok = lambda w: pl.BlockSpec((1, tm, w), lambda b, t: (b, t, 0))
    in_specs = [tok(D_MODEL), _const_spec((1, D_MODEL)), _const_spec((D_MODEL, W1_COLS)),
                _const_spec((D_MODEL, W2_COLS)), _const_spec((CONV_WIDTH, D_CONV)),
                pl.BlockSpec((tm, HEAD_DIM + IDX_DIM), lambda b, t: (t, 0))]
    args = [x, norm_g.reshape(1, D_MODEL), w1, w2, conv_w, tab]
    kvw = N_KV_HEADS * HEAD_DIM
    qiw = N_IDX_HEADS * IDX_DIM
    wit_spec = pl.BlockSpec((1, N_IDX_HEADS, tm), lambda b, t: (b, 0, t))
    kv_spec = pl.BlockSpec((1, tm * N_KV_HEADS, HEAD_DIM), lambda b, t: (b, t, 0))
    if mode == "prompt":
        out_shape = [
            jax.ShapeDtypeStruct((B, T, D_MODEL), BF16),
            jax.ShapeDtypeStruct((B, T * N_KV_HEADS, HEAD_DIM), F32),
            jax.ShapeDtypeStruct((B, T * N_KV_HEADS, HEAD_DIM), F32),
            jax.ShapeDtypeStruct((B, T, kvw), BF16),
            jax.ShapeDtypeStruct((B, T // KEY_STEP, N_KV_HEADS * VT_ROWS, KEY_STEP), BF16),
            jax.ShapeDtypeStruct((B, T, qiw), BF16),
            jax.ShapeDtypeStruct((B, T, IDX_DIM), F32),
            jax.ShapeDtypeStruct((B, T, IDX_DIM), BF16),
            jax.ShapeDtypeStruct((B, N_IDX_HEADS, T), F32),
            jax.ShapeDtypeStruct((B, T, D_MODEL), F32),
            jax.ShapeDtypeStruct((B, T, D_MODEL), F32),
            jax.ShapeDtypeStruct((B, CONV_WIDTH - 1, D_CONV), F32),
        ]
        out_specs = [tok(D_MODEL), kv_spec, kv_spec, tok(kvw),
                     pl.BlockSpec((1, tm // KEY_STEP, N_KV_HEADS * VT_ROWS, KEY_STEP), lambda b, t: (b, t, 0, 0)),
                     tok(qiw), tok(IDX_DIM), tok(IDX_DIM), wit_spec, tok(D_MODEL), tok(D_MODEL),
                     pl.BlockSpec((1, CONV_WIDTH - 1, D_CONV), lambda b, t: (b, 0, 0))]
        scratch = [pltpu.VMEM((SUBLANES, D_CONV), F32)]
    else:
        in_specs.append(tok(D_CONV))
        args.append(prev)
        out_shape = [
            jax.ShapeDtypeStruct((B, T, D_MODEL), BF16),
            jax.ShapeDtypeStruct((B, T * N_KV_HEADS, HEAD_DIM), F32),
            jax.ShapeDtypeStruct((B, T * N_KV_HEADS, HEAD_DIM), F32),
            jax.ShapeDtypeStruct((B, T, qiw), BF16),
            jax.ShapeDtypeStruct((B, T, IDX_DIM), F32),
            jax.ShapeDtypeStruct((B, N_IDX_HEADS, T), F32),
            jax.ShapeDtypeStruct((B, T, D_MODEL), F32),
            jax.ShapeDtypeStruct((B, T, D_MODEL), F32),
            jax.ShapeDtypeStruct((B, T, D_CONV), F32),
        ]
        out_specs = [tok(D_MODEL), kv_spec, kv_spec, tok(qiw), tok(IDX_DIM), wit_spec,
                     tok(D_MODEL), tok(D_MODEL), tok(D_CONV)]
        scratch = []
    return pl.pallas_call(
        functools.partial(_inproj_kernel, mode=mode, tm=tm),
        grid=(B, nt),
        in_specs=in_specs,
        out_specs=out_specs,
        out_shape=out_shape,
        scratch_shapes=scratch,
        compiler_params=pltpu.CompilerParams(dimension_semantics=("arbitrary", "arbitrary"),
                                             vmem_limit_bytes=VMEM_LIMIT),
        name="inproj_" + mode,
    )(*args)


def _key_value(key):
    return lax.bitcast_convert_type(key ^ ((key >> 31) & 0x7FFFFFFF), F32)


def _search_keys(count_ge, kk, n_all):
    lo0 = jnp.full(kk.shape, KEY_LOWEST, jnp.int32)
    hi0 = jnp.full(kk.shape, KEY_INF, jnp.int32)
    n_lo0 = n_all
    def body(_, c):
        lo, hi, n_lo, n_hi, done = c
        mid = (lo & hi) + ((lo ^ hi) >> 1)
        stuck = (mid == lo) | ((lo >= 0) & (hi <= KEY_TINY))
        n_mid = count_ge(mid)
        upd = (done < 0.5) & jnp.logical_not(stuck)
        go_lo = n_mid >= kk
        up_lo = upd & go_lo
        lo = jnp.where(up_lo, mid, lo)
        n_lo = jnp.where(up_lo, n_mid, n_lo)
        up_hi = upd & jnp.logical_not(go_lo)
        hi = jnp.where(up_hi, mid, hi)
        n_hi = jnp.where(up_hi, n_mid, n_hi)
        done = jnp.where(stuck | (upd & (n_mid == kk)), 1.0, done)
        return lo, hi, n_lo, n_hi, done

    done0 = jnp.where(n_lo0 <= kk, 1.0, 0.0)
    state = (lo0, hi0, n_lo0, jnp.zeros_like(kk), done0)
    state = lax.fori_loop(0, SEARCH_BLIND, body, state)
    lo, hi, n_lo, n_hi, _ = lax.while_loop(lambda c: jnp.min(c[4]) < 0.5, lambda c: body(0, body(0, c)), state)
    return _key_value(lo), _key_value(hi), n_lo, n_hi


def _resolve_ties(count_ge, kk, n_lo, n_hi, tied, top):
    def by_position():
        return jnp.zeros_like(kk), jnp.where(tied, TINY, -LOWEST), n_hi

    def by_value():
        lo2, hi2, _, n_hi2 = _search_keys(count_ge, kk, n_lo)
        return lo2, hi2, n_hi2

    return lax.cond(jnp.max(jnp.where(tied & (top > 0.0), 1.0, 0.0)) > 0.5, by_value, by_position)


def _spread_ties(sc, lo, hi):
    return jnp.where(sc >= hi, -LOWEST, jnp.where(sc >= lo, sc - lo, NEG_INF))


def _fold_rows(x, op):
    return op(x.reshape(x.shape[0] // SUBLANES, SUBLANES, x.shape[1]), axis=0)


def _prompt_attn_kernel(qi_ref, wit_ref, q_ref, ki_ref, k_ref, vt_ref, o_ref,
                        sc_scr, b_scr, d_a, d_b, s_a, s_b, p_a, p_b, acc_scr, *, n_sel, count_step):
    j = pl.program_id(1)
    per_step = KEY_STEP // Q_BLOCK
    n_steps = (j + per_step) // per_step
    n_pairs = n_steps // 2
    odd = n_steps % 2 == 1
    last = n_steps - 1
    per_count = count_step // KEY_STEP
    n_count = (n_steps + per_count - 1) // per_count
    qi = qi_ref[0]
    q_stack = jnp.concatenate([qi[:, h * IDX_DIM:(h + 1) * IDX_DIM] for h in range(N_IDX_HEADS)], axis=0)
    wit = wit_ref[0]
    q = q_ref[0]
    q_grp = [jnp.concatenate([q[:, (kv * GROUP + g) * HEAD_DIM:(kv * GROUP + g + 1) * HEAD_DIM]
                              for g in range(GROUP)], axis=0) for kv in range(N_KV_HEADS)]
    key_pos = lax.broadcasted_iota(jnp.int32, (KEY_STEP, Q_BLOCK), 0)
    q_pos = lax.broadcasted_iota(jnp.int32, (KEY_STEP, Q_BLOCK), 1) + j * Q_BLOCK

    def step(i):
        return pl.ds(pl.multiple_of(i * KEY_STEP, KEY_STEP), KEY_STEP)

    def idx_dots(i):
        return _dot_nt(ki_ref[0, step(i), :], q_stack)

    def score_trip(i, d_cur, d_nxt):
        d_nxt[...] = idx_dots(jnp.minimum(i + 1, last))
        acc = wit[0:1, :] * jnp.maximum(d_cur[:, 0:Q_BLOCK], 0.0)
        for h in range(1, N_IDX_HEADS):
            acc = acc + wit[h:h + 1, :] * jnp.maximum(d_cur[:, h * Q_BLOCK:(h + 1) * Q_BLOCK], 0.0)
        sc_scr[step(i), :] = jnp.where(key_pos + i * KEY_STEP <= q_pos, acc, NEG_INF)

    def logits(i, kv):
        return _dot_nt(k_ref[0, step(i), kv * HEAD_DIM:(kv + 1) * HEAD_DIM], q_grp[kv])

    d_a[...] = idx_dots(0)
    for kv in range(N_KV_HEADS):
        s_a[kv] = logits(0, kv)

    def score_pair(pi, c):
        score_trip(2 * pi, d_a, d_b)
        score_trip(2 * pi + 1, d_b, d_a)
        return c

    lax.fori_loop(0, n_pairs, score_pair, 0)

    @pl.when(odd)
    def _():
        score_trip(last, d_a, d_b)

    def fill(i, c):
        sc_scr[step(i), :] = jnp.full((KEY_STEP, Q_BLOCK), NEG_INF, F32)
        return c

    lax.fori_loop(n_steps, n_count * per_count, fill, 0)

    def count_ge(t):
        tb = jnp.broadcast_to(_key_value(t), (SUBLANES, Q_BLOCK))

        def body(i, accs):
            accs = list(accs)
            base = pl.multiple_of(i * count_step, count_step)
            for r in range(count_step // SUBLANES):
                sc = sc_scr[pl.ds(base + r * SUBLANES, SUBLANES), :]
                accs[r % COUNT_LANES] = accs[r % COUNT_LANES] + jnp.where(sc >= tb, 1.0, 0.0)
            return tuple(accs)

        accs = lax.fori_loop(0, n_count, body, tuple(jnp.zeros((SUBLANES, Q_BLOCK), F32) for _ in range(COUNT_LANES)))
        return jnp.sum(functools.reduce(lambda a, b: a + b, accs), axis=0, keepdims=True)

    n_valid = (j * Q_BLOCK + 1 + lax.broadcasted_iota(jnp.int32, (1, Q_BLOCK), 1)).astype(F32)
    kk = jnp.minimum(n_valid, float(n_sel))
    lo, hi, n_lo, n_hi = lax.cond((j + 1) * Q_BLOCK <= n_sel,
                                  lambda: (jnp.full_like(kk, LOWEST), jnp.full_like(kk, jnp.inf), n_valid,
                                           jnp.zeros_like(kk)),
                                  lambda: _search_keys(count_ge, kk, n_valid))

    def mask_plain(i, c):
        b_scr[step(i), :] = jnp.where(sc_scr[step(i), :] >= lo, 0.0, NEG_INF)
        return c

    def spread(i, top):
        sp = _spread_ties(sc_scr[step(i), :], lo, hi)
        sc_scr[step(i), :] = sp
        return jnp.maximum(top, _fold_rows(jnp.where(sp < -LOWEST, sp, NEG_INF), jnp.max))

    tied = n_lo > kk
    any_tie = jnp.max(jnp.where(tied, 1.0, 0.0)) > 0.5

    @pl.when(any_tie)
    def _():
        top = lax.fori_loop(0, n_steps, spread, jnp.full((SUBLANES, Q_BLOCK), NEG_INF, F32))
        lo2, hi2, n_hi2 = _resolve_ties(count_ge, kk, n_lo, n_hi, tied, jnp.max(top, axis=0, keepdims=True))
        tri = jnp.where(lax.broadcasted_iota(jnp.int32, (KEY_STEP, KEY_STEP), 1)
                        <= lax.broadcasted_iota(jnp.int32, (KEY_STEP, KEY_STEP), 0), 1.0, 0.0).astype(BF16)

        def mask_ties(i, run):
            sc = sc_scr[step(i), :]
            above = sc >= hi2
            elig = (sc >= lo2) & jnp.logical_not(above)
            rank = run + _dot(tri, jnp.where(elig, 1.0, 0.0).astype(BF16))
            b_scr[step(i), :] = jnp.where(above | (elig & (rank <= kk - n_hi2)), 0.0, NEG_INF)
            return rank[KEY_STEP - 1:KEY_STEP, :]

        lax.fori_loop(0, n_steps, mask_ties, jnp.zeros((1, Q_BLOCK), F32))

    @pl.when(jnp.logical_not(any_tie))
    def _():
        lax.fori_loop(0, n_steps, mask_plain, 0)

    def pv_dot(i, kv, p_ref):
        return _dot(vt_ref[0, i, kv * VT_ROWS:(kv + 1) * VT_ROWS, :], p_ref[kv])

    def attn_trip(i, s_cur, s_nxt, p_cur, p_prv, m):
        pv = [pv_dot(jnp.maximum(i - 1, 0), kv, p_prv) for kv in range(N_KV_HEADS)]
        for kv in range(N_KV_HEADS):
            s_nxt[kv] = logits(jnp.minimum(i + 1, last), kv)
        b = b_scr[step(i), :]
        m_rows = []
        for kv in range(N_KV_HEADS):
            ps, alphas = [], []
            for g in range(GROUP):
                h = kv * GROUP + g
                sg = s_cur[kv, :, g * Q_BLOCK:(g + 1) * Q_BLOCK] + b
                m_old = m[h:h + 1, :]
                m_new = jnp.maximum(m_old, jnp.max(_fold_rows(sg, jnp.max), axis=0, keepdims=True))
                ps.append(jnp.exp2(sg - m_new).astype(BF16))
                alphas.append(jnp.exp2(m_old - m_new))
                m_rows.append(m_new)
            p_cur[kv] = jnp.concatenate(ps, axis=1)
            acc_scr[kv] = (acc_scr[kv] + pv[kv]) * jnp.concatenate(alphas, axis=1)
        return jnp.concatenate(m_rows, axis=0)

    acc_scr[...] = jnp.zeros_like(acc_scr)
    p_b[...] = jnp.zeros_like(p_b)

    def attn_pair(pi, m):
        m = attn_trip(2 * pi, s_a, s_b, p_a, p_b, m)
        return attn_trip(2 * pi + 1, s_b, s_a, p_b, p_a, m)

    m_end = lax.fori_loop(0, n_pairs, attn_pair, jnp.full((N_HEADS, Q_BLOCK), M_FLOOR, F32))

    @pl.when(odd)
    def _():
        attn_trip(last, s_a, s_b, p_a, p_b, m_end)
        p_b[...] = p_a[...]

    for h in range(N_HEADS):
        kv, g = divmod(h, GROUP)
        if g == 0:
            acc_kv = acc_scr[kv] + pv_dot(last, kv, p_b)
        cols = slice(g * Q_BLOCK, (g + 1) * Q_BLOCK)
        inv_l = 1.0 / acc_kv[HEAD_DIM:HEAD_DIM + 1, cols]
        o_ref[0, :, h * HEAD_DIM:(h + 1) * HEAD_DIM] = (acc_kv[0:HEAD_DIM, cols] * inv_l).T


def _prompt_attention(qi, wit, q, ki_bf, k_bf, vt):
    B, T, _ = q.shape
    nq = T // Q_BLOCK
    n_sel = min(TOPK_MAX, T // 4)
    kvw = N_KV_HEADS * HEAD_DIM
    assert T % KEY_STEP == 0 and KEY_STEP % Q_BLOCK == 0
    count_step = next(c for c in (COUNT_STEP, COUNT_STEP // 2, KEY_STEP) if T % c == 0)
    gq = GROUP * Q_BLOCK
    return pl.pallas_call(
        functools.partial(_prompt_attn_kernel, n_sel=n_sel, count_step=count_step),
        grid=(B, nq),
        in_specs=[pl.BlockSpec((1, Q_BLOCK, N_IDX_HEADS * IDX_DIM), lambda b, j: (b, j, 0)),
                  pl.BlockSpec((1, N_IDX_HEADS, Q_BLOCK), lambda b, j: (b, 0, j)),
                  pl.BlockSpec((1, Q_BLOCK, D_MODEL), lambda b, j: (b, j, 0)),
                  pl.BlockSpec((1, T, IDX_DIM), lambda b, j: (b, 0, 0)),
                  pl.BlockSpec((1, T, kvw), lambda b, j: (b, 0, 0)),
                  pl.BlockSpec((1, T // KEY_STEP, N_KV_HEADS * VT_ROWS, KEY_STEP), lambda b, j: (b, 0, 0, 0))],
        out_specs=pl.BlockSpec((1, Q_BLOCK, D_MODEL), lambda b, j: (b, j, 0)),
        out_shape=jax.ShapeDtypeStruct((B, T, D_MODEL), F32),
        scratch_shapes=[pltpu.VMEM((T, Q_BLOCK), F32), pltpu.VMEM((T, Q_BLOCK), F32),
                        pltpu.VMEM((KEY_STEP, N_IDX_HEADS * Q_BLOCK), F32),
                        pltpu.VMEM((KEY_STEP, N_IDX_HEADS * Q_BLOCK), F32),
                        pltpu.VMEM((N_KV_HEADS, KEY_STEP, gq), F32), pltpu.VMEM((N_KV_HEADS, KEY_STEP, gq), F32),
                        pltpu.VMEM((N_KV_HEADS, KEY_STEP, gq), BF16), pltpu.VMEM((N_KV_HEADS, KEY_STEP, gq), BF16),
                        pltpu.VMEM((N_KV_HEADS, VT_ROWS, gq), F32)],
        compiler_params=pltpu.CompilerParams(dimension_semantics=("arbitrary", "arbitrary"),
                                             vmem_limit_bytes=VMEM_LIMIT),
        name="prompt_attention",
    )(qi, wit, q, ki_bf, k_bf, vt)


def _sample_score_kernel(pt_ref, qi_ref, wcol_ref, kin_ref, *rest, n_pages, ts, n_seq):
    page_refs, s_ref = rest[:n_seq * n_pages], rest[n_seq * n_pages]
    lane = lax.broadcasted_iota(jnp.int32, (ts, PAGE_SIZE), 1)
    row = lax.broadcasted_iota(jnp.int32, (ts, PAGE_SIZE), 0)
    for q in range(n_seq):
        qi = qi_ref[q].astype(F32)
        q_all = jnp.concatenate([qi[:, h * IDX_DIM:(h + 1) * IDX_DIM] for h in range(N_IDX_HEADS)],
                                axis=0).astype(BF16)
        wcol = wcol_ref[q]

        def score(dots, wcol=wcol):
            r = jnp.maximum(dots, 0.0) * wcol
            acc = r[0:ts]
            for h in range(1, N_IDX_HEADS):
                acc = acc + r[h * ts:(h + 1) * ts]
            return acc

        past = jnp.concatenate([page_refs[q * n_pages + p][0] for p in range(n_pages)], axis=1).astype(BF16)
        s_ref[q, :, 0:n_pages * PAGE_SIZE] = score(_dot(q_all, past))
        new_keys = jnp.concatenate([kin_ref[q], jnp.zeros((PAGE_SIZE - ts, IDX_DIM), F32)], axis=0).astype(BF16)
        s_ref[q, :, n_pages * PAGE_SIZE:(n_pages + 1) * PAGE_SIZE] = jnp.where(
            lane <= row, score(_dot_nt(q_all, new_keys)), NEG_INF)


def _sample_scores(page_table, qi, wcol, ki_bf, cache_kidx):
    nb, ts, _ = qi.shape
    n_pages = page_table.shape[1]
    n_seq = SCORE_SEQS if nb % SCORE_SEQS == 0 else 1

    def page_spec(q, p):
        return pl.BlockSpec((1, IDX_DIM, PAGE_SIZE), lambda b, pt: (pt[b * n_seq + q, p], 0, 0))

    grid_spec = pltpu.PrefetchScalarGridSpec(
        num_scalar_prefetch=1,
        grid=(nb // n_seq,),
        in_specs=[pl.BlockSpec((n_seq, ts, N_IDX_HEADS * IDX_DIM), lambda b, pt: (b, 0, 0)),
                  pl.BlockSpec((n_seq, N_IDX_HEADS * ts, 1), lambda b, pt: (b, 0, 0)),
                  pl.BlockSpec((n_seq, ts, IDX_DIM), lambda b, pt: (b, 0, 0))]
                 + [page_spec(q, p) for q in range(n_seq) for p in range(n_pages)],
        out_specs=pl.BlockSpec((n_seq, ts, (n_pages + 1) * PAGE_SIZE), lambda b, pt: (b, 0, 0)),
    )
    return pl.pallas_call(
        functools.partial(_sample_score_kernel, n_pages=n_pages, ts=ts, n_seq=n_seq),
        grid_spec=grid_spec,
        out_shape=jax.ShapeDtypeStruct((nb, ts, (n_pages + 1) * PAGE_SIZE), F32),
        compiler_params=pltpu.CompilerParams(dimension_semantics=("arbitrary",), vmem_limit_bytes=VMEM_LIMIT),
        name="sample_scores",
    )(page_table, qi, wcol, ki_bf, *([cache_kidx] * (n_seq * n_pages)))


def _sample_select_kernel(s_ref, b_ref, sc_scr, *, n_sel, n_chunks):
    rows = s_ref.shape[0]

    def cols(c):
        return slice(c * LANES, (c + 1) * LANES)

    def counter(ref):
        def count_ge(t):
            tf = _key_value(t)
            acc = jnp.where(ref[:, cols(0)] >= tf, 1.0, 0.0)
            for c in range(1, n_chunks):
                acc = acc + jnp.where(ref[:, cols(c)] >= tf, 1.0, 0.0)
            return jnp.sum(acc, axis=1, keepdims=True)
        return count_ge

    kk = jnp.full((rows, 1), float(n_sel), F32)
    n_all = counter(s_ref)(jnp.full((rows, 1), KEY_LOWEST, jnp.int32))
    lo, hi, n_lo, n_hi = _search_keys(counter(s_ref), kk, n_all)
    tied = n_lo > kk
    any_tie = jnp.max(jnp.where(tied, 1.0, 0.0)) > 0.5

    @pl.when(jnp.logical_not(any_tie))
    def _():
        for c in range(n_chunks):
            b_ref[:, cols(c)] = jnp.where(s_ref[:, cols(c)] >= lo, 0.0, NEG_INF)

    @pl.when(any_tie)
    def _():
        top = jnp.full((rows, LANES), NEG_INF, F32)
        for c in range(n_chunks):
            sp = _spread_ties(s_ref[:, cols(c)], lo, hi)
            sc_scr[:, cols(c)] = sp
            top = jnp.maximum(top, jnp.where(sp < -LOWEST, sp, NEG_INF))
        lo2, hi2, n_hi2 = _resolve_ties(counter(sc_scr), kk, n_lo, n_hi, tied, jnp.max(top, axis=1, keepdims=True))
        tri = jnp.where(lax.broadcasted_iota(jnp.int32, (LANES, LANES), 0)
                        <= lax.broadcasted_iota(jnp.int32, (LANES, LANES), 1), 1.0, 0.0).astype(BF16)
        run = jnp.zeros((rows, 1), F32)
        for c in range(n_chunks):
            sc = sc_scr[:, cols(c)]
            above = sc >= hi2
            elig = (sc >= lo2) & jnp.logical_not(above)
            rank = run + _dot(jnp.where(elig, 1.0, 0.0).astype(BF16), tri)
            b_ref[:, cols(c)] = jnp.where(above | (elig & (rank <= kk - n_hi2)), 0.0, NEG_INF)
            run = rank[:, LANES - 1:LANES]


def _sample_select(scores, n_sel, rows_per_step):
    rows, width = scores.shape
    assert rows % rows_per_step == 0 and width % LANES == 0
    return pl.pallas_call(
        functools.partial(_sample_select_kernel, n_sel=n_sel, n_chunks=width // LANES),
        grid=(rows // rows_per_step,),
        in_specs=[pl.BlockSpec((rows_per_step, width), lambda i: (i, 0))],
        out_specs=pl.BlockSpec((rows_per_step, width), lambda i: (i, 0)),
        out_shape=jax.ShapeDtypeStruct((rows, width), F32),
        scratch_shapes=[pltpu.VMEM((rows_per_step, width), F32)],
        compiler_params=pltpu.CompilerParams(dimension_semantics=("arbitrary",), vmem_limit_bytes=VMEM_LIMIT),
        name="sample_select",
    )(scores)


def _sample_attn_kernel(pt_ref, q_ref, b_ref, kn_ref, vn_ref, *rest, n_pages, ts, n_seq):
    k_refs, v_refs, o_ref = rest[:n_seq * n_pages], rest[n_seq * n_pages:2 * n_seq * n_pages], rest[2 * n_seq * n_pages]
    pad = jnp.zeros((PAGE_SIZE - ts, HEAD_DIM), F32)

    def head_rows(refs, new_ref, u, kv):
        tiles = [r[0, pl.ds(kv, PAGE_SIZE, stride=N_KV_HEADS), :] for r in refs[u * n_pages:(u + 1) * n_pages]]
        tiles += [new_ref[u, pl.ds(kv, ts, stride=N_KV_HEADS), :], pad]
        return jnp.concatenate(tiles, axis=0).astype(BF16)

    units = [(u, kv) for u in range(n_seq) for kv in range(N_KV_HEADS)]
    s = []
    for u, kv in units:
        q = q_ref[u].astype(F32)
        qg = jnp.concatenate([q[:, (kv * GROUP + g) * HEAD_DIM:(kv * GROUP + g + 1) * HEAD_DIM]
                              for g in range(GROUP)], axis=0).astype(BF16)
        bias = jnp.concatenate([b_ref[u]] * GROUP, axis=0)
        s.append(_dot_nt(qg, head_rows(k_refs, kn_ref, u, kv)) + bias)
    p_ = [jnp.exp2(x - jnp.max(x, axis=1, keepdims=True)) for x in s]
    for (u, kv), pr in zip(units, p_):
        o = _dot(pr.astype(BF16), head_rows(v_refs, vn_ref, u, kv)) / jnp.sum(pr, axis=1, keepdims=True)
        for g in range(GROUP):
            hh = kv * GROUP + g
            o_ref[u, :, hh * HEAD_DIM:(hh + 1) * HEAD_DIM] = o[g * ts:(g + 1) * ts, :]


def _sample_attention(page_table, q, bias, k_new, v_new, cache_k, cache_v):
    nb, ts, _ = q.shape
    n_pages = page_table.shape[1]
    n_seq = ATTN_SEQS if nb % ATTN_SEQS == 0 else 1

    def page_spec(u, p):
        return pl.BlockSpec((1, PAGE_SIZE * N_KV_HEADS, HEAD_DIM), lambda b, pt: (pt[b * n_seq + u, p], 0, 0))

    pages = [page_spec(u, p) for u in range(n_seq) for p in range(n_pages)]
    grid_spec = pltpu.PrefetchScalarGridSpec(
        num_scalar_prefetch=1,
        grid=(nb // n_seq,),
        in_specs=[pl.BlockSpec((n_seq, ts, D_MODEL), lambda b, pt: (b, 0, 0)),
                  pl.BlockSpec((n_seq, ts, (n_pages + 1) * PAGE_SIZE), lambda b, pt: (b, 0, 0)),
                  pl.BlockSpec((n_seq, ts * N_KV_HEADS, HEAD_DIM), lambda b, pt: (b, 0, 0)),
                  pl.BlockSpec((n_seq, ts * N_KV_HEADS, HEAD_DIM), lambda b, pt: (b, 0, 0))] + pages * 2,
        out_specs=pl.BlockSpec((n_seq, ts, D_MODEL), lambda b, pt: (b, 0, 0)),
    )
    return pl.pallas_call(
        functools.partial(_sample_attn_kernel, n_pages=n_pages, ts=ts, n_seq=n_seq),
        grid_spec=grid_spec,
        out_shape=jax.ShapeDtypeStruct((nb, ts, D_MODEL), F32),
        compiler_params=pltpu.CompilerParams(dimension_semantics=("arbitrary",), vmem_limit_bytes=VMEM_LIMIT),
        name="sample_attention",
    )(page_table, q, bias, k_new, v_new, *([cache_k] * (n_seq * n_pages)), *([cache_v] * (n_seq * n_pages)))


def _ffn_kernel(x_ref, oa_ref, ga_ref, cm_ref, p_ref, wout_ref, gffn_ref, wg_ref, wu_ref, wd_ref,
                gple_ref, wple_ref, wpg_ref, gfin_ref, y_ref, x1_scr, h_scr, acc_scr):
    f = pl.program_id(1)

    @pl.when(f == 0)
    def _():
        merged = ga_ref[...] * oa_ref[...] + cm_ref[...]
        x1 = x_ref[...] + _dot(merged.astype(BF16), wout_ref[...])
        x1_scr[...] = x1
        h_scr[...] = _rmsnorm(x1, gffn_ref[...]).astype(BF16)
        acc_scr[...] = jnp.zeros_like(acc_scr)

    h = h_scr[...]
    g = _dot(h, wg_ref[...])
    u = _dot(h, wu_ref[...])
    acc_scr[...] += _dot((g * jax.nn.sigmoid(g) * u).astype(BF16), wd_ref[...])

    @pl.when(f == pl.num_programs(1) - 1)
    def _():
        x2 = x1_scr[...] + acc_scr[...]
        gate = jax.nn.sigmoid(_dot(_rmsnorm(x2, gple_ref[...]).astype(BF16), wpg_ref[...]))
        x3 = x2 + _dot(p_ref[...].astype(BF16), wple_ref[...]) * gate
        y_ref[...] = _rmsnorm(x3, gfin_ref[...])


def _ffn(x, oa, ga, cm, p, wout, gffn, wg, wu, wd, gple, wple, wpg, gfin, *, tm, tf):
    n = x.shape[0]
    tm = min(tm, n)
    assert n % tm == 0 and D_FF % tf == 0
    tok = lambda w: pl.BlockSpec((tm, w), lambda i, f: (i, 0))
    vec = lambda a: a.reshape(1, D_MODEL)
    return pl.pallas_call(
        _ffn_kernel,
        grid=(n // tm, D_FF // tf),
        in_specs=[tok(D_MODEL), tok(D_MODEL), tok(D_MODEL), tok(D_MODEL), tok(D_PLE),
                  _const_spec((D_MODEL, D_MODEL)), _const_spec((1, D_MODEL)),
                  pl.BlockSpec((D_MODEL, tf), lambda i, f: (0, f)),
                  pl.BlockSpec((D_MODEL, tf), lambda i, f: (0, f)),
                  pl.BlockSpec((tf, D_MODEL), lambda i, f: (f, 0)),
                  _const_spec((1, D_MODEL)), _const_spec((D_PLE, D_MODEL)), _const_spec((D_MODEL, D_MODEL)),
                  _const_spec((1, D_MODEL))],
        out_specs=tok(D_MODEL),
        out_shape=jax.ShapeDtypeStruct((n, D_MODEL), F32),
        scratch_shapes=[pltpu.VMEM((tm, D_MODEL), F32), pltpu.VMEM((tm, D_MODEL), BF16),
                        pltpu.VMEM((tm, D_MODEL), F32)],
        compiler_params=pltpu.CompilerParams(dimension_semantics=("arbitrary", "arbitrary"),
                                             vmem_limit_bytes=VMEM_LIMIT),
        name="ffn",
    )(x, oa, ga, cm, p, wout, vec(gffn), wg, wu, wd, vec(gple), wple, wpg, vec(gfin))


def _split_w_in(w_in):
    n_att = N_HEADS * HEAD_DIM + 2 * N_KV_HEADS * HEAD_DIM + N_IDX_HEADS * IDX_DIM + IDX_DIM + N_IDX_HEADS
    return w_in[:, :W1_COLS].astype(BF16), w_in[:, n_att:].astype(BF16)


def kernel(x_prompt, x_sample, cache_k, cache_v, cache_kidx, state_conv, page_table, p_prompt, p_sample, norm_mix, w_in, conv_w, w_out, norm_ffn, w_gate_up, w_down, norm_ple, w_ple, w_ple_gate, norm_final):
    Bp, Tp, _ = x_prompt.shape
    Bs, Ts, _ = x_sample.shape
    n_pages = page_table.shape[1]
    past_len = n_pages * PAGE_SIZE
    n_phys = cache_k.shape[1]
    assert w_in.shape[0] == 1, "one layer: the per-layer tensors are indexed at l = 0 below"
    assert Ts == SUBLANES and cache_k.shape[2] == PAGE_SIZE
    l = 0

    w1, w2 = _split_w_in(w_in[l])
    wout = w_out[l].astype(BF16)
    wg = w_gate_up[l][:, :D_FF].astype(BF16)
    wu = w_gate_up[l][:, D_FF:].astype(BF16)
    wd = w_down[l].astype(BF16)
    wple = w_ple[l].astype(BF16)
    wpg = w_ple_gate[l].astype(BF16)

    tab_p = _rope_table(jnp.arange(Tp))
    (q_p, k_p, v_p, kbf_p, vt_p, qi_p, ki_p, kibf_p, wit_p, ga_p, cm_p, cnew_p) = _inproj(
        x_prompt, norm_mix[l], w1, w2, conv_w[l], tab_p, None, mode="prompt", tm=min(INPROJ_ROWS, Tp))
    oa_p = _prompt_attention(qi_p, wit_p, q_p, kibf_p, kbf_p, vt_p)
    n_p = Bp * Tp
    y_p = _ffn(x_prompt.reshape(n_p, D_MODEL), oa_p.reshape(n_p, D_MODEL), ga_p.reshape(n_p, D_MODEL),
               cm_p.reshape(n_p, D_MODEL), p_prompt[l].reshape(n_p, D_PLE), wout, norm_ffn[l], wg, wu, wd,
               norm_ple[l], wple, wpg, norm_final, tm=512, tf=D_FF // 2)

    n_s = Bs * Ts
    tm_s = min(INPROJ_ROWS, n_s)
    tab_s = _rope_table(past_len + (jnp.arange(tm_s) % Ts))
    prev = jnp.concatenate([state_conv[l], jnp.zeros((Bs, Ts - (CONV_WIDTH - 1), D_CONV), F32)], axis=1)
    (q_s, k_s, v_s, qi_s, ki_s, wit_s, ga_s, cm_s, u_s) = _inproj(
        x_sample.reshape(n_s // tm_s, tm_s, D_MODEL), norm_mix[l], w1, w2, conv_w[l], tab_s,
        prev.reshape(n_s // tm_s, tm_s, D_CONV), mode="sample", tm=tm_s)
    wcol = wit_s.transpose(1, 0, 2).reshape(N_IDX_HEADS, Bs, Ts).transpose(1, 0, 2).reshape(Bs, N_IDX_HEADS * Ts, 1)
    scores = _sample_scores(page_table, qi_s.reshape(Bs, Ts, -1), wcol, ki_s.reshape(Bs, Ts, IDX_DIM),
                            jnp.swapaxes(cache_kidx[l], 1, 2))
    n_sel = min(TOPK_MAX, (past_len + Ts) // 4)
    bias = _sample_select(scores.reshape(n_s, -1), n_sel, 256)
    oa_s = _sample_attention(page_table, q_s.reshape(Bs, Ts, D_MODEL), bias.reshape(Bs, Ts, -1),
                             k_s.reshape(Bs, Ts * N_KV_HEADS, HEAD_DIM), v_s.reshape(Bs, Ts * N_KV_HEADS, HEAD_DIM),
                             cache_k[l].reshape(n_phys, PAGE_SIZE * N_KV_HEADS, HEAD_DIM),
                             cache_v[l].reshape(n_phys, PAGE_SIZE * N_KV_HEADS, HEAD_DIM))
    y_s = _ffn(x_sample.reshape(n_s, D_MODEL), oa_s.reshape(n_s, D_MODEL), ga_s.reshape(n_s, D_MODEL),
               cm_s.reshape(n_s, D_MODEL), p_sample[l].reshape(n_s, D_PLE), wout, norm_ffn[l], wg, wu, wd,
               norm_ple[l], wple, wpg, norm_final, tm=512, tf=D_FF // 2)

    return (y_p.reshape(Bp, Tp, D_MODEL), y_s.reshape(Bs, Ts, D_MODEL),
            k_p.reshape(1, Bp, Tp, N_KV_HEADS, HEAD_DIM), v_p.reshape(1, Bp, Tp, N_KV_HEADS, HEAD_DIM),
            ki_p.reshape(1, Bp, Tp, IDX_DIM), cnew_p.reshape(1, Bp, CONV_WIDTH - 1, D_CONV),
            k_s.reshape(1, Bs, Ts, N_KV_HEADS, HEAD_DIM), v_s.reshape(1, Bs, Ts, N_KV_HEADS, HEAD_DIM),
            ki_s.reshape(1, Bs, Ts, IDX_DIM),
            u_s.reshape(Bs, Ts, D_CONV)[:, Ts - (CONV_WIDTH - 1):, :].reshape(1, Bs, CONV_WIDTH - 1, D_CONV))
```

```python
import functools

import jax
import jax.numpy as jnp
from jax import lax
from jax.experimental import pallas as pl
from jax.experimental.pallas import tpu as pltpu

D_MODEL = 1024
N_HEADS = 8
N_KV_HEADS = 2
GROUP = N_HEADS // N_KV_HEADS
HEAD_DIM = 128
N_IDX_HEADS = 8
IDX_DIM = 64
IDX_SCALE = (N_IDX_HEADS * IDX_DIM) ** -0.5
QK_SCALE = HEAD_DIM ** -0.5
TOPK_MAX = 256
D_CONV = D_MODEL
CONV_WIDTH = 3
D_FF = 2816
D_PLE = 256
PAGE_SIZE = 128
ROPE_THETA = 10000.0
EPS = 1e-6

LANES = 128
SUBLANES = 8
KEY_STEP = 256
Q_BLOCK = 128
COUNT_STEP = 512
COUNT_LANES = 8
VT_ROWS = HEAD_DIM + 16
CONV_SLAB = 256
INPROJ_ROWS = 512
SCORE_SEQS = 8
ATTN_SEQS = 4
VMEM_LIMIT = 56 * 1024 * 1024

W1_COLS = N_HEADS * HEAD_DIM + 2 * N_KV_HEADS * HEAD_DIM + N_IDX_HEADS * IDX_DIM + LANES
W2_COLS = 5 * D_MODEL

F32 = jnp.float32
BF16 = jnp.bfloat16
NEG_INF = float("-inf")
LOWEST = -3.4028234663852886e38
TINY = 1.1754943508222875e-38
KEY_LOWEST = -0x7F800000
KEY_INF = 0x7F800000
SEARCH_BLIND = 24
M_FLOOR = -1e30
LOG2E = 1.4426950408889634


def _dot(a, b):
    return jnp.dot(a, b, preferred_element_type=F32)


def _dot_nt(a, b):
    return lax.dot_general(a, b, (((1,), (1,)), ((), ())), preferred_element_type=F32)


def _rmsnorm(x, g):
    var = jnp.mean(x * x, axis=-1, keepdims=True)
    return (x * lax.rsqrt(var + EPS)) * g


def _rope_table(pos):
    def tab(half):
        freqs = ROPE_THETA ** (-jnp.arange(half, dtype=F32) / half)
        ang = pos.astype(F32)[:, None] * freqs[None, :]
        return [jnp.cos(ang), jnp.sin(ang)]
    return jnp.concatenate(tab(HEAD_DIM // 2) + tab(IDX_DIM // 2), axis=1)


def _rope128(x, cos, sin):
    return x * cos + pltpu.roll(x, HEAD_DIM // 2, axis=1) * sin


def _rope64(x, cos, sin, first_half):
    partner = jnp.where(first_half, pltpu.roll(x, LANES - IDX_DIM // 2, axis=1), pltpu.roll(x, IDX_DIM // 2, axis=1))
    return x * cos + partner * sin


def _inproj_kernel(*refs, mode, tm):
    if mode == "prompt":
        (x_ref, g_ref, w1_ref, w2_ref, cw_ref, tab_ref,
         q_ref, k_ref, v_ref, kbf_ref, vt_ref, qi_ref, ki_ref, kibf_ref, wit_ref, ga_ref, cm_ref, cnew_ref,
         carry_ref) = refs
    else:
        (x_ref, g_ref, w1_ref, w2_ref, cw_ref, tab_ref, prev_ref,
         q_ref, k_ref, v_ref, qi_ref, ki_ref, wit_ref, ga_ref, cm_ref, u_ref) = refs

    x = x_ref[0]
    h = _rmsnorm(x, g_ref[...]).astype(BF16)
    z1 = _dot(h, w1_ref[...])
    tab = tab_ref[...]
    h128, h64 = HEAD_DIM // 2, IDX_DIM // 2
    c, sn = tab[:, 0:h128], tab[:, h128:2 * h128]
    cos128, sin128 = jnp.concatenate([c, c], axis=1), jnp.concatenate([-sn, sn], axis=1)
    c, sn = tab[:, 2 * h128:2 * h128 + h64], tab[:, 2 * h128 + h64:2 * h128 + 2 * h64]
    cos64 = jnp.concatenate([c, c] * (LANES // IDX_DIM), axis=1)
    sin64 = jnp.concatenate([-sn, sn] * (LANES // IDX_DIM), axis=1)
    lane = lax.broadcasted_iota(jnp.int32, (tm, LANES), 1)
    first_half = (lane % IDX_DIM) < (IDX_DIM // 2)

    off = 0
    for hh in range(N_HEADS):
        sl = z1[:, off:off + HEAD_DIM]
        q_ref[0, :, hh * HEAD_DIM:(hh + 1) * HEAD_DIM] = (_rope128(sl, cos128, sin128) * (QK_SCALE * LOG2E)).astype(BF16)
        off += HEAD_DIM
    for hh in range(N_KV_HEADS):
        kr = _rope128(z1[:, off:off + HEAD_DIM], cos128, sin128)
        k_ref[0, pl.ds(hh, tm, stride=N_KV_HEADS), :] = kr
        if mode == "prompt":
            kbf_ref[0, :, hh * HEAD_DIM:(hh + 1) * HEAD_DIM] = kr.astype(BF16)
        off += HEAD_DIM
    v = z1[:, off:off + N_KV_HEADS * HEAD_DIM]
    for hh in range(N_KV_HEADS):
        v_ref[0, pl.ds(hh, tm, stride=N_KV_HEADS), :] = v[:, hh * HEAD_DIM:(hh + 1) * HEAD_DIM]
    if mode == "prompt":
        for c in range(tm // KEY_STEP):
            vt = v[c * KEY_STEP:(c + 1) * KEY_STEP, :].T
            ones_row = jnp.where(lax.broadcasted_iota(jnp.int32, (VT_ROWS - HEAD_DIM, KEY_STEP), 0) == 0, 1.0, 0.0)
            for hh in range(N_KV_HEADS):
                vt_ref[0, c, hh * VT_ROWS:(hh + 1) * VT_ROWS, :] = jnp.concatenate(
                    [vt[hh * HEAD_DIM:(hh + 1) * HEAD_DIM, :], ones_row], axis=0).astype(BF16)
    off += N_KV_HEADS * HEAD_DIM
    for hh in range(N_IDX_HEADS * IDX_DIM // LANES):
        sl = z1[:, off:off + LANES]
        qi_ref[0, :, hh * LANES:(hh + 1) * LANES] = _rope64(sl, cos64, sin64, first_half).astype(BF16)
        off += LANES
    kiw = z1[:, off:off + LANES]
    kir = _rope64(kiw, cos64, sin64, first_half)[:, 0:IDX_DIM]
    ki_ref[0] = kir
    if mode == "prompt":
        kibf_ref[0] = kir.astype(BF16)
    wit_ref[0] = kiw.T[IDX_DIM:IDX_DIM + N_IDX_HEADS, :] * IDX_SCALE

    if mode == "prompt":
        @pl.when(pl.program_id(1) == 0)
        def _():
            carry_ref[...] = jnp.zeros_like(carry_ref)

    row = lax.broadcasted_iota(jnp.int32, (tm, CONV_SLAB), 0)
    for c in range(D_CONV // CONV_SLAB):
        cols = slice(c * CONV_SLAB, (c + 1) * CONV_SLAB)
        bg, cg, xc, ga, gb = (_dot(h, w2_ref[:, k * D_MODEL + c * CONV_SLAB:k * D_MODEL + (c + 1) * CONV_SLAB])
                              for k in range(5))
        u = cg * xc
        r1 = pltpu.roll(u, 1, axis=0)
        r2 = pltpu.roll(u, 2, axis=0)
        if mode == "prompt":
            c0 = carry_ref[0:1, cols]
            c1 = carry_ref[1:2, cols]
            um1 = jnp.where(row == 0, c1, r1)
            um2 = jnp.where(row == 0, c0, jnp.where(row == 1, c1, r2))
            carry_ref[0:2, cols] = u[tm - 2:tm, :]
            cnew_ref[0, :, cols] = u[tm - 2:tm, :]
        else:
            prev = prev_ref[0, :, cols]
            seq_row = row % SUBLANES
            um1 = jnp.where(seq_row == 0, pltpu.roll(prev, tm - 1, axis=0), r1)
            um2 = jnp.where(seq_row < 2, prev, r2)
            u_ref[0, :, cols] = u
        cw = cw_ref[:, cols]
        conv = cw[0:1, :] * um2 + cw[1:2, :] * um1 + cw[2:3, :] * u
        ga_ref[0, :, cols] = jax.nn.sigmoid(ga)
        cm_ref[0, :, cols] = jax.nn.sigmoid(gb) * (bg * conv)


def _const_spec(shape):
    nd = len(shape)
    return pl.BlockSpec(shape, lambda *_: (0,) * nd, pipeline_mode=pl.Buffered(1))


def _inproj(x, norm_g, w1, w2, conv_w, tab, prev, *, mode, tm):
    B, T, _ = x.shape
    assert T % tm == 0 and tm % KEY_STEP == 0 and D_CONV % CONV_SLAB == 0
    nt = T // tm
    tok = lambda w: pl.BlockSpec((1, tm, w), lambda b, t: (b, t, 0))
    in_specs = [tok(D_MODEL), _const_spec((1, D_MODEL)), _const_spec((D_MODEL, W1_COLS)),
                _const_spec((D_MODEL, W2_COLS)), _const_spec((CONV_WIDTH, D_CONV)),
                pl.BlockSpec((tm, HEAD_DIM + IDX_DIM), lambda b, t: (t, 0))]
    args = [x, norm_g.reshape(1, D_MODEL), w1, w2, conv_w, tab]
    kvw = N_KV_HEADS * HEAD_DIM
    qiw = N_IDX_HEADS * IDX_DIM
    wit_spec = pl.BlockSpec((1, N_IDX_HEADS, tm), lambda b, t: (b, 0, t))
    kv_spec = pl.BlockSpec((1, tm * N_KV_HEADS, HEAD_DIM), lambda b, t: (b, t, 0))
    if mode == "prompt":
        out_shape = [
            jax.ShapeDtypeStruct((B, T, D_MODEL), BF16),
            jax.ShapeDtypeStruct((B, T * N_KV_HEADS, HEAD_DIM), F32),
            jax.ShapeDtypeStruct((B, T * N_KV_HEADS, HEAD_DIM), F32),
            jax.ShapeDtypeStruct((B, T, kvw), BF16),
            jax.ShapeDtypeStruct((B, T // KEY_STEP, N_KV_HEADS * VT_ROWS, KEY_STEP), BF16),
            jax.ShapeDtypeStruct((B, T, qiw), BF16),
            jax.ShapeDtypeStruct((B, T, IDX_DIM), F32),
            jax.ShapeDtypeStruct((B, T, IDX_DIM), BF16),
            jax.ShapeDtypeStruct((B, N_IDX_HEADS, T), F32),
            jax.ShapeDtypeStruct((B, T, D_MODEL), F32),
            jax.ShapeDtypeStruct((B, T, D_MODEL), F32),
            jax.ShapeDtypeStruct((B, CONV_WIDTH - 1, D_CONV), F32),
        ]
        out_specs = [tok(D_MODEL), kv_spec, kv_spec, tok(kvw),
                     pl.BlockSpec((1, tm // KEY_STEP, N_KV_HEADS * VT_ROWS, KEY_STEP), lambda b, t: (b, t, 0, 0)),
                     tok(qiw), tok(IDX_DIM), tok(IDX_DIM), wit_spec, tok(D_MODEL), tok(D_MODEL),
                     pl.BlockSpec((1, CONV_WIDTH - 1, D_CONV), lambda b, t: (b, 0, 0))]
        scratch = [pltpu.VMEM((SUBLANES, D_CONV), F32)]
    else:
        in_specs.append(tok(D_CONV))
        args.append(prev)
        out_shape = [
            jax.ShapeDtypeStruct((B, T, D_MODEL), BF16),
            jax.ShapeDtypeStruct((B, T * N_KV_HEADS, HEAD_DIM), F32),
            jax.ShapeDtypeStruct((B, T * N_KV_HEADS, HEAD_DIM), F32),
            jax.ShapeDtypeStruct((B, T, qiw), BF16),
            jax.ShapeDtypeStruct((B, T, IDX_DIM), F32),
            jax.ShapeDtypeStruct((B, N_IDX_HEADS, T), F32),
            jax.ShapeDtypeStruct((B, T, D_MODEL), F32),
            jax.ShapeDtypeStruct((B, T, D_MODEL), F32),
            jax.ShapeDtypeStruct((B, T, D_CONV), F32),
        ]
        out_specs = [tok(D_MODEL), kv_spec, kv_spec, tok(qiw), tok(IDX_DIM), wit_spec,
                     tok(D_MODEL), tok(D_MODEL), tok(D_CONV)]
        scratch = []
    return pl.pallas_call(
        functools.partial(_inproj_kernel, mode=mode, tm=tm),
        grid=(B, nt),
        in_specs=in_specs,
        out_specs=out_specs,
        out_shape=out_shape,
        scratch_shapes=scratch,
        compiler_params=pltpu.CompilerParams(dimension_semantics=("arbitrary", "arbitrary"),
                                             vmem_limit_bytes=VMEM_LIMIT),
        name="inproj_" + mode,
    )(*args)


def _key_value(key):
    return lax.bitcast_convert_type(key ^ ((key >> 31) & 0x7FFFFFFF), F32)


def _search_keys(count_ge, kk, n_all):
    lo0 = jnp.full(kk.shape, KEY_LOWEST, jnp.int32)
    hi0 = jnp.full(kk.shape, KEY_INF, jnp.int32)
    n_lo0 = n_all
    def body(_, c):
        lo, hi, n_lo, n_hi, done = c
        mid = (lo & hi) + ((lo ^ hi) >> 1)
        stuck = mid == lo
        n_mid = count_ge(mid)
        upd = (done < 0.5) & jnp.logical_not(stuck)
        go_lo = n_mid >= kk
        up_lo = upd & go_lo
        lo = jnp.where(up_lo, mid, lo)
        n_lo = jnp.where(up_lo, n_mid, n_lo)
        up_hi = upd & jnp.logical_not(go_lo)
        hi = jnp.where(up_hi, mid, hi)
        n_hi = jnp.where(up_hi, n_mid, n_hi)
        done = jnp.where(stuck | (upd & (n_mid == kk)), 1.0, done)
        return lo, hi, n_lo, n_hi, done

    done0 = jnp.where(n_lo0 <= kk, 1.0, 0.0)
    state = (lo0, hi0, n_lo0, jnp.zeros_like(kk), done0)
    state = lax.fori_loop(0, SEARCH_BLIND, body, state)
    lo, hi, n_lo, n_hi, _ = lax.while_loop(lambda c: jnp.min(c[4]) < 0.5, lambda c: body(0, body(0, c)), state)
    return _key_value(lo), _key_value(hi), n_lo, n_hi


def _resolve_ties(count_ge, kk, n_lo, n_hi, tied, top):
    def by_position():
        return jnp.zeros_like(kk), jnp.where(tied, TINY, -LOWEST), n_hi

    def by_value():
        lo2, hi2, _, n_hi2 = _search_keys(count_ge, kk, n_lo)
        return lo2, hi2, n_hi2

    return lax.cond(jnp.max(jnp.where(tied & (top > 0.0), 1.0, 0.0)) > 0.5, by_value, by_position)


def _spread_ties(sc, lo, hi):
    return jnp.where(sc >= hi, -LOWEST, jnp.where(sc >= lo, sc - lo, NEG_INF))


def _fold_rows(x, op):
    return op(x.reshape(x.shape[0] // SUBLANES, SUBLANES, x.shape[1]), axis=0)


def _prompt_attn_kernel(qi_ref, wit_ref, q_ref, ki_ref, k_ref, vt_ref, o_ref,
                        sc_scr, b_scr, d_a, d_b, s_a, s_b, p_a, p_b, acc_scr, *, n_sel, count_step):
    j = pl.program_id(1)
    per_step = KEY_STEP // Q_BLOCK
    n_steps = (j + per_step) // per_step
    n_pairs = n_steps // 2
    odd = n_steps % 2 == 1
    last = n_steps - 1
    per_count = count_step // KEY_STEP
    n_count = (n_steps + per_count - 1) // per_count
    qi = qi_ref[0]
    q_stack = jnp.concatenate([qi[:, h * IDX_DIM:(h + 1) * IDX_DIM] for h in range(N_IDX_HEADS)], axis=0)
    wit = wit_ref[0]
    q = q_ref[0]
    q_grp = [jnp.concatenate([q[:, (kv * GROUP + g) * HEAD_DIM:(kv * GROUP + g + 1) * HEAD_DIM]
                              for g in range(GROUP)], axis=0) for kv in range(N_KV_HEADS)]
    key_pos = lax.broadcasted_iota(jnp.int32, (KEY_STEP, Q_BLOCK), 0)
    q_pos = lax.broadcasted_iota(jnp.int32, (KEY_STEP, Q_BLOCK), 1) + j * Q_BLOCK

    def step(i):
        return pl.ds(pl.multiple_of(i * KEY_STEP, KEY_STEP), KEY_STEP)

    def idx_dots(i):
        return _dot_nt(ki_ref[0, step(i), :], q_stack)

    def score_trip(i, d_cur, d_nxt):
        d_nxt[...] = idx_dots(jnp.minimum(i + 1, last))
        acc = wit[0:1, :] * jnp.maximum(d_cur[:, 0:Q_BLOCK], 0.0)
        for h in range(1, N_IDX_HEADS):
            acc = acc + wit[h:h + 1, :] * jnp.maximum(d_cur[:, h * Q_BLOCK:(h + 1) * Q_BLOCK], 0.0)
        sc_scr[step(i), :] = jnp.where(key_pos + i * KEY_STEP <= q_pos, acc, NEG_INF)

    def logits(i, kv):
        return _dot_nt(k_ref[0, step(i), kv * HEAD_DIM:(kv + 1) * HEAD_DIM], q_grp[kv])

    d_a[...] = idx_dots(0)
    for kv in range(N_KV_HEADS):
        s_a[kv] = logits(0, kv)

    def score_pair(pi, c):
        score_trip(2 * pi, d_a, d_b)
        score_trip(2 * pi + 1, d_b, d_a)
        return c

    lax.fori_loop(0, n_pairs, score_pair, 0)

    @pl.when(odd)
    def _():
        score_trip(last, d_a, d_b)

    def fill(i, c):
        sc_scr[step(i), :] = jnp.full((KEY_STEP, Q_BLOCK), NEG_INF, F32)
        return c

    lax.fori_loop(n_steps, n_count * per_count, fill, 0)

    def count_ge(t):
        tb = jnp.broadcast_to(_key_value(t), (SUBLANES, Q_BLOCK))

        def body(i, accs):
            accs = list(accs)
            base = pl.multiple_of(i * count_step, count_step)
            for r in range(count_step // SUBLANES):
                sc = sc_scr[pl.ds(base + r * SUBLANES, SUBLANES), :]
                accs[r % COUNT_LANES] = accs[r % COUNT_LANES] + jnp.where(sc >= tb, 1.0, 0.0)
            return tuple(accs)

        accs = lax.fori_loop(0, n_count, body, tuple(jnp.zeros((SUBLANES, Q_BLOCK), F32) for _ in range(COUNT_LANES)))
        return jnp.sum(functools.reduce(lambda a, b: a + b, accs), axis=0, keepdims=True)

    n_valid = (j * Q_BLOCK + 1 + lax.broadcasted_iota(jnp.int32, (1, Q_BLOCK), 1)).astype(F32)
    kk = jnp.minimum(n_valid, float(n_sel))
    lo, hi, n_lo, n_hi = lax.cond((j + 1) * Q_BLOCK <= n_sel,
                                  lambda: (jnp.full_like(kk, LOWEST), jnp.full_like(kk, jnp.inf), n_valid,
                                           jnp.zeros_like(kk)),
                                  lambda: _search_keys(count_ge, kk, n_valid))

    def mask_plain(i, c):
        b_scr[step(i), :] = jnp.where(sc_scr[step(i), :] >= lo, 0.0, NEG_INF)
        return c

    def spread(i, top):
        sp = _spread_ties(sc_scr[step(i), :], lo, hi)
        sc_scr[step(i), :] = sp
        return jnp.maximum(top, _fold_rows(jnp.where(sp < -LOWEST, sp, NEG_INF), jnp.max))

    tied = n_lo > kk
    any_tie = jnp.max(jnp.where(tied, 1.0, 0.0)) > 0.5

    @pl.when(any_tie)
    def _():
        top = lax.fori_loop(0, n_steps, spread, jnp.full((SUBLANES, Q_BLOCK), NEG_INF, F32))
        lo2, hi2, n_hi2 = _resolve_ties(count_ge, kk, n_lo, n_hi, tied, jnp.max(top, axis=0, keepdims=True))
        tri = jnp.where(lax.broadcasted_iota(jnp.int32, (KEY_STEP, KEY_STEP), 1)
                        <= lax.broadcasted_iota(jnp.int32, (KEY_STEP, KEY_STEP), 0), 1.0, 0.0).astype(BF16)

        def mask_ties(i, run):
            sc = sc_scr[step(i), :]
            above = sc >= hi2
            elig = (sc >= lo2) & jnp.logical_not(above)
            rank = run + _dot(tri, jnp.where(elig, 1.0, 0.0).astype(BF16))
            b_scr[step(i), :] = jnp.where(above | (elig & (rank <= kk - n_hi2)), 0.0, NEG_INF)
            return rank[KEY_STEP - 1:KEY_STEP, :]

        lax.fori_loop(0, n_steps, mask_ties, jnp.zeros((1, Q_BLOCK), F32))

    @pl.when(jnp.logical_not(any_tie))
    def _():
        lax.fori_loop(0, n_steps, mask_plain, 0)

    def pv_dot(i, kv, p_ref):
        return _dot(vt_ref[0, i, kv * VT_ROWS:(kv + 1) * VT_ROWS, :], p_ref[kv])

    def attn_trip(i, s_cur, s_nxt, p_cur, p_prv, m):
        pv = [pv_dot(jnp.maximum(i - 1, 0), kv, p_prv) for kv in range(N_KV_HEADS)]
        for kv in range(N_KV_HEADS):
            s_nxt[kv] = logits(jnp.minimum(i + 1, last), kv)
        b = b_scr[step(i), :]
        m_rows = []
        for kv in range(N_KV_HEADS):
            ps, alphas = [], []
            for g in range(GROUP):
                h = kv * GROUP + g
                sg = s_cur[kv, :, g * Q_BLOCK:(g + 1) * Q_BLOCK] + b
                m_old = m[h:h + 1, :]
                m_new = jnp.maximum(m_old, jnp.max(_fold_rows(sg, jnp.max), axis=0, keepdims=True))
                ps.append(jnp.exp2(sg - m_new).astype(BF16))
                alphas.append(jnp.exp2(m_old - m_new))
                m_rows.append(m_new)
            p_cur[kv] = jnp.concatenate(ps, axis=1)
            acc_scr[kv] = (acc_scr[kv] + pv[kv]) * jnp.concatenate(alphas, axis=1)
        return jnp.concatenate(m_rows, axis=0)

    acc_scr[...] = jnp.zeros_like(acc_scr)
    p_b[...] = jnp.zeros_like(p_b)

    def attn_pair(pi, m):
        m = attn_trip(2 * pi, s_a, s_b, p_a, p_b, m)
        return attn_trip(2 * pi + 1, s_b, s_a, p_b, p_a, m)

    m_end = lax.fori_loop(0, n_pairs, attn_pair, jnp.full((N_HEADS, Q_BLOCK), M_FLOOR, F32))

    @pl.when(odd)
    def _():
        attn_trip(last, s_a, s_b, p_a, p_b, m_end)
        p_b[...] = p_a[...]

    for h in range(N_HEADS):
        kv, g = divmod(h, GROUP)
        if g == 0:
            acc_kv = acc_scr[kv] + pv_dot(last, kv, p_b)
        cols = slice(g * Q_BLOCK, (g + 1) * Q_BLOCK)
        inv_l = 1.0 / acc_kv[HEAD_DIM:HEAD_DIM + 1, cols]
        o_ref[0, :, h * HEAD_DIM:(h + 1) * HEAD_DIM] = (acc_kv[0:HEAD_DIM, cols] * inv_l).T


def _prompt_attention(qi, wit, q, ki_bf, k_bf, vt):
    B, T, _ = q.shape
    nq = T // Q_BLOCK
    n_sel = min(TOPK_MAX, T // 4)
    kvw = N_KV_HEADS * HEAD_DIM
    assert T % KEY_STEP == 0 and KEY_STEP % Q_BLOCK == 0
    count_step = next(c for c in (COUNT_STEP, COUNT_STEP // 2, KEY_STEP) if T % c == 0)
    gq = GROUP * Q_BLOCK
    return pl.pallas_call(
        functools.partial(_prompt_attn_kernel, n_sel=n_sel, count_step=count_step),
        grid=(B, nq),
        in_specs=[pl.BlockSpec((1, Q_BLOCK, N_IDX_HEADS * IDX_DIM), lambda b, j: (b, j, 0)),
                  pl.BlockSpec((1, N_IDX_HEADS, Q_BLOCK), lambda b, j: (b, 0, j)),
                  pl.BlockSpec((1, Q_BLOCK, D_MODEL), lambda b, j: (b, j, 0)),
                  pl.BlockSpec((1, T, IDX_DIM), lambda b, j: (b, 0, 0)),
                  pl.BlockSpec((1, T, kvw), lambda b, j: (b, 0, 0)),
                  pl.BlockSpec((1, T // KEY_STEP, N_KV_HEADS * VT_ROWS, KEY_STEP), lambda b, j: (b, 0, 0, 0))],
        out_specs=pl.BlockSpec((1, Q_BLOCK, D_MODEL), lambda b, j: (b, j, 0)),
        out_shape=jax.ShapeDtypeStruct((B, T, D_MODEL), F32),
        scratch_shapes=[pltpu.VMEM((T, Q_BLOCK), F32), pltpu.VMEM((T, Q_BLOCK), F32),
                        pltpu.VMEM((KEY_STEP, N_IDX_HEADS * Q_BLOCK), F32),
                        pltpu.VMEM((KEY_STEP, N_IDX_HEADS * Q_BLOCK), F32),
                        pltpu.VMEM((N_KV_HEADS, KEY_STEP, gq), F32), pltpu.VMEM((N_KV_HEADS, KEY_STEP, gq), F32),
                        pltpu.VMEM((N_KV_HEADS, KEY_STEP, gq), BF16), pltpu.VMEM((N_KV_HEADS, KEY_STEP, gq), BF16),
                        pltpu.VMEM((N_KV_HEADS, VT_ROWS, gq), F32)],
        compiler_params=pltpu.CompilerParams(dimension_semantics=("arbitrary", "arbitrary"),
                                             vmem_limit_bytes=VMEM_LIMIT),
        name="prompt_attention",
    )(qi, wit, q, ki_bf, k_bf, vt)


def _sample_score_kernel(pt_ref, qi_ref, wcol_ref, kin_ref, *rest, n_pages, ts, n_seq):
    page_refs, s_ref = rest[:n_seq * n_pages], rest[n_seq * n_pages]
    lane = lax.broadcasted_iota(jnp.int32, (ts, PAGE_SIZE), 1)
    row = lax.broadcasted_iota(jnp.int32, (ts, PAGE_SIZE), 0)
    for q in range(n_seq):
        qi = qi_ref[q].astype(F32)
        q_all = jnp.concatenate([qi[:, h * IDX_DIM:(h + 1) * IDX_DIM] for h in range(N_IDX_HEADS)],
                                axis=0).astype(BF16)
        wcol = wcol_ref[q]

        def score(dots, wcol=wcol):
            r = jnp.maximum(dots, 0.0) * wcol
            acc = r[0:ts]
            for h in range(1, N_IDX_HEADS):
                acc = acc + r[h * ts:(h + 1) * ts]
            return acc

        past = jnp.concatenate([page_refs[q * n_pages + p][0] for p in range(n_pages)], axis=1).astype(BF16)
        s_ref[q, :, 0:n_pages * PAGE_SIZE] = score(_dot(q_all, past))
        new_keys = jnp.concatenate([kin_ref[q], jnp.zeros((PAGE_SIZE - ts, IDX_DIM), F32)], axis=0).astype(BF16)
        s_ref[q, :, n_pages * PAGE_SIZE:(n_pages + 1) * PAGE_SIZE] = jnp.where(
            lane <= row, score(_dot_nt(q_all, new_keys)), NEG_INF)


def _sample_scores(page_table, qi, wcol, ki_bf, cache_kidx):
    nb, ts, _ = qi.shape
    n_pages = page_table.shape[1]
    n_seq = SCORE_SEQS if nb % SCORE_SEQS == 0 else 1

    def page_spec(q, p):
        return pl.BlockSpec((1, IDX_DIM, PAGE_SIZE), lambda b, pt: (pt[b * n_seq + q, p], 0, 0))

    grid_spec = pltpu.PrefetchScalarGridSpec(
        num_scalar_prefetch=1,
        grid=(nb // n_seq,),
        in_specs=[pl.BlockSpec((n_seq, ts, N_IDX_HEADS * IDX_DIM), lambda b, pt: (b, 0, 0)),
                  pl.BlockSpec((n_seq, N_IDX_HEADS * ts, 1), lambda b, pt: (b, 0, 0)),
                  pl.BlockSpec((n_seq, ts, IDX_DIM), lambda b, pt: (b, 0, 0))]
                 + [page_spec(q, p) for q in range(n_seq) for p in range(n_pages)],
        out_specs=pl.BlockSpec((n_seq, ts, (n_pages + 1) * PAGE_SIZE), lambda b, pt: (b, 0, 0)),
    )
    return pl.pallas_call(
        functools.partial(_sample_score_kernel, n_pages=n_pages, ts=ts, n_seq=n_seq),
        grid_spec=grid_spec,
        out_shape=jax.ShapeDtypeStruct((nb, ts, (n_pages + 1) * PAGE_SIZE), F32),
        compiler_params=pltpu.CompilerParams(dimension_semantics=("arbitrary",), vmem_limit_bytes=VMEM_LIMIT),
        name="sample_scores",
    )(page_table, qi, wcol, ki_bf, *([cache_kidx] * (n_seq * n_pages)))


def _sample_select_kernel(s_ref, b_ref, sc_scr, *, n_sel, n_chunks):
    rows = s_ref.shape[0]

    def cols(c):
        return slice(c * LANES, (c + 1) * LANES)

    def counter(ref):
        def count_ge(t):
            tf = _key_value(t)
            acc = jnp.where(ref[:, cols(0)] >= tf, 1.0, 0.0)
            for c in range(1, n_chunks):
                acc = acc + jnp.where(ref[:, cols(c)] >= tf, 1.0, 0.0)
            return jnp.sum(acc, axis=1, keepdims=True)
        return count_ge

    kk = jnp.full((rows, 1), float(n_sel), F32)
    n_all = counter(s_ref)(jnp.full((rows, 1), KEY_LOWEST, jnp.int32))
    lo, hi, n_lo, n_hi = _search_keys(counter(s_ref), kk, n_all)
    tied = n_lo > kk
    any_tie = jnp.max(jnp.where(tied, 1.0, 0.0)) > 0.5

    @pl.when(jnp.logical_not(any_tie))
    def _():
        for c in range(n_chunks):
            b_ref[:, cols(c)] = jnp.where(s_ref[:, cols(c)] >= lo, 0.0, NEG_INF)

    @pl.when(any_tie)
    def _():
        top = jnp.full((rows, LANES), NEG_INF, F32)
        for c in range(n_chunks):
            sp = _spread_ties(s_ref[:, cols(c)], lo, hi)
            sc_scr[:, cols(c)] = sp
            top = jnp.maximum(top, jnp.where(sp < -LOWEST, sp, NEG_INF))
        lo2, hi2, n_hi2 = _resolve_ties(counter(sc_scr), kk, n_lo, n_hi, tied, jnp.max(top, axis=1, keepdims=True))
        tri = jnp.where(lax.broadcasted_iota(jnp.int32, (LANES, LANES), 0)
                        <= lax.broadcasted_iota(jnp.int32, (LANES, LANES), 1), 1.0, 0.0).astype(BF16)
        run = jnp.zeros((rows, 1), F32)
        for c in range(n_chunks):
            sc = sc_scr[:, cols(c)]
            above = sc >= hi2
            elig = (sc >= lo2) & jnp.logical_not(above)
            rank = run + _dot(jnp.where(elig, 1.0, 0.0).astype(BF16), tri)
            b_ref[:, cols(c)] = jnp.where(above | (elig & (rank <= kk - n_hi2)), 0.0, NEG_INF)
            run = rank[:, LANES - 1:LANES]


def _sample_select(scores, n_sel, rows_per_step):
    rows, width = scores.shape
    assert rows % rows_per_step == 0 and width % LANES == 0
    return pl.pallas_call(
        functools.partial(_sample_select_kernel, n_sel=n_sel, n_chunks=width // LANES),
        grid=(rows // rows_per_step,),
        in_specs=[pl.BlockSpec((rows_per_step, width), lambda i: (i, 0))],
        out_specs=pl.BlockSpec((rows_per_step, width), lambda i: (i, 0)),
        out_shape=jax.ShapeDtypeStruct((rows, width), F32),
        scratch_shapes=[pltpu.VMEM((rows_per_step, width), F32)],
        compiler_params=pltpu.CompilerParams(dimension_semantics=("arbitrary",), vmem_limit_bytes=VMEM_LIMIT),
        name="sample_select",
    )(scores)


def _sample_attn_kernel(pt_ref, q_ref, b_ref, kn_ref, vn_ref, *rest, n_pages, ts, n_seq):
    k_refs, v_refs, o_ref = rest[:n_seq * n_pages], rest[n_seq * n_pages:2 * n_seq * n_pages], rest[2 * n_seq * n_pages]
    pad = jnp.zeros((PAGE_SIZE - ts, HEAD_DIM), F32)

    def head_rows(refs, new_ref, u, kv):
        tiles = [r[0, pl.ds(kv, PAGE_SIZE, stride=N_KV_HEADS), :] for r in refs[u * n_pages:(u + 1) * n_pages]]
        tiles += [new_ref[u, pl.ds(kv, ts, stride=N_KV_HEADS), :], pad]
        return jnp.concatenate(tiles, axis=0).astype(BF16)

    units = [(u, kv) for u in range(n_seq) for kv in range(N_KV_HEADS)]
    s = []
    for u, kv in units:
        q = q_ref[u].astype(F32)
        qg = jnp.concatenate([q[:, (kv * GROUP + g) * HEAD_DIM:(kv * GROUP + g + 1) * HEAD_DIM]
                              for g in range(GROUP)], axis=0).astype(BF16)
        bias = jnp.concatenate([b_ref[u]] * GROUP, axis=0)
        s.append(_dot_nt(qg, head_rows(k_refs, kn_ref, u, kv)) + bias)
    p_ = [jnp.exp2(x - jnp.max(x, axis=1, keepdims=True)) for x in s]
    for (u, kv), pr in zip(units, p_):
        o = _dot(pr.astype(BF16), head_rows(v_refs, vn_ref, u, kv)) / jnp.sum(pr, axis=1, keepdims=True)
        for g in range(GROUP):
            hh = kv * GROUP + g
            o_ref[u, :, hh * HEAD_DIM:(hh + 1) * HEAD_DIM] = o[g * ts:(g + 1) * ts, :]


def _sample_attention(page_table, q, bias, k_new, v_new, cache_k, cache_v):
    nb, ts, _ = q.shape
    n_pages = page_table.shape[1]
    n_seq = ATTN_SEQS if nb % ATTN_SEQS == 0 else 1

    def page_spec(u, p):
        return pl.BlockSpec((1, PAGE_SIZE * N_KV_HEADS, HEAD_DIM), lambda b, pt: (pt[b * n_seq + u, p], 0, 0))

    pages = [page_spec(u, p) for u in range(n_seq) for p in range(n_pages)]
    grid_spec = pltpu.PrefetchScalarGridSpec(
        num_scalar_prefetch=1,
        grid=(nb // n_seq,),
        in_specs=[pl.BlockSpec((n_seq, ts, D_MODEL), lambda b, pt: (b, 0, 0)),
                  pl.BlockSpec((n_seq, ts, (n_pages + 1) * PAGE_SIZE), lambda b, pt: (b, 0, 0)),
                  pl.BlockSpec((n_seq, ts * N_KV_HEADS, HEAD_DIM), lambda b, pt: (b, 0, 0)),
                  pl.BlockSpec((n_seq, ts * N_KV_HEADS, HEAD_DIM), lambda b, pt: (b, 0, 0))] + pages * 2,
        out_specs=pl.BlockSpec((n_seq, ts, D_MODEL), lambda b, pt: (b, 0, 0)),
    )
    return pl.pallas_call(
        functools.partial(_sample_attn_kernel, n_pages=n_pages, ts=ts, n_seq=n_seq),
        grid_spec=grid_spec,
        out_shape=jax.ShapeDtypeStruct((nb, ts, D_MODEL), F32),
        compiler_params=pltpu.CompilerParams(dimension_semantics=("arbitrary",), vmem_limit_bytes=VMEM_LIMIT),
        name="sample_attention",
    )(page_table, q, bias, k_new, v_new, *([cache_k] * (n_seq * n_pages)), *([cache_v] * (n_seq * n_pages)))


def _ffn_kernel(x_ref, oa_ref, ga_ref, cm_ref, p_ref, wout_ref, gffn_ref, wg_ref, wu_ref, wd_ref,
                gple_ref, wple_ref, wpg_ref, gfin_ref, y_ref, x1_scr, h_scr, acc_scr):
    f = pl.program_id(1)

    @pl.when(f == 0)
    def _():
        merged = ga_ref[...] * oa_ref[...] + cm_ref[...]
        x1 = x_ref[...] + _dot(merged.astype(BF16), wout_ref[...])
        x1_scr[...] = x1
        h_scr[...] = _rmsnorm(x1, gffn_ref[...]).astype(BF16)
        acc_scr[...] = jnp.zeros_like(acc_scr)

    h = h_scr[...]
    g = _dot(h, wg_ref[...])
    u = _dot(h, wu_ref[...])
    acc_scr[...] += _dot((g * jax.nn.sigmoid(g) * u).astype(BF16), wd_ref[...])

    @pl.when(f == pl.num_programs(1) - 1)
    def _():
        x2 = x1_scr[...] + acc_scr[...]
        gate = jax.nn.sigmoid(_dot(_rmsnorm(x2, gple_ref[...]).astype(BF16), wpg_ref[...]))
        x3 = x2 + _dot(p_ref[...].astype(BF16), wple_ref[...]) * gate
        y_ref[...] = _rmsnorm(x3, gfin_ref[...])


def _ffn(x, oa, ga, cm, p, wout, gffn, wg, wu, wd, gple, wple, wpg, gfin, *, tm, tf):
    n = x.shape[0]
    tm = min(tm, n)
    assert n % tm == 0 and D_FF % tf == 0
    tok = lambda w: pl.BlockSpec((tm, w), lambda i, f: (i, 0))
    vec = lambda a: a.reshape(1, D_MODEL)
    return pl.pallas_call(
        _ffn_kernel,
        grid=(n // tm, D_FF // tf),
        in_specs=[tok(D_MODEL), tok(D_MODEL), tok(D_MODEL), tok(D_MODEL), tok(D_PLE),
                  _const_spec((D_MODEL, D_MODEL)), _const_spec((1, D_MODEL)),
                  pl.BlockSpec((D_MODEL, tf), lambda i, f: (0, f)),
                  pl.BlockSpec((D_MODEL, tf), lambda i, f: (0, f)),
                  pl.BlockSpec((tf, D_MODEL), lambda i, f: (f, 0)),
                  _const_spec((1, D_MODEL)), _const_spec((D_PLE, D_MODEL)), _const_spec((D_MODEL, D_MODEL)),
                  _const_spec((1, D_MODEL))],
        out_specs=tok(D_MODEL),
        out_shape=jax.ShapeDtypeStruct((n, D_MODEL), F32),
        scratch_shapes=[pltpu.VMEM((tm, D_MODEL), F32), pltpu.VMEM((tm, D_MODEL), BF16),
                        pltpu.VMEM((tm, D_MODEL), F32)],
        compiler_params=pltpu.CompilerParams(dimension_semantics=("arbitrary", "arbitrary"),
                                             vmem_limit_bytes=VMEM_LIMIT),
        name="ffn",
    )(x, oa, ga, cm, p, wout, vec(gffn), wg, wu, wd, vec(gple), wple, wpg, vec(gfin))


def _split_w_in(w_in):
    n_att = N_HEADS * HEAD_DIM + 2 * N_KV_HEADS * HEAD_DIM + N_IDX_HEADS * IDX_DIM + IDX_DIM + N_IDX_HEADS
    return w_in[:, :W1_COLS].astype(BF16), w_in[:, n_att:].astype(BF16)


def kernel(x_prompt, x_sample, cache_k, cache_v, cache_kidx, state_conv, page_table, p_prompt, p_sample, norm_mix, w_in, conv_w, w_out, norm_ffn, w_gate_up, w_down, norm_ple, w_ple, w_ple_gate, norm_final):
    Bp, Tp, _ = x_prompt.shape
    Bs, Ts, _ = x_sample.shape
    n_pages = page_table.shape[1]
    past_len = n_pages * PAGE_SIZE
    n_phys = cache_k.shape[1]
    assert w_in.shape[0] == 1, "one layer: the per-layer tensors are indexed at l = 0 below"
    assert Ts == SUBLANES and cache_k.shape[2] == PAGE_SIZE
    l = 0

    w1, w2 = _split_w_in(w_in[l])
    wout = w_out[l].astype(BF16)
    wg = w_gate_up[l][:, :D_FF].astype(BF16)
    wu = w_gate_up[l][:, D_FF:].astype(BF16)
    wd = w_down[l].astype(BF16)
    wple = w_ple[l].astype(BF16)
    wpg = w_ple_gate[l].astype(BF16)

    tab_p = _rope_table(jnp.arange(Tp))
    (q_p, k_p, v_p, kbf_p, vt_p, qi_p, ki_p, kibf_p, wit_p, ga_p, cm_p, cnew_p) = _inproj(
        x_prompt, norm_mix[l], w1, w2, conv_w[l], tab_p, None, mode="prompt", tm=min(INPROJ_ROWS, Tp))
    oa_p = _prompt_attention(qi_p, wit_p, q_p, kibf_p, kbf_p, vt_p)
    n_p = Bp * Tp
    y_p = _ffn(x_prompt.reshape(n_p, D_MODEL), oa_p.reshape(n_p, D_MODEL), ga_p.reshape(n_p, D_MODEL),
               cm_p.reshape(n_p, D_MODEL), p_prompt[l].reshape(n_p, D_PLE), wout, norm_ffn[l], wg, wu, wd,
               norm_ple[l], wple, wpg, norm_final, tm=512, tf=D_FF // 2)

    n_s = Bs * Ts
    tm_s = min(INPROJ_ROWS, n_s)
    tab_s = _rope_table(past_len + (jnp.arange(tm_s) % Ts))
    prev = jnp.concatenate([state_conv[l], jnp.zeros((Bs, Ts - (CONV_WIDTH - 1), D_CONV), F32)], axis=1)
    (q_s, k_s, v_s, qi_s, ki_s, wit_s, ga_s, cm_s, u_s) = _inproj(
        x_sample.reshape(n_s // tm_s, tm_s, D_MODEL), norm_mix[l], w1, w2, conv_w[l], tab_s,
        prev.reshape(n_s // tm_s, tm_s, D_CONV), mode="sample", tm=tm_s)
    wcol = wit_s.transpose(1, 0, 2).reshape(N_IDX_HEADS, Bs, Ts).transpose(1, 0, 2).reshape(Bs, N_IDX_HEADS * Ts, 1)
    scores = _sample_scores(page_table, qi_s.reshape(Bs, Ts, -1), wcol, ki_s.reshape(Bs, Ts, IDX_DIM),
                            jnp.swapaxes(cache_kidx[l], 1, 2))
    n_sel = min(TOPK_MAX, (past_len + Ts) // 4)
    bias = _sample_select(scores.reshape(n_s, -1), n_sel, 256)
    oa_s = _sample_attention(page_table, q_s.reshape(Bs, Ts, D_MODEL), bias.reshape(Bs, Ts, -1),
                             k_s.reshape(Bs, Ts * N_KV_HEADS, HEAD_DIM), v_s.reshape(Bs, Ts * N_KV_HEADS, HEAD_DIM),
                             cache_k[l].reshape(n_phys, PAGE_SIZE * N_KV_HEADS, HEAD_DIM),
                             cache_v[l].reshape(n_phys, PAGE_SIZE * N_KV_HEADS, HEAD_DIM))
    y_s = _ffn(x_sample.reshape(n_s, D_MODEL), oa_s.reshape(n_s, D_MODEL), ga_s.reshape(n_s, D_MODEL),
               cm_s.reshape(n_s, D_MODEL), p_sample[l].reshape(n_s, D_PLE), wout, norm_ffn[l], wg, wu, wd,
               norm_ple[l], wple, wpg, norm_final, tm=512, tf=D_FF // 2)

    return (y_p.reshape(Bp, Tp, D_MODEL), y_s.reshape(Bs, Ts, D_MODEL),
            k_p.reshape(1, Bp, Tp, N_KV_HEADS, HEAD_DIM), v_p.reshape(1, Bp, Tp, N_KV_HEADS, HEAD_DIM),
            ki_p.reshape(1, Bp, Tp, IDX_DIM), cnew_p.reshape(1, Bp, CONV_WIDTH - 1, D_CONV),
            k_s.reshape(1, Bs, Ts, N_KV_HEADS, HEAD_DIM), v_s.reshape(1, Bs, Ts, N_KV_HEADS, HEAD_DIM),
            ki_s.reshape(1, Bs, Ts, IDX_DIM),
            u_s.reshape(Bs, Ts, D_CONV)[:, Ts - (CONV_WIDTH - 1):, :].reshape(1, Bs, CONV_WIDTH - 1, D_CONV))
```

```python
import functools

import jax
import jax.numpy as jnp
from jax import lax
from jax.experimental import pallas as pl
from jax.experimental.pallas import tpu as pltpu

D_MODEL = 1024
N_HEADS = 8
N_KV_HEADS = 2
GROUP = N_HEADS // N_KV_HEADS
HEAD_DIM = 128
N_IDX_HEADS = 8
IDX_DIM = 64
IDX_SCALE = (N_IDX_HEADS * IDX_DIM) ** -0.5
QK_SCALE = HEAD_DIM ** -0.5
TOPK_MAX = 256
D_CONV = D_MODEL
CONV_WIDTH = 3
D_FF = 2816
D_PLE = 256
PAGE_SIZE = 128
ROPE_THETA = 10000.0
EPS = 1e-6

LANES = 128
SUBLANES = 8
KEY_STEP = 256
Q_BLOCK = 128
COUNT_STEP = 512
COUNT_LANES = 8
VT_ROWS = HEAD_DIM + 16
CONV_SLAB = 256
INPROJ_ROWS = 512
SCORE_SEQS = 8
ATTN_SEQS = 4
FFN_SLAB = 256
VMEM_LIMIT = 56 * 1024 * 1024

W1_COLS = N_HEADS * HEAD_DIM + 2 * N_KV_HEADS * HEAD_DIM + N_IDX_HEADS * IDX_DIM + LANES
W2_COLS = 5 * D_MODEL

F32 = jnp.float32
BF16 = jnp.bfloat16
NEG_INF = float("-inf")
LOWEST = -3.4028234663852886e38
TINY = 1.1754943508222875e-38
KEY_LOWEST = -0x7F800000
KEY_INF = 0x7F800000
SEARCH_BLIND = 24
M_FLOOR = -1e30
LOG2E = 1.4426950408889634


def _dot(a, b):
    return jnp.dot(a, b, preferred_element_type=F32)


def _dot_nt(a, b):
    return lax.dot_general(a, b, (((1,), (1,)), ((), ())), preferred_element_type=F32)


def _rmsnorm(x, g):
    var = jnp.mean(x * x, axis=-1, keepdims=True)
    return (x * lax.rsqrt(var + EPS)) * g


def _rope_table(pos):
    def tab(half):
        freqs = ROPE_THETA ** (-jnp.arange(half, dtype=F32) / half)
        ang = pos.astype(F32)[:, None] * freqs[None, :]
        return [jnp.cos(ang), jnp.sin(ang)]
    return jnp.concatenate(tab(HEAD_DIM // 2) + tab(IDX_DIM // 2), axis=1)


def _rope128(x, cos, sin):
    return x * cos + pltpu.roll(x, HEAD_DIM // 2, axis=1) * sin


def _rope64(x, cos, sin, first_half):
    partner = jnp.where(first_half, pltpu.roll(x, LANES - IDX_DIM // 2, axis=1), pltpu.roll(x, IDX_DIM // 2, axis=1))
    return x * cos + partner * sin


def _inproj_kernel(*refs, mode, tm):
    if mode == "prompt":
        (x_ref, g_ref, w1_ref, w2_ref, cw_ref, tab_ref,
         q_ref, k_ref, v_ref, kbf_ref, vt_ref, qi_ref, ki_ref, kibf_ref, wit_ref, ga_ref, cm_ref, cnew_ref,
         carry_ref) = refs
    else:
        (x_ref, g_ref, w1_ref, w2_ref, cw_ref, tab_ref, prev_ref,
         q_ref, k_ref, v_ref, qi_ref, ki_ref, wit_ref, ga_ref, cm_ref, u_ref) = refs

    x = x_ref[0]
    h = _rmsnorm(x, g_ref[...]).astype(BF16)
    z1 = _dot(h, w1_ref[...])
    tab = tab_ref[...]
    h128, h64 = HEAD_DIM // 2, IDX_DIM // 2
    c, sn = tab[:, 0:h128], tab[:, h128:2 * h128]
    cos128, sin128 = jnp.concatenate([c, c], axis=1), jnp.concatenate([-sn, sn], axis=1)
    c, sn = tab[:, 2 * h128:2 * h128 + h64], tab[:, 2 * h128 + h64:2 * h128 + 2 * h64]
    cos64 = jnp.concatenate([c, c] * (LANES // IDX_DIM), axis=1)
    sin64 = jnp.concatenate([-sn, sn] * (LANES // IDX_DIM), axis=1)
    lane = lax.broadcasted_iota(jnp.int32, (tm, LANES), 1)
    first_half = (lane % IDX_DIM) < (IDX_DIM // 2)

    off = 0
    for hh in range(N_HEADS):
        sl = z1[:, off:off + HEAD_DIM]
        q_ref[0, :, hh * HEAD_DIM:(hh + 1) * HEAD_DIM] = (_rope128(sl, cos128, sin128) * (QK_SCALE * LOG2E)).astype(BF16)
        off += HEAD_DIM
    for hh in range(N_KV_HEADS):
        kr = _rope128(z1[:, off:off + HEAD_DIM], cos128, sin128)
        k_ref[0, pl.ds(hh, tm, stride=N_KV_HEADS), :] = kr
        if mode == "prompt":
            kbf_ref[0, :, hh * HEAD_DIM:(hh + 1) * HEAD_DIM] = kr.astype(BF16)
        off += HEAD_DIM
    v = z1[:, off:off + N_KV_HEADS * HEAD_DIM]
    for hh in range(N_KV_HEADS):
        v_ref[0, pl.ds(hh, tm, stride=N_KV_HEADS), :] = v[:, hh * HEAD_DIM:(hh + 1) * HEAD_DIM]
    if mode == "prompt":
        for c in range(tm // KEY_STEP):
            vt = v[c * KEY_STEP:(c + 1) * KEY_STEP, :].T
            ones_row = jnp.where(lax.broadcasted_iota(jnp.int32, (VT_ROWS - HEAD_DIM, KEY_STEP), 0) == 0, 1.0, 0.0)
            for hh in range(N_KV_HEADS):
                vt_ref[0, c, hh * VT_ROWS:(hh + 1) * VT_ROWS, :] = jnp.concatenate(
                    [vt[hh * HEAD_DIM:(hh + 1) * HEAD_DIM, :], ones_row], axis=0).astype(BF16)
    off += N_KV_HEADS * HEAD_DIM
    for hh in range(N_IDX_HEADS * IDX_DIM // LANES):
        sl = z1[:, off:off + LANES]
        qi_ref[0, :, hh * LANES:(hh + 1) * LANES] = _rope64(sl, cos64, sin64, first_half).astype(BF16)
        off += LANES
    kiw = z1[:, off:off + LANES]
    kir = _rope64(kiw, cos64, sin64, first_half)[:, 0:IDX_DIM]
    ki_ref[0] = kir
    if mode == "prompt":
        kibf_ref[0] = kir.astype(BF16)
    wit_ref[0] = kiw.T[IDX_DIM:IDX_DIM + N_IDX_HEADS, :] * IDX_SCALE

    if mode == "prompt":
        @pl.when(pl.program_id(1) == 0)
        def _():
            carry_ref[...] = jnp.zeros_like(carry_ref)

    row = lax.broadcasted_iota(jnp.int32, (tm, CONV_SLAB), 0)
    for c in range(D_CONV // CONV_SLAB):
        cols = slice(c * CONV_SLAB, (c + 1) * CONV_SLAB)
        bg, cg, xc, ga, gb = (_dot(h, w2_ref[:, k * D_MODEL + c * CONV_SLAB:k * D_MODEL + (c + 1) * CONV_SLAB])
                              for k in range(5))
        u = cg * xc
        r1 = pltpu.roll(u, 1, axis=0)
        r2 = pltpu.roll(u, 2, axis=0)
        if mode == "prompt":
            c0 = carry_ref[0:1, cols]
            c1 = carry_ref[1:2, cols]
            um1 = jnp.where(row == 0, c1, r1)
            um2 = jnp.where(row == 0, c0, jnp.where(row == 1, c1, r2))
            carry_ref[0:2, cols] = u[tm - 2:tm, :]
            cnew_ref[0, :, cols] = u[tm - 2:tm, :]
        else:
            prev = prev_ref[0, :, cols]
            seq_row = row % SUBLANES
            um1 = jnp.where(seq_row == 0, pltpu.roll(prev, tm - 1, axis=0), r1)
            um2 = jnp.where(seq_row < 2, prev, r2)
            u_ref[0, :, cols] = u
        cw = cw_ref[:, cols]
        conv = cw[0:1, :] * um2 + cw[1:2, :] * um1 + cw[2:3, :] * u
        ga_ref[0, :, cols] = jax.nn.sigmoid(ga)
        cm_ref[0, :, cols] = jax.nn.sigmoid(gb) * (bg * conv)


def _const_spec(shape):
    nd = len(shape)
    return pl.BlockSpec(shape, lambda *_: (0,) * nd, pipeline_mode=pl.Buffered(1))


def _inproj(x, norm_g, w1, w2, conv_w, tab, prev, *, mode, tm):
    B, T, _ = x.shape
    assert T % tm == 0 and tm % KEY_STEP == 0 and D_CONV % CONV_SLAB == 0
    nt = T // tm
    tok = lambda w: pl.BlockSpec((1, tm, w), lambda b, t: (b, t, 0))
    in_specs = [tok(D_MODEL), _const_spec((1, D_MODEL)), _const_spec((D_MODEL, W1_COLS)),
                _const_spec((D_MODEL, W2_COLS)), _const_spec((CONV_WIDTH, D_CONV)),
                pl.BlockSpec((tm, HEAD_DIM + IDX_DIM), lambda b, t: (t, 0))]
    args = [x, norm_g.reshape(1, D_MODEL), w1, w2, conv_w, tab]
    kvw = N_KV_HEADS * HEAD_DIM
    qiw = N_IDX_HEADS * IDX_DIM
    wit_spec = pl.BlockSpec((1, N_IDX_HEADS, tm), lambda b, t: (b, 0, t))
    kv_spec = pl.BlockSpec((1, tm * N_KV_HEADS, HEAD_DIM), lambda b, t: (b, t, 0))
    if mode == "prompt":
        out_shape = [
            jax.ShapeDtypeStruct((B, T, D_MODEL), BF16),
            jax.ShapeDtypeStruct((B, T * N_KV_HEADS, HEAD_DIM), F32),
            jax.ShapeDtypeStruct((B, T * N_KV_HEADS, HEAD_DIM), F32),
            jax.ShapeDtypeStruct((B, T, kvw), BF16),
            jax.ShapeDtypeStruct((B, T // KEY_STEP, N_KV_HEADS * VT_ROWS, KEY_STEP), BF16),
            jax.ShapeDtypeStruct((B, T, qiw), BF16),
            jax.ShapeDtypeStruct((B, T, IDX_DIM), F32),
            jax.ShapeDtypeStruct((B, T, IDX_DIM), BF16),
            jax.ShapeDtypeStruct((B, N_IDX_HEADS, T), F32),
            jax.ShapeDtypeStruct((B, T, D_MODEL), F32),
            jax.ShapeDtypeStruct((B, T, D_MODEL), F32),
            jax.ShapeDtypeStruct((B, CONV_WIDTH - 1, D_CONV), F32),
        ]
        out_specs = [tok(D_MODEL), kv_spec, kv_spec, tok(kvw),
                     pl.BlockSpec((1, tm // KEY_STEP, N_KV_HEADS * VT_ROWS, KEY_STEP), lambda b, t: (b, t, 0, 0)),
                     tok(qiw), tok(IDX_DIM), tok(IDX_DIM), wit_spec, tok(D_MODEL), tok(D_MODEL),
                     pl.BlockSpec((1, CONV_WIDTH - 1, D_CONV), lambda b, t: (b, 0, 0))]
        scratch = [pltpu.VMEM((SUBLANES, D_CONV), F32)]
    else:
        in_specs.append(tok(D_CONV))
        args.append(prev)
        out_shape = [
            jax.ShapeDtypeStruct((B, T, D_MODEL), BF16),
            jax.ShapeDtypeStruct((B, T * N_KV_HEADS, HEAD_DIM), F32),
            jax.ShapeDtypeStruct((B, T * N_KV_HEADS, HEAD_DIM), F32),
            jax.ShapeDtypeStruct((B, T, qiw), BF16),
            jax.ShapeDtypeStruct((B, T, IDX_DIM), F32),
            jax.ShapeDtypeStruct((B, N_IDX_HEADS, T), F32),
            jax.ShapeDtypeStruct((B, T, D_MODEL), F32),
            jax.ShapeDtypeStruct((B, T, D_MODEL), F32),
            jax.ShapeDtypeStruct((B, T, D_CONV), F32),
        ]
        out_specs = [tok(D_MODEL), kv_spec, kv_spec, tok(qiw), tok(IDX_DIM), wit_spec,
                     tok(D_MODEL), tok(D_MODEL), tok(D_CONV)]
        scratch = []
    return pl.pallas_call(
        functools.partial(_inproj_kernel, mode=mode, tm=tm),
        grid=(B, nt),
        in_specs=in_specs,
        out_specs=out_specs,
        out_shape=out_shape,
        scratch_shapes=scratch,
        compiler_params=pltpu.CompilerParams(dimension_semantics=("arbitrary", "arbitrary"),
                                             vmem_limit_bytes=VMEM_LIMIT),
        name="inproj_" + mode,
    )(*args)


def _key_value(key):
    return lax.bitcast_convert_type(key ^ ((key >> 31) & 0x7FFFFFFF), F32)


def _search_keys(count_ge, kk, n_all):
    lo0 = jnp.full(kk.shape, KEY_LOWEST, jnp.int32)
    hi0 = jnp.full(kk.shape, KEY_INF, jnp.int32)
    n_lo0 = n_all
    def body(_, c):
        lo, hi, n_lo, n_hi, done = c
        mid = (lo & hi) + ((lo ^ hi) >> 1)
        stuck = mid == lo
        n_mid = count_ge(mid)
        upd = (done < 0.5) & jnp.logical_not(stuck)
        go_lo = n_mid >= kk
        up_lo = upd & go_lo
        lo = jnp.where(up_lo, mid, lo)
        n_lo = jnp.where(up_lo, n_mid, n_lo)
        up_hi = upd & jnp.logical_not(go_lo)
        hi = jnp.where(up_hi, mid, hi)
        n_hi = jnp.where(up_hi, n_mid, n_hi)
        done = jnp.where(stuck | (upd & (n_mid == kk)), 1.0, done)
        return lo, hi, n_lo, n_hi, done

    done0 = jnp.where(n_lo0 <= kk, 1.0, 0.0)
    state = (lo0, hi0, n_lo0, jnp.zeros_like(kk), done0)
    state = lax.fori_loop(0, SEARCH_BLIND, body, state)
    lo, hi, n_lo, n_hi, _ = lax.while_loop(lambda c: jnp.min(c[4]) < 0.5, lambda c: body(0, body(0, c)), state)
    return _key_value(lo), _key_value(hi), n_lo, n_hi


def _resolve_ties(count_ge, kk, n_lo, n_hi, tied, top):
    def by_position():
        return jnp.zeros_like(kk), jnp.where(tied, TINY, -LOWEST), n_hi

    def by_value():
        lo2, hi2, _, n_hi2 = _search_keys(count_ge, kk, n_lo)
        return lo2, hi2, n_hi2

    return lax.cond(jnp.max(jnp.where(tied & (top > 0.0), 1.0, 0.0)) > 0.5, by_value, by_position)


def _spread_ties(sc, lo, hi):
    return jnp.where(sc >= hi, -LOWEST, jnp.where(sc >= lo, sc - lo, NEG_INF))


def _fold_rows(x, op):
    return op(x.reshape(x.shape[0] // SUBLANES, SUBLANES, x.shape[1]), axis=0)


def _prompt_attn_kernel(qi_ref, wit_ref, q_ref, ki_ref, k_ref, vt_ref, o_ref,
                        sc_scr, b_scr, d_a, d_b, s_a, s_b, p_a, p_b, acc_scr, *, n_sel, count_step):
    j = pl.program_id(1)
    per_step = KEY_STEP // Q_BLOCK
    n_steps = (j + per_step) // per_step
    n_pairs = n_steps // 2
    odd = n_steps % 2 == 1
    last = n_steps - 1
    per_count = count_step // KEY_STEP
    n_count = (n_steps + per_count - 1) // per_count
    qi = qi_ref[0]
    q_stack = jnp.concatenate([qi[:, h * IDX_DIM:(h + 1) * IDX_DIM] for h in range(N_IDX_HEADS)], axis=0)
    wit = wit_ref[0]
    q = q_ref[0]
    q_grp = [jnp.concatenate([q[:, (kv * GROUP + g) * HEAD_DIM:(kv * GROUP + g + 1) * HEAD_DIM]
                              for g in range(GROUP)], axis=0) for kv in range(N_KV_HEADS)]
    key_pos = lax.broadcasted_iota(jnp.int32, (KEY_STEP, Q_BLOCK), 0)
    q_pos = lax.broadcasted_iota(jnp.int32, (KEY_STEP, Q_BLOCK), 1) + j * Q_BLOCK

    def step(i):
        return pl.ds(pl.multiple_of(i * KEY_STEP, KEY_STEP), KEY_STEP)

    def idx_dots(i):
        return _dot_nt(ki_ref[0, step(i), :], q_stack)

    def score_trip(i, d_cur, d_nxt):
        d_nxt[...] = idx_dots(jnp.minimum(i + 1, last))
        acc = wit[0:1, :] * jnp.maximum(d_cur[:, 0:Q_BLOCK], 0.0)
        for h in range(1, N_IDX_HEADS):
            acc = acc + wit[h:h + 1, :] * jnp.maximum(d_cur[:, h * Q_BLOCK:(h + 1) * Q_BLOCK], 0.0)
        sc_scr[step(i), :] = jnp.where(key_pos + i * KEY_STEP <= q_pos, acc, NEG_INF)

    def logits(i, kv):
        return _dot_nt(k_ref[0, step(i), kv * HEAD_DIM:(kv + 1) * HEAD_DIM], q_grp[kv])

    d_a[...] = idx_dots(0)
    for kv in range(N_KV_HEADS):
        s_a[kv] = logits(0, kv)

    def score_pair(pi, c):
        score_trip(2 * pi, d_a, d_b)
        score_trip(2 * pi + 1, d_b, d_a)
        return c

    lax.fori_loop(0, n_pairs, score_pair, 0)

    @pl.when(odd)
    def _():
        score_trip(last, d_a, d_b)

    def fill(i, c):
        sc_scr[step(i), :] = jnp.full((KEY_STEP, Q_BLOCK), NEG_INF, F32)
        return c

    lax.fori_loop(n_steps, n_count * per_count, fill, 0)

    def count_ge(t):
        tb = jnp.broadcast_to(_key_value(t), (SUBLANES, Q_BLOCK))

        def body(i, accs):
            accs = list(accs)
            base = pl.multiple_of(i * count_step, count_step)
            for r in range(count_step // SUBLANES):
                sc = sc_scr[pl.ds(base + r * SUBLANES, SUBLANES), :]
                accs[r % COUNT_LANES] = accs[r % COUNT_LANES] + jnp.where(sc >= tb, 1.0, 0.0)
            return tuple(accs)

        accs = lax.fori_loop(0, n_count, body, tuple(jnp.zeros((SUBLANES, Q_BLOCK), F32) for _ in range(COUNT_LANES)))
        return jnp.sum(functools.reduce(lambda a, b: a + b, accs), axis=0, keepdims=True)

    n_valid = (j * Q_BLOCK + 1 + lax.broadcasted_iota(jnp.int32, (1, Q_BLOCK), 1)).astype(F32)
    kk = jnp.minimum(n_valid, float(n_sel))
    lo, hi, n_lo, n_hi = lax.cond((j + 1) * Q_BLOCK <= n_sel,
                                  lambda: (jnp.full_like(kk, LOWEST), jnp.full_like(kk, jnp.inf), n_valid,
                                           jnp.zeros_like(kk)),
                                  lambda: _search_keys(count_ge, kk, n_valid))

    def mask_plain(i, c):
        b_scr[step(i), :] = jnp.where(sc_scr[step(i), :] >= lo, 0.0, NEG_INF)
        return c

    def spread(i, top):
        sp = _spread_ties(sc_scr[step(i), :], lo, hi)
        sc_scr[step(i), :] = sp
        return jnp.maximum(top, _fold_rows(jnp.where(sp < -LOWEST, sp, NEG_INF), jnp.max))

    tied = n_lo > kk
    any_tie = jnp.max(jnp.where(tied, 1.0, 0.0)) > 0.5

    @pl.when(any_tie)
    def _():
        top = lax.fori_loop(0, n_steps, spread, jnp.full((SUBLANES, Q_BLOCK), NEG_INF, F32))
        lo2, hi2, n_hi2 = _resolve_ties(count_ge, kk, n_lo, n_hi, tied, jnp.max(top, axis=0, keepdims=True))
        tri = jnp.where(lax.broadcasted_iota(jnp.int32, (KEY_STEP, KEY_STEP), 1)
                        <= lax.broadcasted_iota(jnp.int32, (KEY_STEP, KEY_STEP), 0), 1.0, 0.0).astype(BF16)

        def mask_ties(i, run):
            sc = sc_scr[step(i), :]
            above = sc >= hi2
            elig = (sc >= lo2) & jnp.logical_not(above)
            rank = run + _dot(tri, jnp.where(elig, 1.0, 0.0).astype(BF16))
            b_scr[step(i), :] = jnp.where(above | (elig & (rank <= kk - n_hi2)), 0.0, NEG_INF)
            return rank[KEY_STEP - 1:KEY_STEP, :]

        lax.fori_loop(0, n_steps, mask_ties, jnp.zeros((1, Q_BLOCK), F32))

    @pl.when(jnp.logical_not(any_tie))
    def _():
        lax.fori_loop(0, n_steps, mask_plain, 0)

    def pv_dot(i, kv, p_ref):
        return _dot(vt_ref[0, i, kv * VT_ROWS:(kv + 1) * VT_ROWS, :], p_ref[kv])

    def attn_trip(i, s_cur, s_nxt, p_cur, p_prv, m):
        pv = [pv_dot(jnp.maximum(i - 1, 0), kv, p_prv) for kv in range(N_KV_HEADS)]
        for kv in range(N_KV_HEADS):
            s_nxt[kv] = logits(jnp.minimum(i + 1, last), kv)
        b = b_scr[step(i), :]
        m_rows = []
        for kv in range(N_KV_HEADS):
            ps, alphas = [], []
            for g in range(GROUP):
                h = kv * GROUP + g
                sg = s_cur[kv, :, g * Q_BLOCK:(g + 1) * Q_BLOCK] + b
                m_old = m[h:h + 1, :]
                m_new = jnp.maximum(m_old, jnp.max(_fold_rows(sg, jnp.max), axis=0, keepdims=True))
                ps.append(jnp.exp2(sg - m_new).astype(BF16))
                alphas.append(jnp.exp2(m_old - m_new))
                m_rows.append(m_new)
            p_cur[kv] = jnp.concatenate(ps, axis=1)
            acc_scr[kv] = (acc_scr[kv] + pv[kv]) * jnp.concatenate(alphas, axis=1)
        return jnp.concatenate(m_rows, axis=0)

    acc_scr[...] = jnp.zeros_like(acc_scr)
    p_b[...] = jnp.zeros_like(p_b)

    def attn_pair(pi, m):
        m = attn_trip(2 * pi, s_a, s_b, p_a, p_b, m)
        return attn_trip(2 * pi + 1, s_b, s_a, p_b, p_a, m)

    m_end = lax.fori_loop(0, n_pairs, attn_pair, jnp.full((N_HEADS, Q_BLOCK), M_FLOOR, F32))

    @pl.when(odd)
    def _():
        attn_trip(last, s_a, s_b, p_a, p_b, m_end)
        p_b[...] = p_a[...]

    for h in range(N_HEADS):
        kv, g = divmod(h, GROUP)
        if g == 0:
            acc_kv = acc_scr[kv] + pv_dot(last, kv, p_b)
        cols = slice(g * Q_BLOCK, (g + 1) * Q_BLOCK)
        inv_l = 1.0 / acc_kv[HEAD_DIM:HEAD_DIM + 1, cols]
        o_ref[0, :, h * HEAD_DIM:(h + 1) * HEAD_DIM] = (acc_kv[0:HEAD_DIM, cols] * inv_l).T


def _prompt_attention(qi, wit, q, ki_bf, k_bf, vt):
    B, T, _ = q.shape
    nq = T // Q_BLOCK
    n_sel = min(TOPK_MAX, T // 4)
    kvw = N_KV_HEADS * HEAD_DIM
    assert T % KEY_STEP == 0 and KEY_STEP % Q_BLOCK == 0
    count_step = next(c for c in (COUNT_STEP, COUNT_STEP // 2, KEY_STEP) if T % c == 0)
    gq = GROUP * Q_BLOCK
    return pl.pallas_call(
        functools.partial(_prompt_attn_kernel, n_sel=n_sel, count_step=count_step),
        grid=(B, nq),
        in_specs=[pl.BlockSpec((1, Q_BLOCK, N_IDX_HEADS * IDX_DIM), lambda b, j: (b, j, 0)),
                  pl.BlockSpec((1, N_IDX_HEADS, Q_BLOCK), lambda b, j: (b, 0, j)),
                  pl.BlockSpec((1, Q_BLOCK, D_MODEL), lambda b, j: (b, j, 0)),
                  pl.BlockSpec((1, T, IDX_DIM), lambda b, j: (b, 0, 0)),
                  pl.BlockSpec((1, T, kvw), lambda b, j: (b, 0, 0)),
                  pl.BlockSpec((1, T // KEY_STEP, N_KV_HEADS * VT_ROWS, KEY_STEP), lambda b, j: (b, 0, 0, 0))],
        out_specs=pl.BlockSpec((1, Q_BLOCK, D_MODEL), lambda b, j: (b, j, 0)),
        out_shape=jax.ShapeDtypeStruct((B, T, D_MODEL), F32),
        scratch_shapes=[pltpu.VMEM((T, Q_BLOCK), F32), pltpu.VMEM((T, Q_BLOCK), F32),
                        pltpu.VMEM((KEY_STEP, N_IDX_HEADS * Q_BLOCK), F32),
                        pltpu.VMEM((KEY_STEP, N_IDX_HEADS * Q_BLOCK), F32),
                        pltpu.VMEM((N_KV_HEADS, KEY_STEP, gq), F32), pltpu.VMEM((N_KV_HEADS, KEY_STEP, gq), F32),
                        pltpu.VMEM((N_KV_HEADS, KEY_STEP, gq), BF16), pltpu.VMEM((N_KV_HEADS, KEY_STEP, gq), BF16),
                        pltpu.VMEM((N_KV_HEADS, VT_ROWS, gq), F32)],
        compiler_params=pltpu.CompilerParams(dimension_semantics=("arbitrary", "arbitrary"),
                                             vmem_limit_bytes=VMEM_LIMIT),
        name="prompt_attention",
    )(qi, wit, q, ki_bf, k_bf, vt)


def _sample_score_kernel(pt_ref, qi_ref, wcol_ref, kin_ref, *rest, n_pages, ts, n_seq):
    page_refs, s_ref = rest[:n_seq * n_pages], rest[n_seq * n_pages]
    lane = lax.broadcasted_iota(jnp.int32, (ts, PAGE_SIZE), 1)
    row = lax.broadcasted_iota(jnp.int32, (ts, PAGE_SIZE), 0)
    for q in range(n_seq):
        qi = qi_ref[q].astype(F32)
        q_all = jnp.concatenate([qi[:, h * IDX_DIM:(h + 1) * IDX_DIM] for h in range(N_IDX_HEADS)],
                                axis=0).astype(BF16)
        wcol = wcol_ref[q]

        def score(dots, wcol=wcol):
            r = jnp.maximum(dots, 0.0) * wcol
            acc = r[0:ts]
            for h in range(1, N_IDX_HEADS):
                acc = acc + r[h * ts:(h + 1) * ts]
            return acc

        past = jnp.concatenate([page_refs[q * n_pages + p][0] for p in range(n_pages)], axis=1).astype(BF16)
        s_ref[q, :, 0:n_pages * PAGE_SIZE] = score(_dot(q_all, past))
        new_keys = jnp.concatenate([kin_ref[q], jnp.zeros((PAGE_SIZE - ts, IDX_DIM), F32)], axis=0).astype(BF16)
        s_ref[q, :, n_pages * PAGE_SIZE:(n_pages + 1) * PAGE_SIZE] = jnp.where(
            lane <= row, score(_dot_nt(q_all, new_keys)), NEG_INF)


def _sample_scores(page_table, qi, wcol, ki_bf, cache_kidx):
    nb, ts, _ = qi.shape
    n_pages = page_table.shape[1]
    n_seq = SCORE_SEQS if nb % SCORE_SEQS == 0 else 1

    def page_spec(q, p):
        return pl.BlockSpec((1, IDX_DIM, PAGE_SIZE), lambda b, pt: (pt[b * n_seq + q, p], 0, 0))

    grid_spec = pltpu.PrefetchScalarGridSpec(
        num_scalar_prefetch=1,
        grid=(nb // n_seq,),
        in_specs=[pl.BlockSpec((n_seq, ts, N_IDX_HEADS * IDX_DIM), lambda b, pt: (b, 0, 0)),
                  pl.BlockSpec((n_seq, N_IDX_HEADS * ts, 1), lambda b, pt: (b, 0, 0)),
                  pl.BlockSpec((n_seq, ts, IDX_DIM), lambda b, pt: (b, 0, 0))]
                 + [page_spec(q, p) for q in range(n_seq) for p in range(n_pages)],
        out_specs=pl.BlockSpec((n_seq, ts, (n_pages + 1) * PAGE_SIZE), lambda b, pt: (b, 0, 0)),
    )
    return pl.pallas_call(
        functools.partial(_sample_score_kernel, n_pages=n_pages, ts=ts, n_seq=n_seq),
        grid_spec=grid_spec,
        out_shape=jax.ShapeDtypeStruct((nb, ts, (n_pages + 1) * PAGE_SIZE), F32),
        compiler_params=pltpu.CompilerParams(dimension_semantics=("arbitrary",), vmem_limit_bytes=VMEM_LIMIT),
        name="sample_scores",
    )(page_table, qi, wcol, ki_bf, *([cache_kidx] * (n_seq * n_pages)))


def _sample_select_kernel(s_ref, b_ref, sc_scr, *, n_sel, n_chunks):
    rows = s_ref.shape[0]

    def cols(c):
        return slice(c * LANES, (c + 1) * LANES)

    def counter(ref):
        def count_ge(t):
            tf = _key_value(t)
            acc = jnp.where(ref[:, cols(0)] >= tf, 1.0, 0.0)
            for c in range(1, n_chunks):
                acc = acc + jnp.where(ref[:, cols(c)] >= tf, 1.0, 0.0)
            return jnp.sum(acc, axis=1, keepdims=True)
        return count_ge

    kk = jnp.full((rows, 1), float(n_sel), F32)
    n_all = counter(s_ref)(jnp.full((rows, 1), KEY_LOWEST, jnp.int32))
    lo, hi, n_lo, n_hi = _search_keys(counter(s_ref), kk, n_all)
    tied = n_lo > kk
    any_tie = jnp.max(jnp.where(tied, 1.0, 0.0)) > 0.5

    @pl.when(jnp.logical_not(any_tie))
    def _():
        for c in range(n_chunks):
            b_ref[:, cols(c)] = jnp.where(s_ref[:, cols(c)] >= lo, 0.0, NEG_INF)

    @pl.when(any_tie)
    def _():
        top = jnp.full((rows, LANES), NEG_INF, F32)
        for c in range(n_chunks):
            sp = _spread_ties(s_ref[:, cols(c)], lo, hi)
            sc_scr[:, cols(c)] = sp
            top = jnp.maximum(top, jnp.where(sp < -LOWEST, sp, NEG_INF))
        lo2, hi2, n_hi2 = _resolve_ties(counter(sc_scr), kk, n_lo, n_hi, tied, jnp.max(top, axis=1, keepdims=True))
        tri = jnp.where(lax.broadcasted_iota(jnp.int32, (LANES, LANES), 0)
                        <= lax.broadcasted_iota(jnp.int32, (LANES, LANES), 1), 1.0, 0.0).astype(BF16)
        run = jnp.zeros((rows, 1), F32)
        for c in range(n_chunks):
            sc = sc_scr[:, cols(c)]
            above = sc >= hi2
            elig = (sc >= lo2) & jnp.logical_not(above)
            rank = run + _dot(jnp.where(elig, 1.0, 0.0).astype(BF16), tri)
            b_ref[:, cols(c)] = jnp.where(above | (elig & (rank <= kk - n_hi2)), 0.0, NEG_INF)
            run = rank[:, LANES - 1:LANES]


def _sample_select(scores, n_sel, rows_per_step):
    rows, width = scores.shape
    assert rows % rows_per_step == 0 and width % LANES == 0
    return pl.pallas_call(
        functools.partial(_sample_select_kernel, n_sel=n_sel, n_chunks=width // LANES),
        grid=(rows // rows_per_step,),
        in_specs=[pl.BlockSpec((rows_per_step, width), lambda i: (i, 0))],
        out_specs=pl.BlockSpec((rows_per_step, width), lambda i: (i, 0)),
        out_shape=jax.ShapeDtypeStruct((rows, width), F32),
        scratch_shapes=[pltpu.VMEM((rows_per_step, width), F32)],
        compiler_params=pltpu.CompilerParams(dimension_semantics=("arbitrary",), vmem_limit_bytes=VMEM_LIMIT),
        name="sample_select",
    )(scores)


def _sample_attn_kernel(pt_ref, q_ref, b_ref, kn_ref, vn_ref, *rest, n_pages, ts, n_seq):
    k_refs, v_refs, o_ref = rest[:n_seq * n_pages], rest[n_seq * n_pages:2 * n_seq * n_pages], rest[2 * n_seq * n_pages]
    pad = jnp.zeros((PAGE_SIZE - ts, HEAD_DIM), F32)

    def head_rows(refs, new_ref, u, kv):
        tiles = [r[0, pl.ds(kv, PAGE_SIZE, stride=N_KV_HEADS), :] for r in refs[u * n_pages:(u + 1) * n_pages]]
        tiles += [new_ref[u, pl.ds(kv, ts, stride=N_KV_HEADS), :], pad]
        return jnp.concatenate(tiles, axis=0).astype(BF16)

    units = [(u, kv) for u in range(n_seq) for kv in range(N_KV_HEADS)]
    s = []
    for u, kv in units:
        q = q_ref[u].astype(F32)
        qg = jnp.concatenate([q[:, (kv * GROUP + g) * HEAD_DIM:(kv * GROUP + g + 1) * HEAD_DIM]
                              for g in range(GROUP)], axis=0).astype(BF16)
        bias = jnp.concatenate([b_ref[u]] * GROUP, axis=0)
        s.append(_dot_nt(qg, head_rows(k_refs, kn_ref, u, kv)) + bias)
    p_ = [jnp.exp2(x - jnp.max(x, axis=1, keepdims=True)) for x in s]
    for (u, kv), pr in zip(units, p_):
        o = _dot(pr.astype(BF16), head_rows(v_refs, vn_ref, u, kv)) / jnp.sum(pr, axis=1, keepdims=True)
        for g in range(GROUP):
            hh = kv * GROUP + g
            o_ref[u, :, hh * HEAD_DIM:(hh + 1) * HEAD_DIM] = o[g * ts:(g + 1) * ts, :]


def _sample_attention(page_table, q, bias, k_new, v_new, cache_k, cache_v):
    nb, ts, _ = q.shape
    n_pages = page_table.shape[1]
    n_seq = ATTN_SEQS if nb % ATTN_SEQS == 0 else 1

    def page_spec(u, p):
        return pl.BlockSpec((1, PAGE_SIZE * N_KV_HEADS, HEAD_DIM), lambda b, pt: (pt[b * n_seq + u, p], 0, 0))

    pages = [page_spec(u, p) for u in range(n_seq) for p in range(n_pages)]
    grid_spec = pltpu.PrefetchScalarGridSpec(
        num_scalar_prefetch=1,
        grid=(nb // n_seq,),
        in_specs=[pl.BlockSpec((n_seq, ts, D_MODEL), lambda b, pt: (b, 0, 0)),
                  pl.BlockSpec((n_seq, ts, (n_pages + 1) * PAGE_SIZE), lambda b, pt: (b, 0, 0)),
                  pl.BlockSpec((n_seq, ts * N_KV_HEADS, HEAD_DIM), lambda b, pt: (b, 0, 0)),
                  pl.BlockSpec((n_seq, ts * N_KV_HEADS, HEAD_DIM), lambda b, pt: (b, 0, 0))] + pages * 2,
        out_specs=pl.BlockSpec((n_seq, ts, D_MODEL), lambda b, pt: (b, 0, 0)),
    )
    return pl.pallas_call(
        functools.partial(_sample_attn_kernel, n_pages=n_pages, ts=ts, n_seq=n_seq),
        grid_spec=grid_spec,
        out_shape=jax.ShapeDtypeStruct((nb, ts, D_MODEL), F32),
        compiler_params=pltpu.CompilerParams(dimension_semantics=("arbitrary",), vmem_limit_bytes=VMEM_LIMIT),
        name="sample_attention",
    )(page_table, q, bias, k_new, v_new, *([cache_k] * (n_seq * n_pages)), *([cache_v] * (n_seq * n_pages)))


def _ffn_kernel(x_ref, oa_ref, ga_ref, cm_ref, p_ref, wout_ref, gffn_ref, wg_ref, wu_ref, wd_ref,
                gple_ref, wple_ref, wpg_ref, gfin_ref, y_ref, x1_scr, h_scr, acc_scr):
    f = pl.program_id(1)

    @pl.when(f == 0)
    def _():
        merged = ga_ref[...] * oa_ref[...] + cm_ref[...]
        x1 = x_ref[...] + _dot(merged.astype(BF16), wout_ref[...])
        x1_scr[...] = x1
        h_scr[...] = _rmsnorm(x1, gffn_ref[...]).astype(BF16)
        acc_scr[...] = jnp.zeros_like(acc_scr)

    h = h_scr[...]
    tf = wg_ref.shape[1]
    part = None
    for c0 in range(0, tf, FFN_SLAB):
        c1 = min(c0 + FFN_SLAB, tf)
        g = _dot(h, wg_ref[:, c0:c1])
        u = _dot(h, wu_ref[:, c0:c1])
        t = _dot((g * jax.nn.sigmoid(g) * u).astype(BF16), wd_ref[c0:c1, :])
        part = t if part is None else part + t
    acc_scr[...] += part

    @pl.when(f == pl.num_programs(1) - 1)
    def _():
        x2 = x1_scr[...] + acc_scr[...]
        gate = jax.nn.sigmoid(_dot(_rmsnorm(x2, gple_ref[...]).astype(BF16), wpg_ref[...]))
        x3 = x2 + _dot(p_ref[...].astype(BF16), wple_ref[...]) * gate
        y_ref[...] = _rmsnorm(x3, gfin_ref[...])


def _ffn(x, oa, ga, cm, p, wout, gffn, wg, wu, wd, gple, wple, wpg, gfin, *, tm, tf):
    n = x.shape[0]
    tm = min(tm, n)
    assert n % tm == 0 and D_FF % tf == 0
    tok = lambda w: pl.BlockSpec((tm, w), lambda i, f: (i, 0))
    vec = lambda a: a.reshape(1, D_MODEL)
    return pl.pallas_call(
        _ffn_kernel,
        grid=(n // tm, D_FF // tf),
        in_specs=[tok(D_MODEL), tok(D_MODEL), tok(D_MODEL), tok(D_MODEL), tok(D_PLE),
                  _const_spec((D_MODEL, D_MODEL)), _const_spec((1, D_MODEL)),
                  pl.BlockSpec((D_MODEL, tf), lambda i, f: (0, f)),
                  pl.BlockSpec((D_MODEL, tf), lambda i, f: (0, f)),
                  pl.BlockSpec((tf, D_MODEL), lambda i, f: (f, 0)),
                  _const_spec((1, D_MODEL)), _const_spec((D_PLE, D_MODEL)), _const_spec((D_MODEL, D_MODEL)),
                  _const_spec((1, D_MODEL))],
        out_specs=tok(D_MODEL),
        out_shape=jax.ShapeDtypeStruct((n, D_MODEL), F32),
        scratch_shapes=[pltpu.VMEM((tm, D_MODEL), F32), pltpu.VMEM((tm, D_MODEL), BF16),
                        pltpu.VMEM((tm, D_MODEL), F32)],
        compiler_params=pltpu.CompilerParams(dimension_semantics=("arbitrary", "arbitrary"),
                                             vmem_limit_bytes=VMEM_LIMIT),
        name="ffn",
    )(x, oa, ga, cm, p, wout, vec(gffn), wg, wu, wd, vec(gple), wple, wpg, vec(gfin))


def _split_w_in(w_in):
    n_att = N_HEADS * HEAD_DIM + 2 * N_KV_HEADS * HEAD_DIM + N_IDX_HEADS * IDX_DIM + IDX_DIM + N_IDX_HEADS
    return w_in[:, :W1_COLS].astype(BF16), w_in[:, n_att:].astype(BF16)


def kernel(x_prompt, x_sample, cache_k, cache_v, cache_kidx, state_conv, page_table, p_prompt, p_sample, norm_mix, w_in, conv_w, w_out, norm_ffn, w_gate_up, w_down, norm_ple, w_ple, w_ple_gate, norm_final):
    Bp, Tp, _ = x_prompt.shape
    Bs, Ts, _ = x_sample.shape
    n_pages = page_table.shape[1]
    past_len = n_pages * PAGE_SIZE
    n_phys = cache_k.shape[1]
    assert w_in.shape[0] == 1, "one layer: the per-layer tensors are indexed at l = 0 below"
    assert Ts == SUBLANES and cache_k.shape[2] == PAGE_SIZE
    l = 0

    w1, w2 = _split_w_in(w_in[l])
    wout = w_out[l].astype(BF16)
    wg = w_gate_up[l][:, :D_FF].astype(BF16)
    wu = w_gate_up[l][:, D_FF:].astype(BF16)
    wd = w_down[l].astype(BF16)
    wple = w_ple[l].astype(BF16)
    wpg = w_ple_gate[l].astype(BF16)

    tab_p = _rope_table(jnp.arange(Tp))
    (q_p, k_p, v_p, kbf_p, vt_p, qi_p, ki_p, kibf_p, wit_p, ga_p, cm_p, cnew_p) = _inproj(
        x_prompt, norm_mix[l], w1, w2, conv_w[l], tab_p, None, mode="prompt", tm=min(INPROJ_ROWS, Tp))
    oa_p = _prompt_attention(qi_p, wit_p, q_p, kibf_p, kbf_p, vt_p)
    n_p = Bp * Tp
    y_p = _ffn(x_prompt.reshape(n_p, D_MODEL), oa_p.reshape(n_p, D_MODEL), ga_p.reshape(n_p, D_MODEL),
               cm_p.reshape(n_p, D_MODEL), p_prompt[l].reshape(n_p, D_PLE), wout, norm_ffn[l], wg, wu, wd,
               norm_ple[l], wple, wpg, norm_final, tm=512, tf=D_FF // 2)

    n_s = Bs * Ts
    tm_s = min(INPROJ_ROWS, n_s)
    tab_s = _rope_table(past_len + (jnp.arange(tm_s) % Ts))
    prev = jnp.concatenate([state_conv[l], jnp.zeros((Bs, Ts - (CONV_WIDTH - 1), D_CONV), F32)], axis=1)
    (q_s, k_s, v_s, qi_s, ki_s, wit_s, ga_s, cm_s, u_s) = _inproj(
        x_sample.reshape(n_s // tm_s, tm_s, D_MODEL), norm_mix[l], w1, w2, conv_w[l], tab_s,
        prev.reshape(n_s // tm_s, tm_s, D_CONV), mode="sample", tm=tm_s)
    wcol = wit_s.transpose(1, 0, 2).reshape(N_IDX_HEADS, Bs, Ts).transpose(1, 0, 2).reshape(Bs, N_IDX_HEADS * Ts, 1)
    scores = _sample_scores(page_table, qi_s.reshape(Bs, Ts, -1), wcol, ki_s.reshape(Bs, Ts, IDX_DIM),
                            jnp.swapaxes(cache_kidx[l], 1, 2))
    n_sel = min(TOPK_MAX, (past_len + Ts) // 4)
    bias = _sample_select(scores.reshape(n_s, -1), n_sel, 256)
    oa_s = _sample_attention(page_table, q_s.reshape(Bs, Ts, D_MODEL), bias.reshape(Bs, Ts, -1),
                             k_s.reshape(Bs, Ts * N_KV_HEADS, HEAD_DIM), v_s.reshape(Bs, Ts * N_KV_HEADS, HEAD_DIM),
                             cache_k[l].reshape(n_phys, PAGE_SIZE * N_KV_HEADS, HEAD_DIM),
                             cache_v[l].reshape(n_phys, PAGE_SIZE * N_KV_HEADS, HEAD_DIM))
    y_s = _ffn(x_sample.reshape(n_s, D_MODEL), oa_s.reshape(n_s, D_MODEL), ga_s.reshape(n_s, D_MODEL),
               cm_s.reshape(n_s, D_MODEL), p_sample[l].reshape(n_s, D_PLE), wout, norm_ffn[l], wg, wu, wd,
               norm_ple[l], wple, wpg, norm_final, tm=512, tf=D_FF // 2)

    return (y_p.reshape(Bp, Tp, D_MODEL), y_s.reshape(Bs, Ts, D_MODEL),
            k_p.reshape(1, Bp, Tp, N_KV_HEADS, HEAD_DIM), v_p.reshape(1, Bp, Tp, N_KV_HEADS, HEAD_DIM),
            ki_p.reshape(1, Bp, Tp, IDX_DIM), cnew_p.reshape(1, Bp, CONV_WIDTH - 1, D_CONV),
            k_s.reshape(1, Bs, Ts, N_KV_HEADS, HEAD_DIM), v_s.reshape(1, Bs, Ts, N_KV_HEADS, HEAD_DIM),
            ki_s.reshape(1, Bs, Ts, IDX_DIM),
            u_s.reshape(Bs, Ts, D_CONV)[:, Ts - (CONV_WIDTH - 1):, :].reshape(1, Bs, CONV_WIDTH - 1, D_CONV))
```
